```python
import math
import jax, jax.numpy as jnp
from jax import lax
import numpy as np

D_MODEL = 1024
BATCH = 2
SEQ = 8192
DEPTH = 2

ALPHA = (2 * DEPTH) ** 0.25
BETA = (8 * DEPTH) ** -0.25
N_EVEN = (DEPTH + 1) // 2
N_ODD = DEPTH // 2

CHUNK = 128
A_WIDTH = D_MODEL // 2
A_GROUPS = 4
A_GROUP_DIM = A_WIDTH // A_GROUPS

B_HEADS = 8
B_KV_HEADS = 2
B_HEAD_DIM = (D_MODEL // 2) // B_HEADS
CMP_STRIDE = 16
CMP_BLOCK = 2 * CMP_STRIDE
SEL_BLOCK = 64
N_SELECT = 16
WINDOW = 512
Q_BLOCK = 128

N_BUCKETS = 32
MAX_DISTANCE = 1024

CONV_WIDTH = 31

FFN_HIDDEN = ((8 * D_MODEL // 3 + 255) // 256) * 256

PROJ_WIDTH = 2 * A_WIDTH + B_HEADS * B_HEAD_DIM + 6 * B_KV_HEADS * B_HEAD_DIM + 3 * B_HEADS

NEG_INF = -1e30
FORCE = 1e9

kernel_name = "hybrid_gmlp_nsa_conformer_deepnorm"


def layer_norm(x, g, b, eps=1e-5):
    xf = x.astype(jnp.float32)
    mu = jnp.mean(xf, axis=-1, keepdims=True)
    var = jnp.mean(jnp.square(xf - mu), axis=-1, keepdims=True)
    y = (xf - mu) * lax.rsqrt(var + eps)
    return (y * g.astype(jnp.float32) + b.astype(jnp.float32)).astype(x.dtype)


def t5_bucket(dist):
    n = jnp.maximum(dist, 0)
    max_exact = N_BUCKETS // 2
    nf = jnp.maximum(n, 1).astype(jnp.float32)
    large = max_exact + (jnp.log(nf / max_exact) / math.log(MAX_DISTANCE / max_exact)
                         * (N_BUCKETS - max_exact)).astype(jnp.int32)
    large = jnp.minimum(large, N_BUCKETS - 1)
    return jnp.where(n < max_exact, n, large)


def spatial_gating(u, v, ln_g, ln_b, w_s, b_s):
    B_, S, _ = v.shape
    v = layer_norm(v, ln_g, ln_b)
    vr = v.reshape(B_, S // CHUNK, CHUNK, A_GROUPS, A_GROUP_DIM)
    mask = jnp.tril(jnp.ones((CHUNK, CHUNK), dtype=bool))
    w = jnp.where(mask[None], w_s, jnp.zeros_like(w_s))
    s = jnp.einsum('gts,bcsgd->bctgd', w, vr) + b_s.T[None, None, :, :, None]
    return u * s.reshape(B_, S, A_WIDTH)


def compress(k, pe, w1, w2):
    B_, S, Hkv, dh = k.shape
    kr = k.reshape(B_, S // CMP_STRIDE, CMP_STRIDE, Hkv, dh)
    blocks = jnp.concatenate([kr[:, :-1], kr[:, 1:]], axis=2)
    blocks = blocks + pe[None, None, :, None, :]
    n_cmp = blocks.shape[1]
    flat = blocks.transpose(0, 1, 3, 2, 4).reshape(B_, n_cmp, Hkv, CMP_BLOCK * dh)
    return jax.nn.gelu(flat @ w1) @ w2


def nsa(q, k_cmp, v_cmp, k_slc, v_slc, k_win, v_win, gates, rel_bias,
        pe_k, w1_k, w2_k, pe_v, w1_v, w2_v):
    B_, S, H, dh = q.shape
    Hkv = k_cmp.shape[2]
    G = H // Hkv
    scale = dh ** -0.5

    kc = compress(k_cmp, pe_k, w1_k, w2_k)
    vc = compress(v_cmp, pe_v, w1_v, w2_v)
    n_cmp = kc.shape[1]
    cmp_start = jnp.arange(n_cmp, dtype=jnp.int32) * CMP_STRIDE
    cmp_end = cmp_start + CMP_BLOCK - 1

    n_sel = S // SEL_BLOCK
    ks_blk = k_slc.reshape(B_, n_sel, SEL_BLOCK, Hkv, dh).transpose(0, 3, 1, 2, 4)
    vs_blk = v_slc.reshape(B_, n_sel, SEL_BLOCK, Hkv, dh).transpose(0, 3, 1, 2, 4)
    sel_start = jnp.arange(n_sel, dtype=jnp.int32) * SEL_BLOCK
    overlap = ((cmp_start[:, None] < sel_start[None] + SEL_BLOCK)
               & (cmp_start[:, None] + CMP_BLOCK > sel_start[None])).astype(jnp.float32)
    n_top = min(N_SELECT, n_sel)

    kw_pad = jnp.pad(k_win, ((0, 0), (WINDOW, 0), (0, 0), (0, 0)))
    vw_pad = jnp.pad(v_win, ((0, 0), (WINDOW, 0), (0, 0), (0, 0)))

    tbl = rel_bias.reshape(N_BUCKETS, Hkv, G).transpose(1, 0, 2)
    b_idx = jnp.arange(B_)[:, None, None, None]
    h_idx = jnp.arange(Hkv)[None, :, None, None]

    n_blk = S // Q_BLOCK
    q_blocks = q.reshape(B_, n_blk, Q_BLOCK, Hkv, G, dh).transpose(1, 0, 2, 3, 4, 5)
    g_blocks = gates.reshape(B_, n_blk, Q_BLOCK, H, 3).transpose(1, 0, 2, 3, 4)

    def block_fn(args):
        qb, gb, ib = args
        t0 = ib * Q_BLOCK
        tq = t0 + jnp.arange(Q_BLOCK, dtype=jnp.int32)
        qh = qb.transpose(0, 2, 3, 1, 4) * scale

        dist_c = tq[:, None] - cmp_end[None]
        valid_c = dist_c >= 0
        bias_c = tbl[:, t5_bucket(dist_c), :].transpose(0, 3, 1, 2)
        lc = jnp.einsum('bhgqd,bchd->bhgqc', qh, kc).astype(jnp.float32)
        lc = jnp.where(valid_c, lc + bias_c.astype(jnp.float32), NEG_INF)
        pc = jax.nn.softmax(lc, axis=-1) * jnp.any(valid_c, axis=-1)[:, None].astype(jnp.float32)
        o_c = jnp.einsum('bhgqc,bchd->bhgqd', pc.astype(vc.dtype), vc)

        imp = jnp.einsum('bhgqc,cs->bhqs', pc, overlap)
        jq = (tq // SEL_BLOCK)[:, None]
        js = jnp.arange(n_sel, dtype=jnp.int32)[None]
        imp = jnp.where(js > jq, NEG_INF, imp)
        imp = jnp.where((js == 0) | (js == jq) | (js == jq - 1), FORCE, imp)
        _, idx = lax.top_k(imp, n_top)

        ks = ks_blk[b_idx, h_idx, idx].reshape(B_, Hkv, Q_BLOCK, n_top * SEL_BLOCK, dh)
        vs = vs_blk[b_idx, h_idx, idx].reshape(B_, Hkv, Q_BLOCK, n_top * SEL_BLOCK, dh)
        pos_s = (idx[..., None] * SEL_BLOCK + jnp.arange(SEL_BLOCK, dtype=jnp.int32)).reshape(
            B_, Hkv, Q_BLOCK, n_top * SEL_BLOCK)
        dist_s = tq[None, None, :, None] - pos_s
        bias_s = tbl[h_idx, t5_bucket(dist_s)].transpose(0, 1, 4, 2, 3)
        ls = jnp.einsum('bhgqd,bhqkd->bhgqk', qh, ks).astype(jnp.float32)
        ls = jnp.where((dist_s >= 0)[:, :, None], ls + bias_s.astype(jnp.float32), NEG_INF)
        o_s = jnp.einsum('bhgqk,bhqkd->bhgqd', jax.nn.softmax(ls, axis=-1).astype(vs.dtype), vs)

        kw = lax.dynamic_slice_in_dim(kw_pad, t0, Q_BLOCK + WINDOW, axis=1)
        vw = lax.dynamic_slice_in_dim(vw_pad, t0, Q_BLOCK + WINDOW, axis=1)
        pos_w = t0 - WINDOW + jnp.arange(Q_BLOCK + WINDOW, dtype=jnp.int32)
        dist_w = tq[:, None] - pos_w[None]
        valid_w = (dist_w >= 0) & (dist_w < WINDOW) & (pos_w[None] >= 0)
        bias_w = tbl[:, t5_bucket(dist_w), :].transpose(0, 3, 1, 2)
        lw = jnp.einsum('bhgqd,bkhd->bhgqk', qh, kw).astype(jnp.float32)
        lw = jnp.where(valid_w, lw + bias_w.astype(jnp.float32), NEG_INF)
        o_w = jnp.einsum('bhgqk,bkhd->bhgqd', jax.nn.softmax(lw, axis=-1).astype(vw.dtype), vw)

        g = gb.reshape(B_, Q_BLOCK, Hkv, G, 3).transpose(0, 2, 3, 1, 4)
        o = g[..., 0:1] * o_c + g[..., 1:2] * o_s + g[..., 2:3] * o_w
        return o.transpose(0, 3, 1, 2, 4).reshape(B_, Q_BLOCK, H * dh)

    out = lax.map(block_fn, (q_blocks, g_blocks, jnp.arange(n_blk, dtype=jnp.int32)))
    return out.transpose(1, 0, 2, 3).reshape(B_, S, H * dh)


def hybrid_mixer(x, w_in, w_out, ln_g, ln_b, w_s, b_s, rel_bias,
                 pe_k, w1_k, w2_k, pe_v, w1_v, w2_v):
    B_, S, _ = x.shape
    h = x @ w_in
    QW = B_HEADS * B_HEAD_DIM
    KV = B_KV_HEADS * B_HEAD_DIM
    uv = jax.nn.gelu(h[..., :2 * A_WIDTH])
    a_out = spatial_gating(uv[..., :A_WIDTH], uv[..., A_WIDTH:], ln_g, ln_b, w_s, b_s)
    o = 2 * A_WIDTH
    q = h[..., o:o + QW].reshape(B_, S, B_HEADS, B_HEAD_DIM)
    o += QW
    kvs = []
    for _ in range(6):
        kvs.append(h[..., o:o + KV].reshape(B_, S, B_KV_HEADS, B_HEAD_DIM))
        o += KV
    gates = jax.nn.sigmoid(h[..., o:o + 3 * B_HEADS]).reshape(B_, S, B_HEADS, 3)
    b_out = nsa(q, kvs[0], kvs[1], kvs[2], kvs[3], kvs[4], kvs[5], gates, rel_bias,
                pe_k, w1_k, w2_k, pe_v, w1_v, w2_v)
    return jnp.concatenate([a_out, b_out], axis=-1) @ w_out


def conv_module(x, w_in, b_in, dw_w, dw_b, ln_g, ln_b, w_out, b_out):
    C = w_out.shape[0]
    h = x @ w_in + b_in
    h = h[..., :C] * jax.nn.sigmoid(h[..., C:])
    h = lax.conv_general_dilated(h, dw_w[:, None, :], window_strides=(1,),
                                 padding=[(CONV_WIDTH - 1, 0)],
                                 dimension_numbers=('NWC', 'WIO', 'NWC'),
                                 feature_group_count=C) + dw_b
    h = jax.nn.silu(layer_norm(h, ln_g, ln_b))
    return h @ w_out + b_out


def swiglu(x, w_gate, w_up, w_down):
    return (jax.nn.silu(x @ w_gate) * (x @ w_up)) @ w_down


def setup_inputs(seed: int = 0) -> dict:
    key = jax.random.key(seed)
    keys = iter(jax.random.split(key, 40))

    def nrm(shape, scale):
        return jax.random.normal(next(keys), shape, jnp.float32) * scale

    D, A, dh = D_MODEL, A_WIDTH, B_HEAD_DIM
    return {
        "x": nrm((BATCH, SEQ, D), 1.0),
        "rel_bias": nrm((N_BUCKETS, B_HEADS), 0.5),
        "hyb_w_in": nrm((N_EVEN, D, PROJ_WIDTH), D ** -0.5),
        "hyb_w_out": nrm((N_EVEN, D, D), BETA * D ** -0.5),
        "gmlp_ln_g": 1.0 + nrm((N_EVEN, A), 0.01),
        "gmlp_ln_b": nrm((N_EVEN, A), 0.01),
        "gmlp_w_s": nrm((N_EVEN, A_GROUPS, CHUNK, CHUNK), CHUNK ** -0.5),
        "gmlp_b_s": 1.0 + nrm((N_EVEN, A_GROUPS, CHUNK), 0.1),
        "cmp_pe_k": nrm((N_EVEN, CMP_BLOCK, dh), 0.1),
        "cmp_w1_k": nrm((N_EVEN, CMP_BLOCK * dh, dh), (CMP_BLOCK * dh) ** -0.5),
        "cmp_w2_k": nrm((N_EVEN, dh, dh), dh ** -0.5),
        "cmp_pe_v": nrm((N_EVEN, CMP_BLOCK, dh), 0.1),
        "cmp_w1_v": nrm((N_EVEN, CMP_BLOCK * dh, dh), (CMP_BLOCK * dh) ** -0.5),
        "cmp_w2_v": nrm((N_EVEN, dh, dh), dh ** -0.5),
        "conv_w_in": nrm((N_ODD, D, 2 * D), D ** -0.5),
        "conv_b_in": nrm((N_ODD, 2 * D), 0.01),
        "conv_dw_w": nrm((N_ODD, CONV_WIDTH, D), CONV_WIDTH ** -0.5),
        "conv_dw_b": nrm((N_ODD, D), 0.01),
        "conv_ln_g": 1.0 + nrm((N_ODD, D), 0.01),
        "conv_ln_b": nrm((N_ODD, D), 0.01),
        "conv_w_out": nrm((N_ODD, D, D), BETA * D ** -0.5),
        "conv_b_out": nrm((N_ODD, D), 0.01),
        "ffn_w_gate": nrm((DEPTH, D, FFN_HIDDEN), D ** -0.5),
        "ffn_w_up": nrm((DEPTH, D, FFN_HIDDEN), D ** -0.5),
        "ffn_w_down": nrm((DEPTH, FFN_HIDDEN, D), BETA * FFN_HIDDEN ** -0.5),
        "norm_mix_g": 1.0 + nrm((DEPTH, D), 0.01),
        "norm_mix_b": nrm((DEPTH, D), 0.01),
        "norm_ffn_g": 1.0 + nrm((DEPTH, D), 0.01),
        "norm_ffn_b": nrm((DEPTH, D), 0.01),
    }


def reference(x, rel_bias, hyb_w_in, hyb_w_out, gmlp_ln_g, gmlp_ln_b, gmlp_w_s, gmlp_b_s,
              cmp_pe_k, cmp_w1_k, cmp_w2_k, cmp_pe_v, cmp_w1_v, cmp_w2_v,
              conv_w_in, conv_b_in, conv_dw_w, conv_dw_b, conv_ln_g, conv_ln_b,
              conv_w_out, conv_b_out, ffn_w_gate, ffn_w_up, ffn_w_down,
              norm_mix_g, norm_mix_b, norm_ffn_g, norm_ffn_b):
    for layer in range(DEPTH):
        i = layer // 2
        if layer % 2 == 0:
            y = hybrid_mixer(x, hyb_w_in[i], hyb_w_out[i], gmlp_ln_g[i], gmlp_ln_b[i],
                             gmlp_w_s[i], gmlp_b_s[i], rel_bias,
                             cmp_pe_k[i], cmp_w1_k[i], cmp_w2_k[i],
                             cmp_pe_v[i], cmp_w1_v[i], cmp_w2_v[i])
        else:
            y = conv_module(x, conv_w_in[i], conv_b_in[i], conv_dw_w[i], conv_dw_b[i],
                            conv_ln_g[i], conv_ln_b[i], conv_w_out[i], conv_b_out[i])
        x = layer_norm(ALPHA * x + y, norm_mix_g[layer], norm_mix_b[layer])
        f = swiglu(x, ffn_w_gate[layer], ffn_w_up[layer], ffn_w_down[layer])
        x = layer_norm(ALPHA * x + f, norm_ffn_g[layer], norm_ffn_b[layer])
    return x
```

```python
import functools
import math

import numpy as np
import jax
import jax.numpy as jnp
from jax import lax
from jax.experimental import pallas as pl
from jax.experimental.pallas import tpu as pltpu

F32 = jnp.float32
BF16 = jnp.bfloat16

D_MODEL = 1024
DEPTH = 2
ALPHA = (2 * DEPTH) ** 0.25
CHUNK = 128
A_WIDTH = D_MODEL // 2
A_GROUPS = 4
HEADS = 8
KV_HEADS = 2
GROUP = HEADS // KV_HEADS
HEAD_DIM = (D_MODEL // 2) // HEADS
CMP_STRIDE = 16
CMP_BLOCK = 32
SEL_BLOCK = 64
N_SELECT = 16
WINDOW = 512
Q_BLOCK = 128
N_BUCKETS = 32
MAX_DISTANCE = 1024
CONV_WIDTH = 31
NEG_INF = -1e30
FORCE = 1e9
LN_EPS = 1e-5

LANES = 128
VMEM_LIMIT = 56 * 1024 * 1024

ROWS = GROUP * Q_BLOCK
FAR_TILE = 512
NEAR_TILES = 8
NEAR = NEAR_TILES * Q_BLOCK
NEAR_BACK = NEAR - Q_BLOCK
WIN_KEYS = WINDOW + Q_BLOCK
DN_WIDTH = NEAR + NEAR_BACK
HALO = 32


def _dot(a, b):
    return jnp.dot(a, b, preferred_element_type=F32)


def _dot_nt(a, b):
    return lax.dot_general(a, b, (((1,), (1,)), ((), ())), preferred_element_type=F32)


def _layer_norm(z, g, b):
    mu = jnp.mean(z, axis=-1, keepdims=True)
    d = z - mu
    var = jnp.mean(d * d, axis=-1, keepdims=True)
    return d * lax.rsqrt(var + LN_EPS) * g + b


def _t5_bucket(dist):
    n = jnp.maximum(dist, 0)
    max_exact = N_BUCKETS // 2
    nf = jnp.maximum(n, 1).astype(jnp.float32)
    large = max_exact + (jnp.log(nf / max_exact) / math.log(MAX_DISTANCE / max_exact)
                         * (N_BUCKETS - max_exact)).astype(jnp.int32)
    large = jnp.minimum(large, N_BUCKETS - 1)
    return jnp.where(n < max_exact, n, large)


def _bucket_thresholds(seq):
    b = _t5_bucket(jnp.arange(seq, dtype=jnp.int32))
    j = jnp.arange(N_BUCKETS, dtype=jnp.int32)
    return jnp.sum((b[None, :] < j[:, None]).astype(jnp.int32), axis=1).astype(jnp.int32)


def _tables_kernel(thr_ref, rb_ref, fc_ref, dn_ref, *, nc):
    h = pl.program_id(0)
    base = h * N_BUCKETS
    far = rb_ref[base + N_BUCKETS - 1]

    def bias_of(dist):
        val = jnp.full(dist.shape, far, F32)
        for j in range(N_BUCKETS - 1, 0, -1):
            val = jnp.where(dist < thr_ref[j], rb_ref[base + j - 1], val)
        return val

    ql = lax.broadcasted_iota(jnp.int32, (Q_BLOCK, LANES), 0)
    ln = lax.broadcasted_iota(jnp.int32, (Q_BLOCK, LANES), 1)
    for c in range(2 * nc // LANES):
        dist = ql - CMP_STRIDE * (ln + (c * LANES - (nc - 8))) - (CMP_BLOCK - 1)
        fc_ref[0, :, c * LANES:(c + 1) * LANES] = jnp.where(dist >= 0, bias_of(dist), NEG_INF)
    for c in range(DN_WIDTH // LANES):
        dist = ql + (NEAR_BACK - c * LANES) - ln
        dn_ref[0, :, c * LANES:(c + 1) * LANES] = jnp.where(dist >= 0, bias_of(dist) - far, NEG_INF)


def _bias_tables(thr, rb_flat, nc):
    return pl.pallas_call(
        functools.partial(_tables_kernel, nc=nc),
        out_shape=(jax.ShapeDtypeStruct((HEADS, Q_BLOCK, 2 * nc), F32),
                   jax.ShapeDtypeStruct((HEADS, Q_BLOCK, DN_WIDTH), F32)),
        grid=(HEADS,),
        in_specs=[pl.BlockSpec(memory_space=pltpu.SMEM), pl.BlockSpec(memory_space=pltpu.SMEM)],
        out_specs=(pl.BlockSpec((1, Q_BLOCK, 2 * nc), lambda h: (h, 0, 0)),
                   pl.BlockSpec((1, Q_BLOCK, DN_WIDTH), lambda h: (h, 0, 0))),
        name="bias_tables",
    )(thr, rb_flat)


def _proj_kernel(x_ref, wuv_ref, wqkv_ref, lng_ref, lnb_ref, ws_ref, bs_ref,
                 a_ref, q_ref, cmp_ref, kaug_ref, vslc_ref, win_ref, gate_ref, *, tm, seq):
    xb = x_ref[...].astype(BF16)
    uv = jax.nn.gelu(_dot(xb, wuv_ref[...]))
    u = uv[:, :A_WIDTH]
    v = _layer_norm(uv[:, A_WIDTH:], lng_ref[...], lnb_ref[...]).astype(BF16)
    row = lax.broadcasted_iota(jnp.int32, (CHUNK, CHUNK), 0)
    col = lax.broadcasted_iota(jnp.int32, (CHUNK, CHUNK), 1)
    gd = A_WIDTH // A_GROUPS
    for g in range(A_GROUPS):
        w = jnp.where(col <= row, ws_ref[g], 0.0).astype(BF16)
        for c in range(tm // CHUNK):
            rs = slice(c * CHUNK, (c + 1) * CHUNK)
            cs = slice(g * gd, (g + 1) * gd)
            s = _dot(w, v[rs, cs]) + bs_ref[g]
            a_ref[rs, cs] = (u[rs, cs] * s).astype(BF16)

    h = _dot(xb, wqkv_ref[...])
    q_ref[...] = (h[:, 0:512] * (HEAD_DIM ** -0.5)).astype(BF16)
    cmp_ref[...] = h[:, 512:768].astype(BF16)
    kaug_ref[:, 0:LANES] = h[:, 768:896].astype(BF16)
    pos = (pl.program_id(0) * tm) % seq + lax.broadcasted_iota(jnp.int32, (tm, LANES), 0)
    blk = lax.broadcasted_iota(jnp.int32, (tm, LANES), 1)
    kaug_ref[:, LANES:2 * LANES] = jnp.where(pos // SEL_BLOCK == blk, 1.0, 0.0).astype(BF16)
    vslc_ref[...] = h[:, 896:1024].astype(BF16)
    win_ref[...] = h[:, 1024:1280].astype(BF16)
    gate_ref[...] = jax.nn.sigmoid(h[:, 1280:1408])


def _proj(x2d, wuv, wqkv, lng, lnb, ws, bs, *, seq, tm=512):
    t = x2d.shape[0]
    row = lambda i: (i, 0)
    const2 = lambda i: (0, 0)
    const3 = lambda i: (0, 0, 0)
    outs = [(A_WIDTH, BF16), (512, BF16), (256, BF16), (256, BF16), (128, BF16), (256, BF16), (128, F32)]
    return pl.pallas_call(
        functools.partial(_proj_kernel, tm=tm, seq=seq),
        out_shape=tuple(jax.ShapeDtypeStruct((t, w), dt) for w, dt in outs),
        grid=(t // tm,),
        in_specs=[pl.BlockSpec((tm, D_MODEL), row),
                  pl.BlockSpec(wuv.shape, const2),
                  pl.BlockSpec(wqkv.shape, const2),
                  pl.BlockSpec(lng.shape, const2),
                  pl.BlockSpec(lnb.shape, const2),
                  pl.BlockSpec(ws.shape, const3),
                  pl.BlockSpec(bs.shape, const3)],
        out_specs=tuple(pl.BlockSpec((tm, w), row) for w, _ in outs),
        compiler_params=pltpu.CompilerParams(dimension_semantics=("parallel",), vmem_limit_bytes=VMEM_LIMIT),
        name="in_proj_gmlp",
    )(x2d, wuv, wqkv, lng, lnb, ws, bs)


def _compress_kernel(r_ref, wa_ref, wb_ref, pea_ref, peb_ref, w2_ref, o_ref):
    r = r_ref[0]
    top = _dot(r, wa_ref[...])
    bot = _dot(r, wb_ref[...])
    pe = _dot(pea_ref[...], wa_ref[...]) + _dot(peb_ref[...], wb_ref[...])
    hid = top + pltpu.roll(bot, bot.shape[0] - 1, 0) + pe[0:1]
    o_ref[0] = _dot(jax.nn.gelu(hid).astype(BF16), w2_ref[...]).astype(BF16)


def _compress(cmp3, wa, wb, pea, peb, w2):
    b, nc, width = cmp3.shape
    const2 = lambda i: (0, 0)
    return pl.pallas_call(
        _compress_kernel,
        out_shape=jax.ShapeDtypeStruct((b, nc, 256), BF16),
        grid=(b,),
        in_specs=[pl.BlockSpec((1, nc, width), lambda i: (i, 0, 0)),
                  pl.BlockSpec(wa.shape, const2), pl.BlockSpec(wb.shape, const2),
                  pl.BlockSpec(pea.shape, const2), pl.BlockSpec(peb.shape, const2),
                  pl.BlockSpec(w2.shape, const2)],
        out_specs=pl.BlockSpec((1, nc, 256), lambda i: (i, 0, 0)),
        compiler_params=pltpu.CompilerParams(dimension_semantics=("parallel",), vmem_limit_bytes=VMEM_LIMIT),
        name="kv_compress",
    )(cmp3, wa, wb, pea, peb, w2)


def _nsa_kernel(q_ref, gate_ref, kvc_ref, kaug_ref, vslc_ref, win_ref, fc_ref, dn_ref, ov_ref,
                o_ref, m_sc, l_sc, acc_sc, *, nc):
    ib = pl.program_id(1)
    t0 = ib * Q_BLOCK
    lane = lax.broadcasted_iota(jnp.int32, (Q_BLOCK, LANES), 1)
    qrow = lax.broadcasted_iota(jnp.int32, (Q_BLOCK, LANES), 0)
    q = q_ref[...]
    gates = gate_ref[...]
    kc = kvc_ref[0, :, 0:LANES]
    vc = kvc_ref[0, :, LANES:2 * LANES]
    ov = ov_ref[...]

    def stack_heads(fn):
        return jnp.concatenate([fn(g) for g in range(GROUP)], axis=0)

    jq = 2 * ib + jnp.where(qrow >= SEL_BLOCK, 1, 0)
    near0 = 2 * ib - NEAR_BACK // SEL_BLOCK
    n_far = jnp.maximum(ib - 4, 0) // 4
    kstart = jnp.maximum(t0 - NEAR_BACK, 0)
    j0 = pl.multiple_of(kstart - (t0 - NEAR_BACK), LANES)
    kstart = pl.multiple_of(kstart, LANES)
    wstart = jnp.maximum(t0 - WINDOW, 0)
    jw = pl.multiple_of(wstart - (t0 - NEAR_BACK), LANES)
    wstart = pl.multiple_of(wstart, LANES)
    fc_shift = nc + 8 + 8 * ib
    has_cmp = (t0 + lax.broadcasted_iota(jnp.int32, (ROWS, 1), 0) % Q_BLOCK) >= CMP_BLOCK - 1
    win_edge = jnp.where((lane > qrow) | (t0 < WINDOW), 0.0, NEG_INF)

    outs = []
    for hkv in range(KV_HEADS):
        mine = (lane >= HEAD_DIM) if hkv else (lane < HEAD_DIM)
        qh = stack_heads(lambda g: jnp.where(mine, q[:, g * LANES:(g + 1) * LANES], 0.0).astype(BF16))

        lc = _dot_nt(qh, kc)
        lc = lc + stack_heads(lambda g: pltpu.roll(fc_ref[hkv * GROUP + g], fc_shift, 1)[:, :nc])
        e = jnp.exp(lc - jnp.max(lc, axis=-1, keepdims=True))
        pc = e / jnp.sum(e, axis=-1, keepdims=True) * jnp.where(has_cmp, 1.0, 0.0)
        o_c = _dot(pc.astype(BF16), vc)

        psum = pc[0:Q_BLOCK] + pc[Q_BLOCK:2 * Q_BLOCK] + pc[2 * Q_BLOCK:3 * Q_BLOCK] + pc[3 * Q_BLOCK:]
        p_hi = psum.astype(BF16)
        p_lo = (psum - p_hi.astype(F32)).astype(BF16)
        imp = _dot(p_hi, ov) + _dot(p_lo, ov)
        imp = jnp.where(lane > jq, NEG_INF, imp)
        imp = jnp.where((lane == 0) | (lane == jq) | (lane == jq - 1), FORCE, imp)
        lane_f = lane.astype(F32)

        def pick_one(_, carry):
            cur, sel = carry
            mx = jnp.max(cur, axis=-1, keepdims=True)
            first = jnp.min(jnp.where(cur == mx, lane_f, float(LANES)), axis=-1, keepdims=True)
            pick = lane_f == first
            return jnp.where(pick, -3e38, cur), jnp.where(pick, 1.0, sel)

        _, sel = lax.fori_loop(0, N_SELECT, pick_one, (imp, jnp.zeros_like(imp)))

        m_near = jnp.where(sel > 0, 0.0, NEG_INF).astype(BF16)
        m_far = jnp.where((sel > 0) & (lane < near0), 0.0, NEG_INF).astype(BF16)
        qa_far = jnp.concatenate([qh, jnp.concatenate([m_far] * GROUP, axis=0)], axis=1)
        qa_near = jnp.concatenate([qh, jnp.concatenate([m_near] * GROUP, axis=0)], axis=1)

        m_sc[...] = jnp.full(m_sc.shape, -jnp.inf, F32)
        l_sc[...] = jnp.zeros(l_sc.shape, F32)
        acc_sc[...] = jnp.zeros(acc_sc.shape, F32)

        def far_step(kt, carry):
            ks = pl.multiple_of(kt * FAR_TILE, FAR_TILE)
            s = _dot_nt(qa_far, kaug_ref[pl.ds(ks, FAR_TILE), :])
            m_old = m_sc[...]
            m_new = jnp.maximum(m_old, jnp.max(s, axis=-1, keepdims=True))
            p = jnp.exp(s - m_new)
            alpha = jnp.exp(m_old - m_new)
            l_sc[...] = alpha * l_sc[...] + jnp.sum(p, axis=-1, keepdims=True)
            acc_sc[...] = alpha * acc_sc[...] + _dot(p.astype(BF16), vslc_ref[pl.ds(ks, FAR_TILE), :])
            m_sc[...] = m_new
            return carry

        lax.fori_loop(0, n_far, far_step, 0)

        s = _dot_nt(qa_near, kaug_ref[pl.ds(kstart, NEAR), :])
        s = s + stack_heads(lambda g: dn_ref[hkv * GROUP + g, :, pl.ds(j0, NEAR)])
        m_old = m_sc[...]
        m_new = jnp.maximum(m_old, jnp.max(s, axis=-1, keepdims=True))
        p = jnp.exp(s - m_new)
        alpha = jnp.exp(m_old - m_new)
        l_fin = alpha * l_sc[...] + jnp.sum(p, axis=-1, keepdims=True)
        o_s = (alpha * acc_sc[...] + _dot(p.astype(BF16), vslc_ref[pl.ds(kstart, NEAR), :])) / l_fin

        sw = _dot_nt(qh, win_ref[pl.ds(wstart, WIN_KEYS), 0:LANES])
        sw = sw + stack_heads(lambda g: dn_ref[hkv * GROUP + g, :, pl.ds(jw, WIN_KEYS)])
        sw = jnp.concatenate([sw[:, :LANES] + jnp.concatenate([win_edge] * GROUP, axis=0), sw[:, LANES:]], axis=1)
        pw = jnp.exp(sw - jnp.max(sw, axis=-1, keepdims=True))
        o_w = _dot(pw.astype(BF16), win_ref[pl.ds(wstart, WIN_KEYS), LANES:2 * LANES]) \
            / jnp.sum(pw, axis=-1, keepdims=True)

        per_head = []
        for g in range(GROUP):
            c = 3 * (hkv * GROUP + g)
            rs = slice(g * Q_BLOCK, (g + 1) * Q_BLOCK)
            per_head.append(gates[:, c:c + 1] * o_c[rs] + gates[:, c + 1:c + 2] * o_s[rs]
                            + gates[:, c + 2:c + 3] * o_w[rs])
        outs.append(per_head)

    for g in range(GROUP):
        o_ref[:, g * LANES:(g + 1) * LANES] = jnp.where(lane < HEAD_DIM, outs[0][g], outs[1][g]).astype(BF16)


def _nsa(qs, gates, kvc, kaug, vslc, win, fc, dn, ov, *, batch, seq):
    nc = seq // CMP_STRIDE
    nq = seq // Q_BLOCK
    qrow = lambda b, i: (b * nq + i, 0)
    per_batch = lambda b, i: (b, 0)
    const2 = lambda b, i: (0, 0)
    const3 = lambda b, i: (0, 0, 0)
    return pl.pallas_call(
        functools.partial(_nsa_kernel, nc=nc),
        out_shape=jax.ShapeDtypeStruct((batch * seq, 4 * LANES), BF16),
        grid=(batch, nq),
        in_specs=[pl.BlockSpec((Q_BLOCK, 4 * LANES), qrow),
                  pl.BlockSpec((Q_BLOCK, LANES), qrow),
                  pl.BlockSpec((1, nc, 256), lambda b, i: (b, 0, 0)),
                  pl.BlockSpec((seq, 256), per_batch),
                  pl.BlockSpec((seq, LANES), per_batch),
                  pl.BlockSpec((seq, 256), per_batch),
                  pl.BlockSpec(fc.shape, const3, pipeline_mode=pl.Buffered(1)),
                  pl.BlockSpec(dn.shape, const3, pipeline_mode=pl.Buffered(1)),
                  pl.BlockSpec(ov.shape, const2)],
        out_specs=pl.BlockSpec((Q_BLOCK, 4 * LANES), qrow),
        scratch_shapes=[pltpu.VMEM((ROWS, 1), F32), pltpu.VMEM((ROWS, 1), F32), pltpu.VMEM((ROWS, LANES), F32)],
        compiler_params=pltpu.CompilerParams(dimension_semantics=("parallel", "arbitrary"),
                                             vmem_limit_bytes=VMEM_LIMIT),
        name="sparse_attention",
    )(qs, gates, kvc, kaug, vslc, win, fc, dn, ov)


def _outproj_kernel(x_ref, a_ref, b_ref, wa_ref, wb_ref, g_ref, beta_ref, o_ref):
    y = _dot(a_ref[...], wa_ref[...]) + _dot(b_ref[...], wb_ref[...])
    o_ref[...] = _layer_norm(ALPHA * x_ref[...] + y, g_ref[...], beta_ref[...])


def _outproj(x2d, a, b, wa, wb, g, beta, *, tm=512):
    t = x2d.shape[0]
    row = lambda i: (i, 0)
    const2 = lambda i: (0, 0)
    return pl.pallas_call(
        _outproj_kernel,
        out_shape=jax.ShapeDtypeStruct((t, D_MODEL), F32),
        grid=(t // tm,),
        in_specs=[pl.BlockSpec((tm, D_MODEL), row), pl.BlockSpec((tm, A_WIDTH), row), pl.BlockSpec((tm, 512), row),
                  pl.BlockSpec(wa.shape, const2), pl.BlockSpec(wb.shape, const2),
                  pl.BlockSpec(g.shape, const2), pl.BlockSpec(beta.shape, const2)],
        out_specs=pl.BlockSpec((tm, D_MODEL), row),
        compiler_params=pltpu.CompilerParams(dimension_semantics=("parallel",), vmem_limit_bytes=VMEM_LIMIT),
        name="out_proj_norm",
    )(x2d, a, b, wa, wb, g, beta)


def _ffn_kernel(x_ref, wg_ref, wu_ref, wd_ref, g_ref, beta_ref, o_ref, acc_ref, *, hc):
    x = x_ref[...]
    xb = x.astype(BF16)
    hidden = wg_ref.shape[1]
    for c in range(hidden // hc):
        cs = slice(c * hc, (c + 1) * hc)
        gate = _dot(xb, wg_ref[:, cs])
        up = _dot(xb, wu_ref[:, cs])
        part = _dot((jax.nn.silu(gate) * up).astype(BF16), wd_ref[cs, :])
        if c == 0:
            acc_ref[...] = part
        else:
            acc_ref[...] += part
    o_ref[...] = _layer_norm(ALPHA * x + acc_ref[...], g_ref[...], beta_ref[...])


def _ffn(x2d, wg, wu, wd, g, beta, *, tm=512, hc=256):
    t = x2d.shape[0]
    row = lambda i: (i, 0)
    const2 = lambda i: (0, 0)
    once = pl.Buffered(1)
    return pl.pallas_call(
        functools.partial(_ffn_kernel, hc=hc),
        out_shape=jax.ShapeDtypeStruct((t, D_MODEL), F32),
        grid=(t // tm,),
        in_specs=[pl.BlockSpec((tm, D_MODEL), row),
                  pl.BlockSpec(wg.shape, const2, pipeline_mode=once),
                  pl.BlockSpec(wu.shape, const2, pipeline_mode=once),
                  pl.BlockSpec(wd.shape, const2, pipeline_mode=once),
                  pl.BlockSpec(g.shape, const2), pl.BlockSpec(beta.shape, const2)],
        out_specs=pl.BlockSpec((tm, D_MODEL), row),
        scratch_shapes=[pltpu.VMEM((tm, D_MODEL), F32)],
        compiler_params=pltpu.CompilerParams(dimension_semantics=("parallel",), vmem_limit_bytes=VMEM_LIMIT),
        name="swiglu_ffn_norm",
    )(x2d, wg, wu, wd, g, beta)


def _conv_kernel(x_ref, win_ref, bin_ref, dww_ref, dwb_ref, lng_ref, lnb_ref, wout_ref, bout_ref,
                 g_ref, beta_ref, o_ref, buf_ref, cv_ref, *, tm, rc):
    @pl.when(pl.program_id(1) == 0)
    def _():
        buf_ref[0:HALO, :] = jnp.zeros((HALO, D_MODEL), F32)

    x = x_ref[...]
    h = _dot(x.astype(BF16), win_ref[...]) + bin_ref[...]
    buf_ref[HALO:HALO + tm, :] = h[:, :D_MODEL] * jax.nn.sigmoid(h[:, D_MODEL:])

    lead = HALO - (CONV_WIDTH - 1)

    def conv_rows(r, carry):
        r0 = pl.multiple_of(r * rc, rc)
        for lc in range(D_MODEL // LANES):
            ls = slice(lc * LANES, (lc + 1) * LANES)
            acc = jnp.broadcast_to(dwb_ref[:, ls], (rc, LANES))
            za = buf_ref[pl.ds(r0, rc + HALO), ls]
            for b in range(8):
                taps = range(b, CONV_WIDTH, 8)
                zb = za[lead + b:lead + b + rc + 8 * (len(taps) - 1)]
                for a, j in enumerate(taps):
                    acc = acc + zb[8 * a:8 * a + rc] * dww_ref[j:j + 1, ls]
            cv_ref[pl.ds(r0, rc), ls] = acc
        return carry

    lax.fori_loop(0, tm // rc, conv_rows, 0)
    buf_ref[0:HALO, :] = buf_ref[tm:tm + HALO, :]

    y = jax.nn.silu(_layer_norm(cv_ref[...], lng_ref[...], lnb_ref[...]))
    y = _dot(y.astype(BF16), wout_ref[...]) + bout_ref[...]
    o_ref[...] = _layer_norm(ALPHA * x + y, g_ref[...], beta_ref[...])


def _conv(x2d, w_in, b_in, dw_w, dw_b, ln_g, ln_b, w_out, b_out, g, beta, *, batch, seq, tm=512, rc=64):
    nt = seq // tm
    row = lambda b, i: (b * nt + i, 0)
    const2 = lambda b, i: (0, 0)
    consts = (w_in, b_in, dw_w, dw_b, ln_g, ln_b, w_out, b_out, g, beta)
    return pl.pallas_call(
        functools.partial(_conv_kernel, tm=tm, rc=rc),
        out_shape=jax.ShapeDtypeStruct((batch * seq, D_MODEL), F32),
        grid=(batch, nt),
        in_specs=[pl.BlockSpec((tm, D_MODEL), row)] + [pl.BlockSpec(c.shape, const2) for c in consts],
        out_specs=pl.BlockSpec((tm, D_MODEL), row),
        scratch_shapes=[pltpu.VMEM((HALO + tm, D_MODEL), F32), pltpu.VMEM((tm, D_MODEL), F32)],
        compiler_params=pltpu.CompilerParams(dimension_semantics=("parallel", "arbitrary"),
                                             vmem_limit_bytes=VMEM_LIMIT),
        name="conv_module_norm",
    )(x2d, *consts)


def _head_perm():
    p = np.arange(HEADS * HEAD_DIM)
    g, half, d = p // LANES, (p % LANES) // HEAD_DIM, p % HEAD_DIM
    return (g + GROUP * half) * HEAD_DIM + d


def _compress_weights(pe_k, w1_k, w2_k, pe_v, w1_v, w2_v):
    eye = jnp.eye(KV_HEADS, dtype=F32)
    w1 = jnp.stack([w1_k, w1_v]).reshape(2, CMP_BLOCK, HEAD_DIM, HEAD_DIM)
    w1x = jnp.einsum('klde,km,hn->lkhdmne', w1, eye, eye).reshape(CMP_BLOCK * 256, 256)
    half = CMP_STRIDE * 256
    pe = jnp.stack([pe_k, pe_v])
    pex = jnp.broadcast_to(pe.transpose(1, 0, 2)[:, :, None, :], (CMP_BLOCK, 2, KV_HEADS, HEAD_DIM))
    pex = pex.reshape(2, half)
    w2 = jnp.stack([w2_k, w2_v])
    w2x = jnp.einsum('kde,km,hn->khdmne', w2, eye, eye).reshape(256, 256)
    pea = jnp.broadcast_to(pex[0:1], (8, half)).astype(BF16)
    peb = jnp.broadcast_to(pex[1:2], (8, half)).astype(BF16)
    return w1x[:half].astype(BF16), w1x[half:].astype(BF16), pea, peb, w2x.astype(BF16)


def _overlap_matrix(nc):
    c0 = np.arange(nc)[:, None] * CMP_STRIDE
    s0 = np.arange(LANES)[None, :] * SEL_BLOCK
    ov = (c0 < s0 + SEL_BLOCK) & (c0 + CMP_BLOCK > s0) & (np.arange(nc)[:, None] < nc - 1)
    return jnp.asarray(ov, dtype=BF16)


def _even_layer(x2d, rel_bias, w_in, w_out, ln_g, ln_b, w_s, b_s, pe_k, w1_k, w2_k, pe_v, w1_v, w2_v,
                norm_g, norm_b, *, batch, seq):
    nc = seq // CMP_STRIDE
    assert seq % FAR_TILE == 0 and seq >= NEAR and seq // SEL_BLOCK <= LANES
    perm = _head_perm()
    qw = HEADS * HEAD_DIM
    o = 2 * A_WIDTH
    wuv = w_in[:, :o].astype(BF16)
    wq = w_in[:, o:o + qw][:, perm]
    wkv = w_in[:, o + qw:o + qw + 768]
    wgt = jnp.pad(w_in[:, o + qw + 768:], ((0, 0), (0, LANES - 3 * HEADS)))
    wqkv = jnp.concatenate([wq, wkv, wgt], axis=1).astype(BF16)
    bs = jnp.broadcast_to(b_s[:, :, None], (A_GROUPS, CHUNK, A_WIDTH // A_GROUPS))
    a_out, qs, cmp2, kaug, vslc, win, gates = _proj(
        x2d, wuv, wqkv, ln_g[None, :], ln_b[None, :], w_s, bs, seq=seq)

    kvc = _compress(cmp2.reshape(batch, nc, CMP_STRIDE * 256),
                    *_compress_weights(pe_k, w1_k, w2_k, pe_v, w1_v, w2_v))

    fc, dn = _bias_tables(_bucket_thresholds(seq), rel_bias.T.reshape(-1), nc)
    b_out = _nsa(qs, gates, kvc, kaug, vslc, win, fc, dn, _overlap_matrix(nc), batch=batch, seq=seq)

    wo_a = w_out[:A_WIDTH].astype(BF16)
    wo_b = w_out[A_WIDTH:][perm].astype(BF16)
    return _outproj(x2d, a_out, b_out, wo_a, wo_b, norm_g[None, :], norm_b[None, :])


def kernel(x, rel_bias, hyb_w_in, hyb_w_out, gmlp_ln_g, gmlp_ln_b, gmlp_w_s, gmlp_b_s, cmp_pe_k, cmp_w1_k, cmp_w2_k, cmp_pe_v, cmp_w1_v, cmp_w2_v, conv_w_in, conv_b_in, conv_dw_w, conv_dw_b, conv_ln_g, conv_ln_b, conv_w_out, conv_b_out, ffn_w_gate, ffn_w_up, ffn_w_down, norm_mix_g, norm_mix_b, norm_ffn_g, norm_ffn_b):
    batch, seq, d = x.shape
    h = x.reshape(batch * seq, d)
    for layer in range(DEPTH):
        i = layer // 2
        if layer % 2 == 0:
            h = _even_layer(h, rel_bias, hyb_w_in[i], hyb_w_out[i], gmlp_ln_g[i], gmlp_ln_b[i],
                            gmlp_w_s[i], gmlp_b_s[i], cmp_pe_k[i], cmp_w1_k[i], cmp_w2_k[i],
                            cmp_pe_v[i], cmp_w1_v[i], cmp_w2_v[i],
                            norm_mix_g[layer], norm_mix_b[layer], batch=batch, seq=seq)
        else:
            h = _conv(h, conv_w_in[i].astype(BF16), conv_b_in[i][None, :], conv_dw_w[i], conv_dw_b[i][None, :],
                      conv_ln_g[i][None, :], conv_ln_b[i][None, :], conv_w_out[i].astype(BF16),
                      conv_b_out[i][None, :], norm_mix_g[layer][None, :], norm_mix_b[layer][None, :],
                      batch=batch, seq=seq)
        h = _ffn(h, ffn_w_gate[layer].astype(BF16), ffn_w_up[layer].astype(BF16),
                 ffn_w_down[layer].astype(BF16), norm_ffn_g[layer][None, :], norm_ffn_b[layer][None, :])
    return h.reshape(batch, seq, d)
```

```python
import functools
import math

import numpy as np
import jax
import jax.numpy as jnp
from jax import lax
from jax.experimental import pallas as pl
from jax.experimental.pallas import tpu as pltpu

F32 = jnp.float32
BF16 = jnp.bfloat16

D_MODEL = 1024
DEPTH = 2
ALPHA = (2 * DEPTH) ** 0.25
CHUNK = 128
A_WIDTH = D_MODEL // 2
A_GROUPS = 4
HEADS = 8
KV_HEADS = 2
GROUP = HEADS // KV_HEADS
HEAD_DIM = (D_MODEL // 2) // HEADS
CMP_STRIDE = 16
CMP_BLOCK = 32
SEL_BLOCK = 64
N_SELECT = 16
WINDOW = 512
Q_BLOCK = 128
N_BUCKETS = 32
MAX_DISTANCE = 1024
CONV_WIDTH = 31
NEG_INF = -1e30
FORCE = 1e9
LN_EPS = 1e-5

LANES = 128
VMEM_LIMIT = 56 * 1024 * 1024

ROWS = GROUP * Q_BLOCK
FAR_TILE = 512
NEAR_TILES = 8
NEAR = NEAR_TILES * Q_BLOCK
NEAR_BACK = NEAR - Q_BLOCK
WIN_KEYS = WINDOW + Q_BLOCK
DN_WIDTH = NEAR + NEAR_BACK
HALO = 32


def _dot(a, b):
    return jnp.dot(a, b, preferred_element_type=F32)


def _dot_nt(a, b):
    return lax.dot_general(a, b, (((1,), (1,)), ((), ())), preferred_element_type=F32)


def _layer_norm(z, g, b):
    mu = jnp.mean(z, axis=-1, keepdims=True)
    d = z - mu
    var = jnp.mean(d * d, axis=-1, keepdims=True)
    return d * lax.rsqrt(var + LN_EPS) * g + b


def _t5_bucket(dist):
    n = jnp.maximum(dist, 0)
    max_exact = N_BUCKETS // 2
    nf = jnp.maximum(n, 1).astype(jnp.float32)
    large = max_exact + (jnp.log(nf / max_exact) / math.log(MAX_DISTANCE / max_exact)
                         * (N_BUCKETS - max_exact)).astype(jnp.int32)
    large = jnp.minimum(large, N_BUCKETS - 1)
    return jnp.where(n < max_exact, n, large)


def _bucket_thresholds(seq):
    b = _t5_bucket(jnp.arange(seq, dtype=jnp.int32))
    j = jnp.arange(N_BUCKETS, dtype=jnp.int32)
    return jnp.sum((b[None, :] < j[:, None]).astype(jnp.int32), axis=1).astype(jnp.int32)


def _tables_kernel(thr_ref, rb_ref, fc_ref, dn_ref, *, nc):
    h = pl.program_id(0)
    base = h * N_BUCKETS
    far = rb_ref[base + N_BUCKETS - 1]

    def bias_of(dist):
        val = jnp.full(dist.shape, far, F32)
        for j in range(N_BUCKETS - 1, 0, -1):
            val = jnp.where(dist < thr_ref[j], rb_ref[base + j - 1], val)
        return val

    ql = lax.broadcasted_iota(jnp.int32, (Q_BLOCK, LANES), 0)
    ln = lax.broadcasted_iota(jnp.int32, (Q_BLOCK, LANES), 1)
    for c in range(2 * nc // LANES):
        dist = ql - CMP_STRIDE * (ln + (c * LANES - (nc - 8))) - (CMP_BLOCK - 1)
        fc_ref[0, :, c * LANES:(c + 1) * LANES] = jnp.where(dist >= 0, bias_of(dist), NEG_INF)
    for c in range(DN_WIDTH // LANES):
        dist = ql + (NEAR_BACK - c * LANES) - ln
        dn_ref[0, :, c * LANES:(c + 1) * LANES] = jnp.where(dist >= 0, bias_of(dist) - far, NEG_INF)


def _bias_tables(thr, rb_flat, nc):
    return pl.pallas_call(
        functools.partial(_tables_kernel, nc=nc),
        out_shape=(jax.ShapeDtypeStruct((HEADS, Q_BLOCK, 2 * nc), F32),
                   jax.ShapeDtypeStruct((HEADS, Q_BLOCK, DN_WIDTH), F32)),
        grid=(HEADS,),
        in_specs=[pl.BlockSpec(memory_space=pltpu.SMEM), pl.BlockSpec(memory_space=pltpu.SMEM)],
        out_specs=(pl.BlockSpec((1, Q_BLOCK, 2 * nc), lambda h: (h, 0, 0)),
                   pl.BlockSpec((1, Q_BLOCK, DN_WIDTH), lambda h: (h, 0, 0))),
        name="bias_tables",
    )(thr, rb_flat)


def _proj_kernel(x_ref, wuv_ref, wqkv_ref, lng_ref, lnb_ref, ws_ref, bs_ref,
                 a_ref, q_ref, cmp_ref, kaug_ref, vaug_ref, wink_ref, winv_ref, gate_ref, *, tm, seq):
    xb = x_ref[...].astype(BF16)
    uv = jax.nn.gelu(_dot(xb, wuv_ref[...]))
    u = uv[:, :A_WIDTH]
    v = _layer_norm(uv[:, A_WIDTH:], lng_ref[...], lnb_ref[...]).astype(BF16)
    row = lax.broadcasted_iota(jnp.int32, (CHUNK, CHUNK), 0)
    col = lax.broadcasted_iota(jnp.int32, (CHUNK, CHUNK), 1)
    gd = A_WIDTH // A_GROUPS
    for g in range(A_GROUPS):
        w = jnp.where(col <= row, ws_ref[g], 0.0).astype(BF16)
        for c in range(tm // CHUNK):
            rs = slice(c * CHUNK, (c + 1) * CHUNK)
            cs = slice(g * gd, (g + 1) * gd)
            s = _dot(w, v[rs, cs]) + bs_ref[g]
            a_ref[rs, cs] = (u[rs, cs] * s).astype(BF16)

    h = _dot(xb, wqkv_ref[...])
    q_ref[...] = (h[:, 0:512] * (HEAD_DIM ** -0.5)).astype(BF16)
    cmp_ref[...] = h[:, 512:768].astype(BF16)
    kaug_ref[:, 0:LANES] = h[:, 768:896].astype(BF16)
    pos = (pl.program_id(0) * tm) % seq + lax.broadcasted_iota(jnp.int32, (tm, LANES), 0)
    blk = lax.broadcasted_iota(jnp.int32, (tm, LANES), 1)
    kaug_ref[:, LANES:2 * LANES] = jnp.where(pos // SEL_BLOCK == blk, 1.0, 0.0).astype(BF16)
    ones = jnp.ones((tm, LANES), BF16)
    vaug_ref[:, 0:LANES] = h[:, 896:1024].astype(BF16)
    vaug_ref[:, LANES:2 * LANES] = ones
    wink_ref[...] = h[:, 1024:1152].astype(BF16)
    winv_ref[:, 0:LANES] = h[:, 1152:1280].astype(BF16)
    winv_ref[:, LANES:2 * LANES] = ones
    gate_ref[...] = jax.nn.sigmoid(h[:, 1280:1408])


def _proj(x2d, wuv, wqkv, lng, lnb, ws, bs, *, seq, tm=512):
    t = x2d.shape[0]
    row = lambda i: (i, 0)
    const2 = lambda i: (0, 0)
    const3 = lambda i: (0, 0, 0)
    outs = [(A_WIDTH, BF16), (512, BF16), (256, BF16), (256, BF16), (256, BF16), (128, BF16), (256, BF16),
            (128, F32)]
    return pl.pallas_call(
        functools.partial(_proj_kernel, tm=tm, seq=seq),
        out_shape=tuple(jax.ShapeDtypeStruct((t, w), dt) for w, dt in outs),
        grid=(t // tm,),
        in_specs=[pl.BlockSpec((tm, D_MODEL), row),
                  pl.BlockSpec(wuv.shape, const2),
                  pl.BlockSpec(wqkv.shape, const2),
                  pl.BlockSpec(lng.shape, const2),
                  pl.BlockSpec(lnb.shape, const2),
                  pl.BlockSpec(ws.shape, const3),
                  pl.BlockSpec(bs.shape, const3)],
        out_specs=tuple(pl.BlockSpec((tm, w), row) for w, _ in outs),
        compiler_params=pltpu.CompilerParams(dimension_semantics=("parallel",), vmem_limit_bytes=VMEM_LIMIT),
        name="in_proj_gmlp",
    )(x2d, wuv, wqkv, lng, lnb, ws, bs)


def _compress_kernel(r_ref, wa_ref, wb_ref, pea_ref, peb_ref, w2_ref, o_ref):
    r = r_ref[0]
    top = _dot(r, wa_ref[...])
    bot = _dot(r, wb_ref[...])
    pe = _dot(pea_ref[...], wa_ref[...]) + _dot(peb_ref[...], wb_ref[...])
    hid = top + pltpu.roll(bot, bot.shape[0] - 1, 0) + pe[0:1]
    o_ref[0] = _dot(jax.nn.gelu(hid).astype(BF16), w2_ref[...]).astype(BF16)


def _compress(cmp3, wa, wb, pea, peb, w2):
    b, nc, width = cmp3.shape
    const2 = lambda i: (0, 0)
    return pl.pallas_call(
        _compress_kernel,
        out_shape=jax.ShapeDtypeStruct((b, nc, 256), BF16),
        grid=(b,),
        in_specs=[pl.BlockSpec((1, nc, width), lambda i: (i, 0, 0)),
                  pl.BlockSpec(wa.shape, const2), pl.BlockSpec(wb.shape, const2),
                  pl.BlockSpec(pea.shape, const2), pl.BlockSpec(peb.shape, const2),
                  pl.BlockSpec(w2.shape, const2)],
        out_specs=pl.BlockSpec((1, nc, 256), lambda i: (i, 0, 0)),
        compiler_params=pltpu.CompilerParams(dimension_semantics=("parallel",), vmem_limit_bytes=VMEM_LIMIT),
        name="kv_compress",
    )(cmp3, wa, wb, pea, peb, w2)


def _softmax_tile(s, m_sc):
    cols = [s[:, j * LANES:(j + 1) * LANES] for j in range(s.shape[1] // LANES)]
    m_old = m_sc[...]
    m_new = jnp.maximum(m_old, jnp.max(functools.reduce(jnp.maximum, cols), axis=-1, keepdims=True))
    m_sc[...] = m_new
    p = jnp.concatenate([jnp.exp(c - m_new) for c in cols], axis=1).astype(BF16)
    return p, jnp.exp(m_old - m_new)


def _scale_both(acc, alpha):
    return jnp.concatenate([acc[:, :LANES] * alpha, acc[:, LANES:] * alpha], axis=1)


def _nsa_kernel(q_ref, gate_ref, kvc_ref, kaug_ref, vaug_ref, wink_ref, winv_ref, fc_ref, dn_ref, ov_ref,
                o_ref, qa_all, s_all, p_all, m_all, acc_all, *, nc, n_tiles):
    ib = pl.program_id(1)
    t0 = ib * Q_BLOCK
    lane = lax.broadcasted_iota(jnp.int32, (Q_BLOCK, LANES), 1)
    qrow = lax.broadcasted_iota(jnp.int32, (Q_BLOCK, LANES), 0)
    q = q_ref[...]
    gates = gate_ref[...]
    kc = kvc_ref[0, :, 0:LANES]
    vc = kvc_ref[0, :, LANES:2 * LANES]
    ov = ov_ref[...]

    def stack_heads(fn):
        return jnp.concatenate([fn(g) for g in range(GROUP)], axis=0)

    jq = 2 * ib + jnp.where(qrow >= SEL_BLOCK, 1, 0)
    near0 = 2 * ib - NEAR_BACK // SEL_BLOCK
    n_far = jnp.maximum(ib - 4, 0) // 4
    n_pairs = (n_far + 1) // 2
    kstart = jnp.maximum(t0 - NEAR_BACK, 0)
    j0 = pl.multiple_of(kstart - (t0 - NEAR_BACK), LANES)
    kstart = pl.multiple_of(kstart, LANES)
    wstart = jnp.maximum(t0 - WINDOW, 0)
    jw = pl.multiple_of(wstart - (t0 - NEAR_BACK), LANES)
    wstart = pl.multiple_of(wstart, LANES)
    fc_shift = nc + 8 + 8 * ib
    has_cmp = (t0 + lax.broadcasted_iota(jnp.int32, (ROWS, 1), 0) % Q_BLOCK) >= CMP_BLOCK - 1
    win_edge = jnp.where((lane > qrow) | (t0 < WINDOW), 0.0, NEG_INF)

    qhs, o_cs, imps = [], [], []
    for hkv in range(KV_HEADS):
        mine = (lane >= HEAD_DIM) if hkv else (lane < HEAD_DIM)
        qh = stack_heads(lambda g: jnp.where(mine, q[:, g * LANES:(g + 1) * LANES], 0.0).astype(BF16))
        lc = _dot_nt(qh, kc)
        lc = lc + stack_heads(lambda g: pltpu.roll(fc_ref[hkv * GROUP + g], fc_shift, 1)[:, :nc])
        e = jnp.exp(lc - jnp.max(lc, axis=-1, keepdims=True))
        pc = e / jnp.sum(e, axis=-1, keepdims=True) * jnp.where(has_cmp, 1.0, 0.0)
        o_cs.append(_dot(pc.astype(BF16), vc))
        psum = pc[0:Q_BLOCK] + pc[Q_BLOCK:2 * Q_BLOCK] + pc[2 * Q_BLOCK:3 * Q_BLOCK] + pc[3 * Q_BLOCK:]
        p_hi = psum.astype(BF16)
        p_lo = (psum - p_hi.astype(F32)).astype(BF16)
        imps.append(_dot(p_hi, ov) + _dot(p_lo, ov))
        qhs.append(qh)

    lane2 = jnp.concatenate([lane, lane], axis=0)
    jq2 = jnp.concatenate([jq, jq], axis=0)
    forced = (lane2 == 0) | (lane2 == jq2) | (lane2 == jq2 - 1)
    cand = jnp.where(forced, -3e38, jnp.where(lane2 > jq2, NEG_INF, jnp.concatenate(imps, axis=0)))
    lane_f = lane2.astype(F32)

    def pick_one(_, carry):
        cur, sel = carry
        mx = jnp.max(cur, axis=-1, keepdims=True)
        first = jnp.min(jnp.where(cur == mx, lane_f, float(LANES)), axis=-1, keepdims=True)
        pick = lane_f == first
        return jnp.where(pick, -3e38, cur), jnp.where(pick, 1.0, sel)

    _, sel2 = lax.fori_loop(0, N_SELECT - 3, pick_one, (cand, jnp.where(forced, 1.0, 0.0)))

    outs = []
    for hkv in range(KV_HEADS):
        qa_sc, s_buf, p_buf = qa_all.at[hkv], s_all.at[hkv], p_all.at[hkv]
        m_sc, acc_sc = m_all.at[hkv], acc_all.at[hkv]
        qh = qhs[hkv]
        sel = sel2[hkv * Q_BLOCK:(hkv + 1) * Q_BLOCK]

        m_near = jnp.where(sel > 0, 0.0, NEG_INF).astype(BF16)
        m_far = jnp.where((sel > 0) & (lane < near0), 0.0, NEG_INF).astype(BF16)
        qa_sc[0, :, 0:LANES] = qh
        qa_sc[0, :, LANES:2 * LANES] = jnp.concatenate([m_far] * GROUP, axis=0)
        qa_sc[1, :, 0:LANES] = qh
        qa_sc[1, :, LANES:2 * LANES] = jnp.concatenate([m_near] * GROUP, axis=0)
        m_sc[...] = jnp.full(m_sc.shape, -jnp.inf, F32)
        acc_sc[...] = jnp.zeros(acc_sc.shape, F32)
        p_buf[1] = jnp.zeros((ROWS, FAR_TILE), BF16)

        def far_logits(tile, slot):
            ks = pl.multiple_of(tile * FAR_TILE, FAR_TILE)
            s_buf[slot] = _dot_nt(qa_sc[0], kaug_ref[pl.ds(ks, FAR_TILE), :])

        def far_values(tile, slot):
            ks = pl.multiple_of(tile * FAR_TILE, FAR_TILE)
            return _dot(p_buf[slot], vaug_ref[pl.ds(ks, FAR_TILE), :])

        def absorb(slot, pending):
            p, alpha = _softmax_tile(s_buf[slot], m_sc)
            p_buf[slot] = p
            acc_sc[...] = _scale_both(acc_sc[...] + pending, alpha)

        far_logits(0, 0)

        sw = _dot_nt(qh, wink_ref[pl.ds(wstart, WIN_KEYS), :])
        sw = sw + stack_heads(lambda g: dn_ref[hkv * GROUP + g, :, pl.ds(jw, WIN_KEYS)])
        sw = jnp.concatenate([sw[:, :LANES] + jnp.concatenate([win_edge] * GROUP, axis=0), sw[:, LANES:]], axis=1)
        pw = jnp.exp(sw - jnp.max(sw, axis=-1, keepdims=True)).astype(BF16)
        ow = _dot(pw, winv_ref[pl.ds(wstart, WIN_KEYS), :])
        o_w = ow[:, :LANES] / ow[:, LANES:]

        def pair(i, carry):
            t = 2 * i
            far_logits(t + 1, 1)
            absorb(0, far_values(jnp.maximum(t - 1, 0), 1))
            far_logits(jnp.minimum(t + 2, n_tiles - 1), 0)
            absorb(1, far_values(t, 0))
            return carry

        lax.fori_loop(0, n_pairs, pair, 0)

        pending = far_values(jnp.maximum(2 * n_pairs - 1, 0), 1)
        acc = acc_sc[...]
        for i in range(NEAR // FAR_TILE):
            ks = pl.multiple_of(kstart + i * FAR_TILE, LANES)
            js = pl.multiple_of(j0 + i * FAR_TILE, LANES)
            s = _dot_nt(qa_sc[1], kaug_ref[pl.ds(ks, FAR_TILE), :])
            s = s + stack_heads(lambda g: dn_ref[hkv * GROUP + g, :, pl.ds(js, FAR_TILE)])
            p, alpha = _softmax_tile(s, m_sc)
            acc = _scale_both(acc + pending, alpha)
            pending = _dot(p, vaug_ref[pl.ds(ks, FAR_TILE), :])
        acc = acc + pending
        o_s = acc[:, :LANES] / acc[:, LANES:]

        o_c = o_cs[hkv]
        per_head = []
        for g in range(GROUP):
            c = 3 * (hkv * GROUP + g)
            rs = slice(g * Q_BLOCK, (g + 1) * Q_BLOCK)
            per_head.append(gates[:, c:c + 1] * o_c[rs] + gates[:, c + 1:c + 2] * o_s[rs]
                            + gates[:, c + 2:c + 3] * o_w[rs])
        outs.append(per_head)

    for g in range(GROUP):
        o_ref[:, g * LANES:(g + 1) * LANES] = jnp.where(lane < HEAD_DIM, outs[0][g], outs[1][g]).astype(BF16)


def _nsa(qs, gates, kvc, kaug, vaug, wink, winv, fc, dn, ov, *, batch, seq):
    nc = seq // CMP_STRIDE
    nq = seq // Q_BLOCK
    qrow = lambda b, i: (b * nq + i, 0)
    per_batch = lambda b, i: (b, 0)
    const2 = lambda b, i: (0, 0)
    const3 = lambda b, i: (0, 0, 0)
    once = pl.Buffered(1)
    return pl.pallas_call(
        functools.partial(_nsa_kernel, nc=nc, n_tiles=seq // FAR_TILE),
        out_shape=jax.ShapeDtypeStruct((batch * seq, 4 * LANES), BF16),
        grid=(batch, nq),
        in_specs=[pl.BlockSpec((Q_BLOCK, 4 * LANES), qrow),
                  pl.BlockSpec((Q_BLOCK, LANES), qrow),
                  pl.BlockSpec((1, nc, 256), lambda b, i: (b, 0, 0)),
                  pl.BlockSpec((seq, 256), per_batch, pipeline_mode=once),
                  pl.BlockSpec((seq, 256), per_batch, pipeline_mode=once),
                  pl.BlockSpec((seq, LANES), per_batch, pipeline_mode=once),
                  pl.BlockSpec((seq, 256), per_batch, pipeline_mode=once),
                  pl.BlockSpec(fc.shape, const3, pipeline_mode=once),
                  pl.BlockSpec(dn.shape, const3, pipeline_mode=once),
                  pl.BlockSpec(ov.shape, const2)],
        out_specs=pl.BlockSpec((Q_BLOCK, 4 * LANES), qrow),
        scratch_shapes=[pltpu.VMEM((KV_HEADS, 2, ROWS, 2 * LANES), BF16),
                        pltpu.VMEM((KV_HEADS, 2, ROWS, FAR_TILE), F32),
                        pltpu.VMEM((KV_HEADS, 2, ROWS, FAR_TILE), BF16),
                        pltpu.VMEM((KV_HEADS, ROWS, LANES), F32),
                        pltpu.VMEM((KV_HEADS, ROWS, 2 * LANES), F32)],
        compiler_params=pltpu.CompilerParams(dimension_semantics=("parallel", "arbitrary"),
                                             vmem_limit_bytes=VMEM_LIMIT),
        name="sparse_attention",
    )(qs, gates, kvc, kaug, vaug, wink, winv, fc, dn, ov)


def _outproj_kernel(x_ref, a_ref, b_ref, wa_ref, wb_ref, g_ref, beta_ref, o_ref):
    y = _dot(a_ref[...], wa_ref[...]) + _dot(b_ref[...], wb_ref[...])
    o_ref[...] = _layer_norm(ALPHA * x_ref[...] + y, g_ref[...], beta_ref[...])


def _outproj(x2d, a, b, wa, wb, g, beta, *, tm=512):
    t = x2d.shape[0]
    row = lambda i: (i, 0)
    const2 = lambda i: (0, 0)
    return pl.pallas_call(
        _outproj_kernel,
        out_shape=jax.ShapeDtypeStruct((t, D_MODEL), F32),
        grid=(t // tm,),
        in_specs=[pl.BlockSpec((tm, D_MODEL), row), pl.BlockSpec((tm, A_WIDTH), row), pl.BlockSpec((tm, 512), row),
                  pl.BlockSpec(wa.shape, const2), pl.BlockSpec(wb.shape, const2),
                  pl.BlockSpec(g.shape, const2), pl.BlockSpec(beta.shape, const2)],
        out_specs=pl.BlockSpec((tm, D_MODEL), row),
        compiler_params=pltpu.CompilerParams(dimension_semantics=("parallel",), vmem_limit_bytes=VMEM_LIMIT),
        name="out_proj_norm",
    )(x2d, a, b, wa, wb, g, beta)


def _ffn_kernel(x_ref, wg_ref, wu_ref, wd_ref, g_ref, beta_ref, o_ref, acc_ref, *, hc):
    x = x_ref[...]
    xb = x.astype(BF16)
    hidden = wg_ref.shape[1]
    for c in range(hidden // hc):
        cs = slice(c * hc, (c + 1) * hc)
        gate = _dot(xb, wg_ref[:, cs])
        up = _dot(xb, wu_ref[:, cs])
        part = _dot((jax.nn.silu(gate) * up).astype(BF16), wd_ref[cs, :])
        if c == 0:
            acc_ref[...] = part
        else:
            acc_ref[...] += part
    o_ref[...] = _layer_norm(ALPHA * x + acc_ref[...], g_ref[...], beta_ref[...])


def _ffn(x2d, wg, wu, wd, g, beta, *, tm=512, hc=256):
    t = x2d.shape[0]
    row = lambda i: (i, 0)
    const2 = lambda i: (0, 0)
    once = pl.Buffered(1)
    return pl.pallas_call(
        functools.partial(_ffn_kernel, hc=hc),
        out_shape=jax.ShapeDtypeStruct((t, D_MODEL), F32),
        grid=(t // tm,),
        in_specs=[pl.BlockSpec((tm, D_MODEL), row),
                  pl.BlockSpec(wg.shape, const2, pipeline_mode=once),
                  pl.BlockSpec(wu.shape, const2, pipeline_mode=once),
                  pl.BlockSpec(wd.shape, const2, pipeline_mode=once),
                  pl.BlockSpec(g.shape, const2), pl.BlockSpec(beta.shape, const2)],
        out_specs=pl.BlockSpec((tm, D_MODEL), row),
        scratch_shapes=[pltpu.VMEM((tm, D_MODEL), F32)],
        compiler_params=pltpu.CompilerParams(dimension_semantics=("parallel",), vmem_limit_bytes=VMEM_LIMIT),
        name="swiglu_ffn_norm",
    )(x2d, wg, wu, wd, g, beta)


def _conv_kernel(x_ref, win_ref, bin_ref, dww_ref, dwb_ref, lng_ref, lnb_ref, wout_ref, bout_ref,
                 g_ref, beta_ref, o_ref, buf_ref, cv_ref, *, tm, rc):
    @pl.when(pl.program_id(1) == 0)
    def _():
        buf_ref[0:HALO, :] = jnp.zeros((HALO, D_MODEL), F32)

    x = x_ref[...]
    h = _dot(x.astype(BF16), win_ref[...]) + bin_ref[...]
    buf_ref[HALO:HALO + tm, :] = h[:, :D_MODEL] * jax.nn.sigmoid(h[:, D_MODEL:])

    lead = HALO - (CONV_WIDTH - 1)

    def conv_rows(r, carry):
        r0 = pl.multiple_of(r * rc, rc)
        for lc in range(D_MODEL // LANES):
            ls = slice(lc * LANES, (lc + 1) * LANES)
            acc = jnp.broadcast_to(dwb_ref[:, ls], (rc, LANES))
            za = buf_ref[pl.ds(r0, rc + HALO), ls]
            for b in range(8):
                taps = range(b, CONV_WIDTH, 8)
                zb = za[lead + b:lead + b + rc + 8 * (len(taps) - 1)]
                for a, j in enumerate(taps):
                    acc = acc + zb[8 * a:8 * a + rc] * dww_ref[j:j + 1, ls]
            cv_ref[pl.ds(r0, rc), ls] = acc
        return carry

    lax.fori_loop(0, tm // rc, conv_rows, 0)
    buf_ref[0:HALO, :] = buf_ref[tm:tm + HALO, :]

    y = jax.nn.silu(_layer_norm(cv_ref[...], lng_ref[...], lnb_ref[...]))
    y = _dot(y.astype(BF16), wout_ref[...]) + bout_ref[...]
    o_ref[...] = _layer_norm(ALPHA * x + y, g_ref[...], beta_ref[...])


def _conv(x2d, w_in, b_in, dw_w, dw_b, ln_g, ln_b, w_out, b_out, g, beta, *, batch, seq, tm=512, rc=64):
    nt = seq // tm
    row = lambda b, i: (b * nt + i, 0)
    const2 = lambda b, i: (0, 0)
    consts = (w_in, b_in, dw_w, dw_b, ln_g, ln_b, w_out, b_out, g, beta)
    return pl.pallas_call(
        functools.partial(_conv_kernel, tm=tm, rc=rc),
        out_shape=jax.ShapeDtypeStruct((batch * seq, D_MODEL), F32),
        grid=(batch, nt),
        in_specs=[pl.BlockSpec((tm, D_MODEL), row)] + [pl.BlockSpec(c.shape, const2) for c in consts],
        out_specs=pl.BlockSpec((tm, D_MODEL), row),
        scratch_shapes=[pltpu.VMEM((HALO + tm, D_MODEL), F32), pltpu.VMEM((tm, D_MODEL), F32)],
        compiler_params=pltpu.CompilerParams(dimension_semantics=("parallel", "arbitrary"),
                                             vmem_limit_bytes=VMEM_LIMIT),
        name="conv_module_norm",
    )(x2d, *consts)


def _head_perm():
    p = np.arange(HEADS * HEAD_DIM)
    g, half, d = p // LANES, (p % LANES) // HEAD_DIM, p % HEAD_DIM
    return (g + GROUP * half) * HEAD_DIM + d


def _compress_weights(pe_k, w1_k, w2_k, pe_v, w1_v, w2_v):
    eye = jnp.eye(KV_HEADS, dtype=F32)
    w1 = jnp.stack([w1_k, w1_v]).reshape(2, CMP_BLOCK, HEAD_DIM, HEAD_DIM)
    w1x = jnp.einsum('klde,km,hn->lkhdmne', w1, eye, eye).reshape(CMP_BLOCK * 256, 256)
    half = CMP_STRIDE * 256
    pe = jnp.stack([pe_k, pe_v])
    pex = jnp.broadcast_to(pe.transpose(1, 0, 2)[:, :, None, :], (CMP_BLOCK, 2, KV_HEADS, HEAD_DIM))
    pex = pex.reshape(2, half)
    w2 = jnp.stack([w2_k, w2_v])
    w2x = jnp.einsum('kde,km,hn->khdmne', w2, eye, eye).reshape(256, 256)
    pea = jnp.broadcast_to(pex[0:1], (8, half)).astype(BF16)
    peb = jnp.broadcast_to(pex[1:2], (8, half)).astype(BF16)
    return w1x[:half].astype(BF16), w1x[half:].astype(BF16), pea, peb, w2x.astype(BF16)


def _overlap_matrix(nc):
    c0 = np.arange(nc)[:, None] * CMP_STRIDE
    s0 = np.arange(LANES)[None, :] * SEL_BLOCK
    ov = (c0 < s0 + SEL_BLOCK) & (c0 + CMP_BLOCK > s0) & (np.arange(nc)[:, None] < nc - 1)
    return jnp.asarray(ov, dtype=BF16)


def _even_layer(x2d, rel_bias, w_in, w_out, ln_g, ln_b, w_s, b_s, pe_k, w1_k, w2_k, pe_v, w1_v, w2_v,
                norm_g, norm_b, *, batch, seq):
    nc = seq // CMP_STRIDE
    assert seq % FAR_TILE == 0 and seq >= NEAR and seq // SEL_BLOCK <= LANES
    perm = _head_perm()
    qw = HEADS * HEAD_DIM
    o = 2 * A_WIDTH
    wuv = w_in[:, :o].astype(BF16)
    wq = w_in[:, o:o + qw][:, perm]
    wkv = w_in[:, o + qw:o + qw + 768]
    wgt = jnp.pad(w_in[:, o + qw + 768:], ((0, 0), (0, LANES - 3 * HEADS)))
    wqkv = jnp.concatenate([wq, wkv, wgt], axis=1).astype(BF16)
    bs = jnp.broadcast_to(b_s[:, :, None], (A_GROUPS, CHUNK, A_WIDTH // A_GROUPS))
    a_out, qs, cmp2, kaug, vaug, wink, winv, gates = _proj(
        x2d, wuv, wqkv, ln_g[None, :], ln_b[None, :], w_s, bs, seq=seq)

    kvc = _compress(cmp2.reshape(batch, nc, CMP_STRIDE * 256),
                    *_compress_weights(pe_k, w1_k, w2_k, pe_v, w1_v, w2_v))

    fc, dn = _bias_tables(_bucket_thresholds(seq), rel_bias.T.reshape(-1), nc)
    b_out = _nsa(qs, gates, kvc, kaug, vaug, wink, winv, fc, dn, _overlap_matrix(nc), batch=batch, seq=seq)

    wo_a = w_out[:A_WIDTH].astype(BF16)
    wo_b = w_out[A_WIDTH:][perm].astype(BF16)
    return _outproj(x2d, a_out, b_out, wo_a, wo_b, norm_g[None, :], norm_b[None, :])


def kernel(x, rel_bias, hyb_w_in, hyb_w_out, gmlp_ln_g, gmlp_ln_b, gmlp_w_s, gmlp_b_s, cmp_pe_k, cmp_w1_k, cmp_w2_k, cmp_pe_v, cmp_w1_v, cmp_w2_v, conv_w_in, conv_b_in, conv_dw_w, conv_dw_b, conv_ln_g, conv_ln_b, conv_w_out, conv_b_out, ffn_w_gate, ffn_w_up, ffn_w_down, norm_mix_g, norm_mix_b, norm_ffn_g, norm_ffn_b):
    batch, seq, d = x.shape
    h = x.reshape(batch * seq, d)
    for layer in range(DEPTH):
        i = layer // 2
        if layer % 2 == 0:
            h = _even_layer(h, rel_bias, hyb_w_in[i], hyb_w_out[i], gmlp_ln_g[i], gmlp_ln_b[i],
                            gmlp_w_s[i], gmlp_b_s[i], cmp_pe_k[i], cmp_w1_k[i], cmp_w2_k[i],
                            cmp_pe_v[i], cmp_w1_v[i], cmp_w2_v[i],
                            norm_mix_g[layer], norm_mix_b[layer], batch=batch, seq=seq)
        else:
            h = _conv(h, conv_w_in[i].astype(BF16), conv_b_in[i][None, :], conv_dw_w[i], conv_dw_b[i][None, :],
                      conv_ln_g[i][None, :], conv_ln_b[i][None, :], conv_w_out[i].astype(BF16),
                      conv_b_out[i][None, :], norm_mix_g[layer][None, :], norm_mix_b[layer][None, :],
                      batch=batch, seq=seq)
        h = _ffn(h, ffn_w_gate[layer].astype(BF16), ffn_w_up[layer].astype(BF16),
                 ffn_w_down[layer].astype(BF16), norm_ffn_g[layer][None, :], norm_ffn_b[layer][None, :])
    return h.reshape(batch, seq, d)
```

```python
import functools
import math

import numpy as np
import jax
import jax.numpy as jnp
from jax import lax
from jax.experimental import pallas as pl
from jax.experimental.pallas import tpu as pltpu

F32 = jnp.float32
BF16 = jnp.bfloat16

D_MODEL = 1024
DEPTH = 2
ALPHA = (2 * DEPTH) ** 0.25
CHUNK = 128
A_WIDTH = D_MODEL // 2
A_GROUPS = 4
HEADS = 8
KV_HEADS = 2
GROUP = HEADS // KV_HEADS
HEAD_DIM = (D_MODEL // 2) // HEADS
CMP_STRIDE = 16
CMP_BLOCK = 32
SEL_BLOCK = 64
N_SELECT = 16
WINDOW = 512
Q_BLOCK = 128
N_BUCKETS = 32
MAX_DISTANCE = 1024
CONV_WIDTH = 31
NEG_INF = -1e30
FORCE = 1e9
LN_EPS = 1e-5

LANES = 128
VMEM_LIMIT = 56 * 1024 * 1024

ROWS = GROUP * Q_BLOCK
FAR_TILE = 512
NEAR_TILES = 8
NEAR = NEAR_TILES * Q_BLOCK
NEAR_BACK = NEAR - Q_BLOCK
WIN_KEYS = WINDOW + Q_BLOCK
DN_WIDTH = NEAR + NEAR_BACK
HALO = 32


def _dot(a, b):
    return jnp.dot(a, b, preferred_element_type=F32)


def _dot_nt(a, b):
    return lax.dot_general(a, b, (((1,), (1,)), ((), ())), preferred_element_type=F32)


def _layer_norm(z, g, b):
    mu = jnp.mean(z, axis=-1, keepdims=True)
    d = z - mu
    var = jnp.mean(d * d, axis=-1, keepdims=True)
    return d * lax.rsqrt(var + LN_EPS) * g + b


def _t5_bucket(dist):
    n = jnp.maximum(dist, 0)
    max_exact = N_BUCKETS // 2
    nf = jnp.maximum(n, 1).astype(jnp.float32)
    large = max_exact + (jnp.log(nf / max_exact) / math.log(MAX_DISTANCE / max_exact)
                         * (N_BUCKETS - max_exact)).astype(jnp.int32)
    large = jnp.minimum(large, N_BUCKETS - 1)
    return jnp.where(n < max_exact, n, large)


def _bucket_thresholds(seq):
    b = _t5_bucket(jnp.arange(seq, dtype=jnp.int32))
    j = jnp.arange(N_BUCKETS, dtype=jnp.int32)
    return jnp.sum((b[None, :] < j[:, None]).astype(jnp.int32), axis=1).astype(jnp.int32)


def _tables_kernel(thr_ref, rb_ref, fc_ref, dn_ref, *, nc):
    h = pl.program_id(0)
    base = h * N_BUCKETS
    far = rb_ref[base + N_BUCKETS - 1]

    def bias_of(dist):
        val = jnp.full(dist.shape, far, F32)
        for j in range(N_BUCKETS - 1, 0, -1):
            val = jnp.where(dist < thr_ref[j], rb_ref[base + j - 1], val)
        return val

    ql = lax.broadcasted_iota(jnp.int32, (Q_BLOCK, LANES), 0)
    ln = lax.broadcasted_iota(jnp.int32, (Q_BLOCK, LANES), 1)
    for c in range(2 * nc // Q_BLOCK):
        dist = ln - CMP_STRIDE * (ql + (c * Q_BLOCK - (nc - 8))) - (CMP_BLOCK - 1)
        fc_ref[0, c * Q_BLOCK:(c + 1) * Q_BLOCK, :] = jnp.where(dist >= 0, bias_of(dist), NEG_INF)
    for c in range(DN_WIDTH // LANES):
        dist = ql + (NEAR_BACK - c * LANES) - ln
        dn_ref[0, :, c * LANES:(c + 1) * LANES] = jnp.where(dist >= 0, bias_of(dist) - far, NEG_INF)


def _bias_tables(thr, rb_flat, nc):
    return pl.pallas_call(
        functools.partial(_tables_kernel, nc=nc),
        out_shape=(jax.ShapeDtypeStruct((HEADS, 2 * nc, Q_BLOCK), F32),
                   jax.ShapeDtypeStruct((HEADS, Q_BLOCK, DN_WIDTH), F32)),
        grid=(HEADS,),
        in_specs=[pl.BlockSpec(memory_space=pltpu.SMEM), pl.BlockSpec(memory_space=pltpu.SMEM)],
        out_specs=(pl.BlockSpec((1, 2 * nc, Q_BLOCK), lambda h: (h, 0, 0)),
                   pl.BlockSpec((1, Q_BLOCK, DN_WIDTH), lambda h: (h, 0, 0))),
        name="bias_tables",
    )(thr, rb_flat)


def _proj_kernel(x_ref, wuv_ref, wqkv_ref, lng_ref, lnb_ref, ws_ref, bs_ref,
                 a_ref, q_ref, cmp_ref, kaug_ref, vaug_ref, wink_ref, winv_ref, gate_ref, *, tm, seq):
    xb = x_ref[...].astype(BF16)
    uv = jax.nn.gelu(_dot(xb, wuv_ref[...]))
    u = uv[:, :A_WIDTH]
    v = _layer_norm(uv[:, A_WIDTH:], lng_ref[...], lnb_ref[...]).astype(BF16)
    row = lax.broadcasted_iota(jnp.int32, (CHUNK, CHUNK), 0)
    col = lax.broadcasted_iota(jnp.int32, (CHUNK, CHUNK), 1)
    gd = A_WIDTH // A_GROUPS
    for g in range(A_GROUPS):
        w = jnp.where(col <= row, ws_ref[g], 0.0).astype(BF16)
        for c in range(tm // CHUNK):
            rs = slice(c * CHUNK, (c + 1) * CHUNK)
            cs = slice(g * gd, (g + 1) * gd)
            s = _dot(w, v[rs, cs]) + bs_ref[g]
            a_ref[rs, cs] = (u[rs, cs] * s).astype(BF16)

    h = _dot(xb, wqkv_ref[...])
    q_ref[...] = (h[:, 0:512] * (HEAD_DIM ** -0.5)).astype(BF16)
    cmp_ref[...] = h[:, 512:768].astype(BF16)
    kaug_ref[:, 0:LANES] = h[:, 768:896].astype(BF16)
    pos = (pl.program_id(0) * tm) % seq + lax.broadcasted_iota(jnp.int32, (tm, LANES), 0)
    blk = lax.broadcasted_iota(jnp.int32, (tm, LANES), 1)
    kaug_ref[:, LANES:2 * LANES] = jnp.where(pos // SEL_BLOCK == blk, 1.0, 0.0).astype(BF16)
    ones = jnp.ones((tm, LANES), BF16)
    vaug_ref[:, 0:LANES] = h[:, 896:1024].astype(BF16)
    vaug_ref[:, LANES:2 * LANES] = ones
    wink_ref[...] = h[:, 1024:1152].astype(BF16)
    winv_ref[:, 0:LANES] = h[:, 1152:1280].astype(BF16)
    winv_ref[:, LANES:2 * LANES] = ones
    gate_ref[...] = jax.nn.sigmoid(h[:, 1280:1408])


def _proj(x2d, wuv, wqkv, lng, lnb, ws, bs, *, seq, tm=512):
    t = x2d.shape[0]
    row = lambda i: (i, 0)
    const2 = lambda i: (0, 0)
    const3 = lambda i: (0, 0, 0)
    outs = [(A_WIDTH, BF16), (512, BF16), (256, BF16), (256, BF16), (256, BF16), (128, BF16), (256, BF16),
            (128, F32)]
    return pl.pallas_call(
        functools.partial(_proj_kernel, tm=tm, seq=seq),
        out_shape=tuple(jax.ShapeDtypeStruct((t, w), dt) for w, dt in outs),
        grid=(t // tm,),
        in_specs=[pl.BlockSpec((tm, D_MODEL), row),
                  pl.BlockSpec(wuv.shape, const2),
                  pl.BlockSpec(wqkv.shape, const2),
                  pl.BlockSpec(lng.shape, const2),
                  pl.BlockSpec(lnb.shape, const2),
                  pl.BlockSpec(ws.shape, const3),
                  pl.BlockSpec(bs.shape, const3)],
        out_specs=tuple(pl.BlockSpec((tm, w), row) for w, _ in outs),
        compiler_params=pltpu.CompilerParams(dimension_semantics=("parallel",), vmem_limit_bytes=VMEM_LIMIT),
        name="in_proj_gmlp",
    )(x2d, wuv, wqkv, lng, lnb, ws, bs)


def _compress_kernel(r_ref, wa_ref, wb_ref, pea_ref, peb_ref, w2_ref, kc_ref, vct_ref):
    r = r_ref[0]
    top = _dot(r, wa_ref[...])
    bot = _dot(r, wb_ref[...])
    pe = _dot(pea_ref[...], wa_ref[...]) + _dot(peb_ref[...], wb_ref[...])
    hid = top + pltpu.roll(bot, bot.shape[0] - 1, 0) + pe[0:1]
    out = _dot(jax.nn.gelu(hid).astype(BF16), w2_ref[...])
    kc_ref[0] = out[:, 0:LANES].astype(BF16)
    vct_ref[0] = out[:, LANES:2 * LANES].T.astype(BF16)


def _compress(cmp3, wa, wb, pea, peb, w2):
    b, nc, width = cmp3.shape
    const2 = lambda i: (0, 0)
    return pl.pallas_call(
        _compress_kernel,
        out_shape=(jax.ShapeDtypeStruct((b, nc, LANES), BF16), jax.ShapeDtypeStruct((b, LANES, nc), BF16)),
        grid=(b,),
        in_specs=[pl.BlockSpec((1, nc, width), lambda i: (i, 0, 0)),
                  pl.BlockSpec(wa.shape, const2), pl.BlockSpec(wb.shape, const2),
                  pl.BlockSpec(pea.shape, const2), pl.BlockSpec(peb.shape, const2),
                  pl.BlockSpec(w2.shape, const2)],
        out_specs=(pl.BlockSpec((1, nc, LANES), lambda i: (i, 0, 0)),
                   pl.BlockSpec((1, LANES, nc), lambda i: (i, 0, 0))),
        compiler_params=pltpu.CompilerParams(dimension_semantics=("parallel",), vmem_limit_bytes=VMEM_LIMIT),
        name="kv_compress",
    )(cmp3, wa, wb, pea, peb, w2)


def _softmax_tile(s, m_sc):
    cols = [s[:, j * LANES:(j + 1) * LANES] for j in range(s.shape[1] // LANES)]
    m_old = m_sc[...]
    m_new = jnp.maximum(m_old, jnp.max(functools.reduce(jnp.maximum, cols), axis=-1, keepdims=True))
    m_sc[...] = m_new
    p = jnp.concatenate([jnp.exp(c - m_new) for c in cols], axis=1).astype(BF16)
    return p, jnp.exp(m_old - m_new)


def _scale_both(acc, alpha):
    return jnp.concatenate([acc[:, :LANES] * alpha, acc[:, LANES:] * alpha], axis=1)


def _nsa_kernel(q_ref, gate_ref, kc_ref, vct_ref, kaug_ref, vaug_ref, wink_ref, winv_ref, fc_ref, dn_ref, ovt_ref,
                o_ref, qa_all, s_all, p_all, m_all, acc_all, *, nc, n_tiles):
    ib = pl.program_id(1)
    t0 = ib * Q_BLOCK
    lane = lax.broadcasted_iota(jnp.int32, (Q_BLOCK, LANES), 1)
    qrow = lax.broadcasted_iota(jnp.int32, (Q_BLOCK, LANES), 0)
    q = q_ref[...]
    gates = gate_ref[...]
    kc = kc_ref[0]
    vct = vct_ref[0]
    ovt = ovt_ref[...]

    def stack_heads(fn):
        return jnp.concatenate([fn(g) for g in range(GROUP)], axis=0)

    near0 = 2 * ib - NEAR_BACK // SEL_BLOCK
    n_far = jnp.maximum(ib - 4, 0) // 4
    n_pairs = (n_far + 1) // 2
    kstart = jnp.maximum(t0 - NEAR_BACK, 0)
    j0 = pl.multiple_of(kstart - (t0 - NEAR_BACK), LANES)
    kstart = pl.multiple_of(kstart, LANES)
    wstart = jnp.maximum(t0 - WINDOW, 0)
    jw = pl.multiple_of(wstart - (t0 - NEAR_BACK), LANES)
    wstart = pl.multiple_of(wstart, LANES)
    win_edge = jnp.where((lane > qrow) | (t0 < WINDOW), 0.0, NEG_INF)

    fc_row = pl.multiple_of((nc - 8) - 8 * ib, 8)
    qcol = lax.broadcasted_iota(jnp.int32, (1, ROWS), 1) % Q_BLOCK
    has_cmp = jnp.where(t0 + qcol >= CMP_BLOCK - 1, 1.0, 0.0)
    qhs, o_cts, imps = [], [], []
    for hkv in range(KV_HEADS):
        mine = (lane >= HEAD_DIM) if hkv else (lane < HEAD_DIM)
        qh = stack_heads(lambda g: jnp.where(mine, q[:, g * LANES:(g + 1) * LANES], 0.0).astype(BF16))
        lct = _dot_nt(kc, qh)
        lct = lct + jnp.concatenate([fc_ref[hkv * GROUP + g, pl.ds(fc_row, nc), :] for g in range(GROUP)], axis=1)
        e = jnp.exp(lct - jnp.max(lct, axis=0, keepdims=True))
        pct = e * (has_cmp / jnp.sum(e, axis=0, keepdims=True))
        o_cts.append(_dot(vct, pct.astype(BF16)))
        psum = pct[:, 0:LANES] + pct[:, LANES:2 * LANES] + pct[:, 2 * LANES:3 * LANES] + pct[:, 3 * LANES:]
        p_hi = psum.astype(BF16)
        p_lo = (psum - p_hi.astype(F32)).astype(BF16)
        imps.append(_dot(ovt, p_hi) + _dot(ovt, p_lo))
        qhs.append(qh)

    blk = lax.broadcasted_iota(jnp.int32, (LANES, KV_HEADS * Q_BLOCK), 0)
    col = lax.broadcasted_iota(jnp.int32, (LANES, KV_HEADS * Q_BLOCK), 1)
    jq = 2 * ib + jnp.where(col % Q_BLOCK >= SEL_BLOCK, 1, 0)
    forced = (blk == 0) | (blk == jq) | (blk == jq - 1)
    cand = jnp.where(forced, -3e38, jnp.where(blk > jq, NEG_INF, jnp.concatenate(imps, axis=1)))
    blk_f = blk.astype(F32)

    def pick_one(_, carry):
        cur, chosen = carry
        mx = jnp.max(cur, axis=0, keepdims=True)
        first = jnp.min(jnp.where(cur == mx, blk_f, float(LANES)), axis=0, keepdims=True)
        pick = blk_f == first
        return jnp.where(pick, -3e38, cur), jnp.where(pick, 1.0, chosen)

    _, sel_t = lax.fori_loop(0, N_SELECT - 3, pick_one, (cand, jnp.where(forced, 1.0, 0.0)))

    outs = []
    for hkv in range(KV_HEADS):
        qa_sc, s_buf, p_buf = qa_all.at[hkv], s_all.at[hkv], p_all.at[hkv]
        m_sc, acc_sc = m_all.at[hkv], acc_all.at[hkv]
        qh = qhs[hkv]
        sel = sel_t[:, hkv * Q_BLOCK:(hkv + 1) * Q_BLOCK].T

        m_near = jnp.where(sel > 0, 0.0, NEG_INF).astype(BF16)
        m_far = jnp.where((sel > 0) & (lane < near0), 0.0, NEG_INF).astype(BF16)
        qa_sc[0, :, 0:LANES] = qh
        qa_sc[0, :, LANES:2 * LANES] = jnp.concatenate([m_far] * GROUP, axis=0)
        qa_sc[1, :, 0:LANES] = qh
        qa_sc[1, :, LANES:2 * LANES] = jnp.concatenate([m_near] * GROUP, axis=0)
        m_sc[...] = jnp.full(m_sc.shape, -jnp.inf, F32)
        acc_sc[...] = jnp.zeros(acc_sc.shape, F32)
        p_buf[1] = jnp.zeros((ROWS, FAR_TILE), BF16)

        def far_logits(tile, slot):
            ks = pl.multiple_of(tile * FAR_TILE, FAR_TILE)
            s_buf[slot] = _dot_nt(qa_sc[0], kaug_ref[pl.ds(ks, FAR_TILE), :])

        def far_values(tile, slot):
            ks = pl.multiple_of(tile * FAR_TILE, FAR_TILE)
            return _dot(p_buf[slot], vaug_ref[pl.ds(ks, FAR_TILE), :])

        def absorb(slot, pending):
            p, alpha = _softmax_tile(s_buf[slot], m_sc)
            p_buf[slot] = p
            acc_sc[...] = _scale_both(acc_sc[...] + pending, alpha)

        far_logits(0, 0)

        sw = _dot_nt(qh, wink_ref[pl.ds(wstart, WIN_KEYS), :])
        sw = sw + stack_heads(lambda g: dn_ref[hkv * GROUP + g, :, pl.ds(jw, WIN_KEYS)])
        sw = jnp.concatenate([sw[:, :LANES] + jnp.concatenate([win_edge] * GROUP, axis=0), sw[:, LANES:]], axis=1)
        pw = jnp.exp(sw - jnp.max(sw, axis=-1, keepdims=True)).astype(BF16)
        ow = _dot(pw, winv_ref[pl.ds(wstart, WIN_KEYS), :])
        o_w = ow[:, :LANES] / ow[:, LANES:]

        def pair(i, carry):
            t = 2 * i
            far_logits(t + 1, 1)
            absorb(0, far_values(jnp.maximum(t - 1, 0), 1))
            far_logits(jnp.minimum(t + 2, n_tiles - 1), 0)
            absorb(1, far_values(t, 0))
            return carry

        lax.fori_loop(0, n_pairs, pair, 0)

        pending = far_values(jnp.maximum(2 * n_pairs - 1, 0), 1)
        acc = acc_sc[...]
        for i in range(NEAR // FAR_TILE):
            ks = pl.multiple_of(kstart + i * FAR_TILE, LANES)
            js = pl.multiple_of(j0 + i * FAR_TILE, LANES)
            s = _dot_nt(qa_sc[1], kaug_ref[pl.ds(ks, FAR_TILE), :])
            s = s + stack_heads(lambda g: dn_ref[hkv * GROUP + g, :, pl.ds(js, FAR_TILE)])
            p, alpha = _softmax_tile(s, m_sc)
            acc = _scale_both(acc + pending, alpha)
            pending = _dot(p, vaug_ref[pl.ds(ks, FAR_TILE), :])
        acc = acc + pending
        o_s = acc[:, :LANES] / acc[:, LANES:]

        per_head = []
        for g in range(GROUP):
            c = 3 * (hkv * GROUP + g)
            rs = slice(g * Q_BLOCK, (g + 1) * Q_BLOCK)
            o_c = o_cts[hkv][:, rs].T
            per_head.append(gates[:, c:c + 1] * o_c + gates[:, c + 1:c + 2] * o_s[rs]
                            + gates[:, c + 2:c + 3] * o_w[rs])
        outs.append(per_head)

    for g in range(GROUP):
        o_ref[:, g * LANES:(g + 1) * LANES] = jnp.where(lane < HEAD_DIM, outs[0][g], outs[1][g]).astype(BF16)


def _nsa(qs, gates, kc, vct, kaug, vaug, wink, winv, fc, dn, ovt, *, batch, seq):
    nc = seq // CMP_STRIDE
    nq = seq // Q_BLOCK
    qrow = lambda b, i: (b * nq + i, 0)
    per_batch = lambda b, i: (b, 0)
    const2 = lambda b, i: (0, 0)
    const3 = lambda b, i: (0, 0, 0)
    once = pl.Buffered(1)
    return pl.pallas_call(
        functools.partial(_nsa_kernel, nc=nc, n_tiles=seq // FAR_TILE),
        out_shape=jax.ShapeDtypeStruct((batch * seq, 4 * LANES), BF16),
        grid=(batch, nq),
        in_specs=[pl.BlockSpec((Q_BLOCK, 4 * LANES), qrow),
                  pl.BlockSpec((Q_BLOCK, LANES), qrow),
                  pl.BlockSpec((1, nc, LANES), lambda b, i: (b, 0, 0)),
                  pl.BlockSpec((1, LANES, nc), lambda b, i: (b, 0, 0)),
                  pl.BlockSpec((seq, 256), per_batch, pipeline_mode=once),
                  pl.BlockSpec((seq, 256), per_batch, pipeline_mode=once),
                  pl.BlockSpec((seq, LANES), per_batch, pipeline_mode=once),
                  pl.BlockSpec((seq, 256), per_batch, pipeline_mode=once),
                  pl.BlockSpec(fc.shape, const3, pipeline_mode=once),
                  pl.BlockSpec(dn.shape, const3, pipeline_mode=once),
                  pl.BlockSpec(ovt.shape, const2)],
        out_specs=pl.BlockSpec((Q_BLOCK, 4 * LANES), qrow),
        scratch_shapes=[pltpu.VMEM((KV_HEADS, 2, ROWS, 2 * LANES), BF16),
                        pltpu.VMEM((KV_HEADS, 2, ROWS, FAR_TILE), F32),
                        pltpu.VMEM((KV_HEADS, 2, ROWS, FAR_TILE), BF16),
                        pltpu.VMEM((KV_HEADS, ROWS, LANES), F32),
                        pltpu.VMEM((KV_HEADS, ROWS, 2 * LANES), F32)],
        compiler_params=pltpu.CompilerParams(dimension_semantics=("parallel", "arbitrary"),
                                             vmem_limit_bytes=VMEM_LIMIT),
        name="sparse_attention",
    )(qs, gates, kc, vct, kaug, vaug, wink, winv, fc, dn, ovt)


def _outproj_kernel(x_ref, a_ref, b_ref, wa_ref, wb_ref, g_ref, beta_ref, o_ref):
    y = _dot(a_ref[...], wa_ref[...]) + _dot(b_ref[...], wb_ref[...])
    o_ref[...] = _layer_norm(ALPHA * x_ref[...] + y, g_ref[...], beta_ref[...])


def _outproj(x2d, a, b, wa, wb, g, beta, *, tm=512):
    t = x2d.shape[0]
    row = lambda i: (i, 0)
    const2 = lambda i: (0, 0)
    return pl.pallas_call(
        _outproj_kernel,
        out_shape=jax.ShapeDtypeStruct((t, D_MODEL), F32),
        grid=(t // tm,),
        in_specs=[pl.BlockSpec((tm, D_MODEL), row), pl.BlockSpec((tm, A_WIDTH), row), pl.BlockSpec((tm, 512), row),
                  pl.BlockSpec(wa.shape, const2), pl.BlockSpec(wb.shape, const2),
                  pl.BlockSpec(g.shape, const2), pl.BlockSpec(beta.shape, const2)],
        out_specs=pl.BlockSpec((tm, D_MODEL), row),
        compiler_params=pltpu.CompilerParams(dimension_semantics=("parallel",), vmem_limit_bytes=VMEM_LIMIT),
        name="out_proj_norm",
    )(x2d, a, b, wa, wb, g, beta)


def _ffn_kernel(x_ref, wg_ref, wu_ref, wd_ref, g_ref, beta_ref, o_ref, acc_ref, *, hc):
    x = x_ref[...]
    xb = x.astype(BF16)
    hidden = wg_ref.shape[1]
    for c in range(hidden // hc):
        cs = slice(c * hc, (c + 1) * hc)
        gate = _dot(xb, wg_ref[:, cs])
        up = _dot(xb, wu_ref[:, cs])
        part = _dot((jax.nn.silu(gate) * up).astype(BF16), wd_ref[cs, :])
        if c == 0:
            acc_ref[...] = part
        else:
            acc_ref[...] += part
    o_ref[...] = _layer_norm(ALPHA * x + acc_ref[...], g_ref[...], beta_ref[...])


def _ffn(x2d, wg, wu, wd, g, beta, *, tm=512, hc=256):
    t = x2d.shape[0]
    row = lambda i: (i, 0)
    const2 = lambda i: (0, 0)
    once = pl.Buffered(1)
    return pl.pallas_call(
        functools.partial(_ffn_kernel, hc=hc),
        out_shape=jax.ShapeDtypeStruct((t, D_MODEL), F32),
        grid=(t // tm,),
        in_specs=[pl.BlockSpec((tm, D_MODEL), row),
                  pl.BlockSpec(wg.shape, const2, pipeline_mode=once),
                  pl.BlockSpec(wu.shape, const2, pipeline_mode=once),
                  pl.BlockSpec(wd.shape, const2, pipeline_mode=once),
                  pl.BlockSpec(g.shape, const2), pl.BlockSpec(beta.shape, const2)],
        out_specs=pl.BlockSpec((tm, D_MODEL), row),
        scratch_shapes=[pltpu.VMEM((tm, D_MODEL), F32)],
        compiler_params=pltpu.CompilerParams(dimension_semantics=("parallel",), vmem_limit_bytes=VMEM_LIMIT),
        name="swiglu_ffn_norm",
    )(x2d, wg, wu, wd, g, beta)


def _conv_kernel(x_ref, win_ref, bin_ref, dww_ref, dwb_ref, lng_ref, lnb_ref, wout_ref, bout_ref,
                 g_ref, beta_ref, o_ref, buf_ref, cv_ref, *, tm, rc):
    @pl.when(pl.program_id(1) == 0)
    def _():
        buf_ref[0:HALO, :] = jnp.zeros((HALO, D_MODEL), F32)

    x = x_ref[...]
    h = _dot(x.astype(BF16), win_ref[...]) + bin_ref[...]
    buf_ref[HALO:HALO + tm, :] = h[:, :D_MODEL] * jax.nn.sigmoid(h[:, D_MODEL:])

    lead = HALO - (CONV_WIDTH - 1)

    def conv_rows(r, carry):
        r0 = pl.multiple_of(r * rc, rc)
        for lc in range(D_MODEL // LANES):
            ls = slice(lc * LANES, (lc + 1) * LANES)
            acc = jnp.broadcast_to(dwb_ref[:, ls], (rc, LANES))
            za = buf_ref[pl.ds(r0, rc + HALO), ls]
            for b in range(8):
                taps = range(b, CONV_WIDTH, 8)
                zb = za[lead + b:lead + b + rc + 8 * (len(taps) - 1)]
                for a, j in enumerate(taps):
                    acc = acc + zb[8 * a:8 * a + rc] * dww_ref[j:j + 1, ls]
            cv_ref[pl.ds(r0, rc), ls] = acc
        return carry

    lax.fori_loop(0, tm // rc, conv_rows, 0)
    buf_ref[0:HALO, :] = buf_ref[tm:tm + HALO, :]

    y = jax.nn.silu(_layer_norm(cv_ref[...], lng_ref[...], lnb_ref[...]))
    y = _dot(y.astype(BF16), wout_ref[...]) + bout_ref[...]
    o_ref[...] = _layer_norm(ALPHA * x + y, g_ref[...], beta_ref[...])


def _conv(x2d, w_in, b_in, dw_w, dw_b, ln_g, ln_b, w_out, b_out, g, beta, *, batch, seq, tm=512, rc=64):
    nt = seq // tm
    row = lambda b, i: (b * nt + i, 0)
    const2 = lambda b, i: (0, 0)
    consts = (w_in, b_in, dw_w, dw_b, ln_g, ln_b, w_out, b_out, g, beta)
    return pl.pallas_call(
        functools.partial(_conv_kernel, tm=tm, rc=rc),
        out_shape=jax.ShapeDtypeStruct((batch * seq, D_MODEL), F32),
        grid=(batch, nt),
        in_specs=[pl.BlockSpec((tm, D_MODEL), row)] + [pl.BlockSpec(c.shape, const2) for c in consts],
        out_specs=pl.BlockSpec((tm, D_MODEL), row),
        scratch_shapes=[pltpu.VMEM((HALO + tm, D_MODEL), F32), pltpu.VMEM((tm, D_MODEL), F32)],
        compiler_params=pltpu.CompilerParams(dimension_semantics=("parallel", "arbitrary"),
                                             vmem_limit_bytes=VMEM_LIMIT),
        name="conv_module_norm",
    )(x2d, *consts)


def _head_perm():
    p = np.arange(HEADS * HEAD_DIM)
    g, half, d = p // LANES, (p % LANES) // HEAD_DIM, p % HEAD_DIM
    return (g + GROUP * half) * HEAD_DIM + d


def _compress_weights(pe_k, w1_k, w2_k, pe_v, w1_v, w2_v):
    eye = jnp.eye(KV_HEADS, dtype=F32)
    w1 = jnp.stack([w1_k, w1_v]).reshape(2, CMP_BLOCK, HEAD_DIM, HEAD_DIM)
    w1x = jnp.einsum('klde,km,hn->lkhdmne', w1, eye, eye).reshape(CMP_BLOCK * 256, 256)
    half = CMP_STRIDE * 256
    pe = jnp.stack([pe_k, pe_v])
    pex = jnp.broadcast_to(pe.transpose(1, 0, 2)[:, :, None, :], (CMP_BLOCK, 2, KV_HEADS, HEAD_DIM))
    pex = pex.reshape(2, half)
    w2 = jnp.stack([w2_k, w2_v])
    w2x = jnp.einsum('kde,km,hn->khdmne', w2, eye, eye).reshape(256, 256)
    pea = jnp.broadcast_to(pex[0:1], (8, half)).astype(BF16)
    peb = jnp.broadcast_to(pex[1:2], (8, half)).astype(BF16)
    return w1x[:half].astype(BF16), w1x[half:].astype(BF16), pea, peb, w2x.astype(BF16)


def _overlap_matrix(nc):
    c0 = np.arange(nc)[None, :] * CMP_STRIDE
    s0 = np.arange(LANES)[:, None] * SEL_BLOCK
    ov = (c0 < s0 + SEL_BLOCK) & (c0 + CMP_BLOCK > s0) & (np.arange(nc)[None, :] < nc - 1)
    return jnp.asarray(ov, dtype=BF16)


def _even_layer(x2d, rel_bias, w_in, w_out, ln_g, ln_b, w_s, b_s, pe_k, w1_k, w2_k, pe_v, w1_v, w2_v,
                norm_g, norm_b, *, batch, seq):
    nc = seq // CMP_STRIDE
    assert seq % FAR_TILE == 0 and seq >= NEAR and seq // SEL_BLOCK <= LANES
    perm = _head_perm()
    qw = HEADS * HEAD_DIM
    o = 2 * A_WIDTH
    wuv = w_in[:, :o].astype(BF16)
    wq = w_in[:, o:o + qw][:, perm]
    wkv = w_in[:, o + qw:o + qw + 768]
    wgt = jnp.pad(w_in[:, o + qw + 768:], ((0, 0), (0, LANES - 3 * HEADS)))
    wqkv = jnp.concatenate([wq, wkv, wgt], axis=1).astype(BF16)
    bs = jnp.broadcast_to(b_s[:, :, None], (A_GROUPS, CHUNK, A_WIDTH // A_GROUPS))
    a_out, qs, cmp2, kaug, vaug, wink, winv, gates = _proj(
        x2d, wuv, wqkv, ln_g[None, :], ln_b[None, :], w_s, bs, seq=seq)

    kc, vct = _compress(cmp2.reshape(batch, nc, CMP_STRIDE * 256),
                    *_compress_weights(pe_k, w1_k, w2_k, pe_v, w1_v, w2_v))

    fc, dn = _bias_tables(_bucket_thresholds(seq), rel_bias.T.reshape(-1), nc)
    b_out = _nsa(qs, gates, kc, vct, kaug, vaug, wink, winv, fc, dn, _overlap_matrix(nc), batch=batch, seq=seq)

    wo_a = w_out[:A_WIDTH].astype(BF16)
    wo_b = w_out[A_WIDTH:][perm].astype(BF16)
    return _outproj(x2d, a_out, b_out, wo_a, wo_b, norm_g[None, :], norm_b[None, :])


def kernel(x, rel_bias, hyb_w_in, hyb_w_out, gmlp_ln_g, gmlp_ln_b, gmlp_w_s, gmlp_b_s, cmp_pe_k, cmp_w1_k, cmp_w2_k, cmp_pe_v, cmp_w1_v, cmp_w2_v, conv_w_in, conv_b_in, conv_dw_w, conv_dw_b, conv_ln_g, conv_ln_b, conv_w_out, conv_b_out, ffn_w_gate, ffn_w_up, ffn_w_down, norm_mix_g, norm_mix_b, norm_ffn_g, norm_ffn_b):
    batch, seq, d = x.shape
    h = x.reshape(batch * seq, d)
    for layer in range(DEPTH):
        i = layer // 2
        if layer % 2 == 0:
            h = _even_layer(h, rel_bias, hyb_w_in[i], hyb_w_out[i], gmlp_ln_g[i], gmlp_ln_b[i],
                            gmlp_w_s[i], gmlp_b_s[i], cmp_pe_k[i], cmp_w1_k[i], cmp_w2_k[i],
                            cmp_pe_v[i], cmp_w1_v[i], cmp_w2_v[i],
                            norm_mix_g[layer], norm_mix_b[layer], batch=batch, seq=seq)
        else:
            h = _conv(h, conv_w_in[i].astype(BF16), conv_b_in[i][None, :], conv_dw_w[i], conv_dw_b[i][None, :],
                      conv_ln_g[i][None, :], conv_ln_b[i][None, :], conv_w_out[i].astype(BF16),
                      conv_b_out[i][None, :], norm_mix_g[layer][None, :], norm_mix_b[layer][None, :],
                      batch=batch, seq=seq)
        h = _ffn(h, ffn_w_gate[layer].astype(BF16), ffn_w_up[layer].astype(BF16),
                 ffn_w_down[layer].astype(BF16), norm_ffn_g[layer][None, :], norm_ffn_b[layer][None, :])
    return h.reshape(batch, seq, d)
```

```python
import functools
import math

import numpy as np
import jax
import jax.numpy as jnp
from jax import lax
from jax.experimental import pallas as pl
from jax.experimental.pallas import tpu as pltpu

F32 = jnp.float32
BF16 = jnp.bfloat16

D_MODEL = 1024
DEPTH = 2
ALPHA = (2 * DEPTH) ** 0.25
CHUNK = 128
A_WIDTH = D_MODEL // 2
A_GROUPS = 4
HEADS = 8
KV_HEADS = 2
GROUP = HEADS // KV_HEADS
HEAD_DIM = (D_MODEL // 2) // HEADS
CMP_STRIDE = 16
CMP_BLOCK = 32
SEL_BLOCK = 64
N_SELECT = 16
WINDOW = 512
Q_BLOCK = 128
N_BUCKETS = 32
MAX_DISTANCE = 1024
CONV_WIDTH = 31
NEG_INF = -1e30
FORCE = 1e9
LN_EPS = 1e-5

LANES = 128
VMEM_LIMIT = 56 * 1024 * 1024

ROWS = GROUP * Q_BLOCK
FAR_TILE = 512
SLOTS = 3
NEAR_TILES = 8
NEAR = NEAR_TILES * Q_BLOCK
NEAR_BACK = NEAR - Q_BLOCK
WIN_KEYS = WINDOW + Q_BLOCK
DN_ZERO = FAR_TILE
DN_ROWS = DN_ZERO + NEAR + NEAR_BACK
ONES_ROWS = 16
V_ROWS = LANES + ONES_ROWS
LOG2E = math.log2(math.e)
HALO = 32


def _dot(a, b):
    return jnp.dot(a, b, preferred_element_type=F32)


def _dot_nt(a, b):
    return lax.dot_general(a, b, (((1,), (1,)), ((), ())), preferred_element_type=F32)


def _layer_norm(z, g, b):
    mu = jnp.mean(z, axis=-1, keepdims=True)
    d = z - mu
    var = jnp.mean(d * d, axis=-1, keepdims=True)
    return d * lax.rsqrt(var + LN_EPS) * g + b


def _t5_bucket(dist):
    n = jnp.maximum(dist, 0)
    max_exact = N_BUCKETS // 2
    nf = jnp.maximum(n, 1).astype(jnp.float32)
    large = max_exact + (jnp.log(nf / max_exact) / math.log(MAX_DISTANCE / max_exact)
                         * (N_BUCKETS - max_exact)).astype(jnp.int32)
    large = jnp.minimum(large, N_BUCKETS - 1)
    return jnp.where(n < max_exact, n, large)


def _bucket_thresholds(seq):
    b = _t5_bucket(jnp.arange(seq, dtype=jnp.int32))
    j = jnp.arange(N_BUCKETS, dtype=jnp.int32)
    return jnp.sum((b[None, :] < j[:, None]).astype(jnp.int32), axis=1).astype(jnp.int32)


def _tables_kernel(thr_ref, rb_ref, fc_ref, dn_ref, *, nc):
    h = pl.program_id(0)
    base = h * N_BUCKETS
    far = rb_ref[base + N_BUCKETS - 1]

    def bias_of(dist):
        val = jnp.full(dist.shape, far, F32)
        for j in range(N_BUCKETS - 1, 0, -1):
            val = jnp.where(dist < thr_ref[j], rb_ref[base + j - 1], val)
        return val * LOG2E

    row = lax.broadcasted_iota(jnp.int32, (Q_BLOCK, LANES), 0)
    ql = lax.broadcasted_iota(jnp.int32, (Q_BLOCK, LANES), 1)
    for c in range(2 * nc // Q_BLOCK):
        dist = ql - CMP_STRIDE * (row + (c * Q_BLOCK - (nc - 8))) - (CMP_BLOCK - 1)
        fc_ref[0, c * Q_BLOCK:(c + 1) * Q_BLOCK, :] = jnp.where(dist >= 0, bias_of(dist), NEG_INF)
    for c in range(DN_ROWS // Q_BLOCK):
        dist = ql + (NEAR_BACK + DN_ZERO - c * Q_BLOCK) - row
        dn_ref[0, c * Q_BLOCK:(c + 1) * Q_BLOCK, :] = jnp.where(dist >= 0, bias_of(dist) - far * LOG2E, NEG_INF)


def _bias_tables(thr, rb_flat, nc):
    return pl.pallas_call(
        functools.partial(_tables_kernel, nc=nc),
        out_shape=(jax.ShapeDtypeStruct((HEADS, 2 * nc, Q_BLOCK), F32),
                   jax.ShapeDtypeStruct((HEADS, DN_ROWS, Q_BLOCK), F32)),
        grid=(HEADS,),
        in_specs=[pl.BlockSpec(memory_space=pltpu.SMEM), pl.BlockSpec(memory_space=pltpu.SMEM)],
        out_specs=(pl.BlockSpec((1, 2 * nc, Q_BLOCK), lambda h: (h, 0, 0)),
                   pl.BlockSpec((1, DN_ROWS, Q_BLOCK), lambda h: (h, 0, 0))),
        name="bias_tables",
    )(thr, rb_flat)


def _proj_kernel(x_ref, wuv_ref, wqkv_ref, lng_ref, lnb_ref, ws_ref, bs_ref,
                 a_ref, q_ref, cmp_ref, kaug_ref, vt_ref, wink_ref, wvt_ref, gate_ref, *, tm, seq):
    xb = x_ref[...].astype(BF16)
    uv = jax.nn.gelu(_dot(xb, wuv_ref[...]))
    u = uv[:, :A_WIDTH]
    v = _layer_norm(uv[:, A_WIDTH:], lng_ref[...], lnb_ref[...]).astype(BF16)
    row = lax.broadcasted_iota(jnp.int32, (CHUNK, CHUNK), 0)
    col = lax.broadcasted_iota(jnp.int32, (CHUNK, CHUNK), 1)
    gd = A_WIDTH // A_GROUPS
    for g in range(A_GROUPS):
        w = jnp.where(col <= row, ws_ref[g], 0.0).astype(BF16)
        for c in range(tm // CHUNK):
            rs = slice(c * CHUNK, (c + 1) * CHUNK)
            cs = slice(g * gd, (g + 1) * gd)
            s = _dot(w, v[rs, cs]) + bs_ref[g]
            a_ref[rs, cs] = (u[rs, cs] * s).astype(BF16)

    h = _dot(xb, wqkv_ref[...])
    q_ref[...] = (h[:, 0:512] * (HEAD_DIM ** -0.5 * LOG2E)).astype(BF16)
    cmp_ref[...] = h[:, 512:768].astype(BF16)
    kaug_ref[:, 0:LANES] = h[:, 768:896].astype(BF16)
    pos = (pl.program_id(0) * tm) % seq + lax.broadcasted_iota(jnp.int32, (tm, LANES), 0)
    blk = lax.broadcasted_iota(jnp.int32, (tm, LANES), 1)
    kaug_ref[:, LANES:2 * LANES] = jnp.where(pos // SEL_BLOCK == blk, 1.0, 0.0).astype(BF16)
    ones = jnp.ones((ONES_ROWS, tm), BF16)
    vt_ref[0:LANES, :] = h[:, 896:1024].T.astype(BF16)
    vt_ref[LANES:V_ROWS, :] = ones
    wink_ref[...] = h[:, 1024:1152].astype(BF16)
    wvt_ref[0:LANES, :] = h[:, 1152:1280].T.astype(BF16)
    wvt_ref[LANES:V_ROWS, :] = ones
    gate_ref[...] = jax.nn.sigmoid(h[:, 1280:1408])


def _proj(x2d, wuv, wqkv, lng, lnb, ws, bs, *, seq, tm=512):
    t = x2d.shape[0]
    row = lambda i: (i, 0)
    const2 = lambda i: (0, 0)
    const3 = lambda i: (0, 0, 0)
    col = lambda i: (0, i)
    outs = [(A_WIDTH, BF16, False), (512, BF16, False), (256, BF16, False), (256, BF16, False),
            (V_ROWS, BF16, True), (LANES, BF16, False), (V_ROWS, BF16, True), (LANES, F32, False)]
    return pl.pallas_call(
        functools.partial(_proj_kernel, tm=tm, seq=seq),
        out_shape=tuple(jax.ShapeDtypeStruct((w, t) if tr else (t, w), dt) for w, dt, tr in outs),
        grid=(t // tm,),
        in_specs=[pl.BlockSpec((tm, D_MODEL), row),
                  pl.BlockSpec(wuv.shape, const2),
                  pl.BlockSpec(wqkv.shape, const2),
                  pl.BlockSpec(lng.shape, const2),
                  pl.BlockSpec(lnb.shape, const2),
                  pl.BlockSpec(ws.shape, const3),
                  pl.BlockSpec(bs.shape, const3)],
        out_specs=tuple(pl.BlockSpec((w, tm), col) if tr else pl.BlockSpec((tm, w), row) for w, _, tr in outs),
        compiler_params=pltpu.CompilerParams(dimension_semantics=("parallel",), vmem_limit_bytes=VMEM_LIMIT),
        name="in_proj_gmlp",
    )(x2d, wuv, wqkv, lng, lnb, ws, bs)


def _compress_kernel(r_ref, wa_ref, wb_ref, pea_ref, peb_ref, w2_ref, kc_ref, vct_ref):
    r = r_ref[0]
    top = _dot(r, wa_ref[...])
    bot = _dot(r, wb_ref[...])
    pe = _dot(pea_ref[...], wa_ref[...]) + _dot(peb_ref[...], wb_ref[...])
    hid = top + pltpu.roll(bot, bot.shape[0] - 1, 0) + pe[0:1]
    out = _dot(jax.nn.gelu(hid).astype(BF16), w2_ref[...])
    kc_ref[0] = out[:, 0:LANES].astype(BF16)
    vct_ref[0] = out[:, LANES:2 * LANES].T.astype(BF16)


def _compress(cmp3, wa, wb, pea, peb, w2):
    b, nc, width = cmp3.shape
    const2 = lambda i: (0, 0)
    return pl.pallas_call(
        _compress_kernel,
        out_shape=(jax.ShapeDtypeStruct((b, nc, LANES), BF16), jax.ShapeDtypeStruct((b, LANES, nc), BF16)),
        grid=(b,),
        in_specs=[pl.BlockSpec((1, nc, width), lambda i: (i, 0, 0)),
                  pl.BlockSpec(wa.shape, const2), pl.BlockSpec(wb.shape, const2),
                  pl.BlockSpec(pea.shape, const2), pl.BlockSpec(peb.shape, const2),
                  pl.BlockSpec(w2.shape, const2)],
        out_specs=(pl.BlockSpec((1, nc, LANES), lambda i: (i, 0, 0)),
                   pl.BlockSpec((1, LANES, nc), lambda i: (i, 0, 0))),
        compiler_params=pltpu.CompilerParams(dimension_semantics=("parallel",), vmem_limit_bytes=VMEM_LIMIT),
        name="kv_compress",
    )(cmp3, wa, wb, pea, peb, w2)


def _nsa_kernel(q_ref, gate_ref, kc_ref, vct_ref, kaug_ref, vt_ref, wink_ref, wvt_ref, fc_ref, dn_ref, ovt_ref,
                o_ref, *scratch, nc, n_tiles):
    ib = pl.program_id(1)
    t0 = ib * Q_BLOCK
    lane = lax.broadcasted_iota(jnp.int32, (Q_BLOCK, LANES), 1)
    srow = lax.broadcasted_iota(jnp.int32, (Q_BLOCK, LANES), 0)
    q = q_ref[...]
    kc = kc_ref[0]
    vct = vct_ref[0]
    ovt = ovt_ref[...]

    def per_head_tiles(ref, hkv, start, size):
        return jnp.concatenate([ref[hkv * GROUP + g, pl.ds(start, size), :] for g in range(GROUP)], axis=1)

    def max_keys(s):
        parts = [s[i * SEL_BLOCK:(i + 1) * SEL_BLOCK] for i in range(s.shape[0] // SEL_BLOCK)]
        while len(parts) > 1:
            parts = [jnp.maximum(a, b) for a, b in zip(parts[0::2], parts[1::2])] + parts[len(parts) & ~1:]
        return jnp.max(parts[0], axis=0, keepdims=True)

    fc_row = pl.multiple_of((nc - 8) - 8 * ib, 8)
    qcol = lax.broadcasted_iota(jnp.int32, (1, ROWS), 1) % Q_BLOCK
    has_cmp = jnp.where(t0 + qcol >= CMP_BLOCK - 1, 1.0, 0.0)
    qhs, o_cs, imps = [], [], []
    for hkv in range(KV_HEADS):
        mine = (lane >= HEAD_DIM) if hkv else (lane < HEAD_DIM)
        qh = jnp.concatenate([jnp.where(mine, q[:, g * LANES:(g + 1) * LANES], 0.0).astype(BF16)
                              for g in range(GROUP)], axis=0)
        lc = _dot_nt(kc, qh) + per_head_tiles(fc_ref, hkv, fc_row, nc)
        e = jnp.exp2(lc - max_keys(lc))
        pc = e * (has_cmp / jnp.sum(e, axis=0, keepdims=True))
        o_cs.append(_dot(vct, pc.astype(BF16)))
        psum = pc[:, 0:LANES] + pc[:, LANES:2 * LANES] + pc[:, 2 * LANES:3 * LANES] + pc[:, 3 * LANES:]
        p_hi = psum.astype(BF16)
        p_lo = (psum - p_hi.astype(F32)).astype(BF16)
        imps.append(_dot(ovt, p_hi) + _dot(ovt, p_lo))
        qhs.append(qh)

    blk = lax.broadcasted_iota(jnp.int32, (LANES, KV_HEADS * Q_BLOCK), 0)
    col = lax.broadcasted_iota(jnp.int32, (LANES, KV_HEADS * Q_BLOCK), 1)
    jq = 2 * ib + jnp.where(col % Q_BLOCK >= SEL_BLOCK, 1, 0)
    forced = (blk == 0) | (blk == jq) | (blk == jq - 1)
    cand = jnp.where(forced, -3e38, jnp.where(blk > jq, NEG_INF, jnp.concatenate(imps, axis=1)))
    blk_f = blk.astype(F32)

    def pick_one(_, carry):
        cur, chosen = carry
        mx = jnp.max(cur, axis=0, keepdims=True)
        first = jnp.min(jnp.where(cur == mx, blk_f, float(LANES)), axis=0, keepdims=True)
        pick = blk_f == first
        return jnp.where(pick, -3e38, cur), jnp.where(pick, 1.0, chosen)

    _, sel_t = lax.fori_loop(0, N_SELECT - 3, pick_one, (cand, jnp.where(forced, 1.0, 0.0)))

    n_trips = (ib // 4 + SLOTS) // SLOTS
    n_plain = jnp.maximum((t0 - NEAR_BACK - DN_ZERO) // FAR_TILE + 1, 0) // SLOTS
    wstart = pl.multiple_of(jnp.maximum(t0 - WINDOW, 0), LANES)
    wrow = pl.multiple_of(NEAR_BACK + DN_ZERO + wstart - t0, LANES)
    win_edge = jnp.where((srow > lane) | (t0 < WINDOW), 0.0, NEG_INF)
    win_edge = jnp.concatenate([win_edge] * GROUP, axis=1)

    o_ss, o_ws = [], []
    per_kv = len(scratch) // KV_HEADS
    for hkv in range(KV_HEADS):
        qa_sc, acc_sc = scratch[per_kv * hkv:per_kv * hkv + 2]
        s_buf = scratch[per_kv * hkv + 2:per_kv * hkv + 2 + SLOTS]
        p_buf = scratch[per_kv * hkv + 2 + SLOTS:per_kv * (hkv + 1)]
        qh = qhs[hkv]

        sel = sel_t[:, hkv * Q_BLOCK:(hkv + 1) * Q_BLOCK].T
        mask = jnp.where(sel > 0, 0.0, NEG_INF).astype(BF16)
        qa_sc[:, 0:LANES] = qh
        qa_sc[:, LANES:2 * LANES] = jnp.concatenate([mask] * GROUP, axis=0)
        acc_sc[...] = jnp.zeros(acc_sc.shape, F32)
        for slot in range(1, SLOTS):
            p_buf[slot][...] = jnp.zeros((FAR_TILE, ROWS), BF16)

        def logits(tile, slot):
            ks = pl.multiple_of(jnp.minimum(tile, n_tiles - 1) * FAR_TILE, FAR_TILE)
            s_buf[slot][...] = _dot_nt(kaug_ref[pl.ds(ks, FAR_TILE), :], qa_sc[...])

        def values(tile, slot):
            ks = pl.multiple_of(jnp.clip(tile, 0, n_tiles - 1) * FAR_TILE, FAR_TILE)
            return _dot(vt_ref[:, pl.ds(ks, FAR_TILE)], p_buf[slot][...])

        def step(cur, slot, carry, biased):
            m_old, a_prev = carry
            logits(cur + 2, (slot + 2) % SLOTS)
            pending = values(cur - 2, (slot + 1) % SLOTS)
            drow = pl.multiple_of(jnp.clip(NEAR_BACK + DN_ZERO + cur * FAR_TILE - t0, 0, DN_ROWS - FAR_TILE), LANES)
            m_parts, a_parts = [], []
            for g in range(GROUP):
                cs = slice(g * Q_BLOCK, (g + 1) * Q_BLOCK)

                def strip():
                    s = s_buf[slot][:, cs]
                    return s + dn_ref[hkv * GROUP + g, pl.ds(drow, FAR_TILE), :] if biased else s

                m_new = jnp.maximum(m_old[:, cs], max_keys(strip()))
                p_buf[slot][:, cs] = jnp.exp2(strip() - m_new).astype(BF16)
                m_parts.append(m_new)
                a_parts.append(jnp.exp2(m_old[:, cs] - m_new))
            alpha = jnp.concatenate(a_parts, axis=1)
            acc_sc[...] = (acc_sc[...] + pending * a_prev) * alpha
            return jnp.concatenate(m_parts, axis=1), alpha

        logits(0, 0)
        logits(1, 1)

        sw = _dot_nt(wink_ref[pl.ds(wstart, WIN_KEYS), :], qh)
        sw = sw + per_head_tiles(dn_ref, hkv, wrow, WIN_KEYS)
        sw = jnp.concatenate([sw[:Q_BLOCK] + win_edge, sw[Q_BLOCK:]], axis=0)
        pw = jnp.exp2(sw - max_keys(sw)).astype(BF16)
        ow = _dot(wvt_ref[:, pl.ds(wstart, WIN_KEYS)], pw)
        o_ws.append(ow[0:LANES] / ow[LANES:LANES + 1])

        def trip(biased, i, carry):
            for k in range(SLOTS):
                carry = step(SLOTS * i + k, k, carry, biased)
            return carry

        carry = (jnp.full((1, ROWS), -jnp.inf, F32), jnp.ones((1, ROWS), F32))
        carry = lax.fori_loop(0, n_plain, functools.partial(trip, False), carry)
        _, a_last = lax.fori_loop(n_plain, n_trips, functools.partial(trip, True), carry)
        last = SLOTS * n_trips - 1
        acc = acc_sc[...] + values(last - 1, SLOTS - 2) * a_last + values(last, SLOTS - 1)
        o_ss.append(acc[0:LANES] / acc[LANES:LANES + 1])

    gates_t = gate_ref[...].T
    for g in range(GROUP):
        cs = slice(g * Q_BLOCK, (g + 1) * Q_BLOCK)
        per_kv = []
        for hkv in range(KV_HEADS):
            c = 3 * (hkv * GROUP + g)
            per_kv.append(gates_t[c:c + 1] * o_cs[hkv][:, cs] + gates_t[c + 1:c + 2] * o_ss[hkv][:, cs]
                          + gates_t[c + 2:c + 3] * o_ws[hkv][:, cs])
        merged = jnp.where(srow < HEAD_DIM, per_kv[0], per_kv[1])
        o_ref[:, g * LANES:(g + 1) * LANES] = merged.T.astype(BF16)


def _nsa(qs, gates, kc, vct, kaug, vt, wink, wvt, fc, dn, ovt, *, batch, seq):
    nc = seq // CMP_STRIDE
    nq = seq // Q_BLOCK
    qrow = lambda b, i: (b * nq + i, 0)
    per_batch = lambda b, i: (b, 0)
    per_batch_t = lambda b, i: (0, b)
    const2 = lambda b, i: (0, 0)
    const3 = lambda b, i: (0, 0, 0)
    once = pl.Buffered(1)
    return pl.pallas_call(
        functools.partial(_nsa_kernel, nc=nc, n_tiles=seq // FAR_TILE),
        out_shape=jax.ShapeDtypeStruct((batch * seq, 4 * LANES), BF16),
        grid=(batch, nq),
        in_specs=[pl.BlockSpec((Q_BLOCK, 4 * LANES), qrow),
                  pl.BlockSpec((Q_BLOCK, LANES), qrow),
                  pl.BlockSpec((1, nc, LANES), lambda b, i: (b, 0, 0)),
                  pl.BlockSpec((1, LANES, nc), lambda b, i: (b, 0, 0)),
                  pl.BlockSpec((seq, 256), per_batch, pipeline_mode=once),
                  pl.BlockSpec((V_ROWS, seq), per_batch_t, pipeline_mode=once),
                  pl.BlockSpec((seq, LANES), per_batch, pipeline_mode=once),
                  pl.BlockSpec((V_ROWS, seq), per_batch_t, pipeline_mode=once),
                  pl.BlockSpec(fc.shape, const3, pipeline_mode=once),
                  pl.BlockSpec(dn.shape, const3, pipeline_mode=once),
                  pl.BlockSpec(ovt.shape, const2)],
        out_specs=pl.BlockSpec((Q_BLOCK, 4 * LANES), qrow),
        scratch_shapes=([pltpu.VMEM((ROWS, 2 * LANES), BF16),
                         pltpu.VMEM((V_ROWS, ROWS), F32)]
                        + [pltpu.VMEM((FAR_TILE, ROWS), F32)] * SLOTS
                        + [pltpu.VMEM((FAR_TILE, ROWS), BF16)] * SLOTS
                        ) * KV_HEADS,
        compiler_params=pltpu.CompilerParams(dimension_semantics=("parallel", "arbitrary"),
                                             vmem_limit_bytes=VMEM_LIMIT),
        name="sparse_attention",
    )(qs, gates, kc, vct, kaug, vt, wink, wvt, fc, dn, ovt)


def _outproj_kernel(x_ref, a_ref, b_ref, wa_ref, wb_ref, g_ref, beta_ref, o_ref):
    y = _dot(a_ref[...], wa_ref[...]) + _dot(b_ref[...], wb_ref[...])
    o_ref[...] = _layer_norm(ALPHA * x_ref[...] + y, g_ref[...], beta_ref[...])


def _outproj(x2d, a, b, wa, wb, g, beta, *, tm=512):
    t = x2d.shape[0]
    row = lambda i: (i, 0)
    const2 = lambda i: (0, 0)
    return pl.pallas_call(
        _outproj_kernel,
        out_shape=jax.ShapeDtypeStruct((t, D_MODEL), F32),
        grid=(t // tm,),
        in_specs=[pl.BlockSpec((tm, D_MODEL), row), pl.BlockSpec((tm, A_WIDTH), row), pl.BlockSpec((tm, 512), row),
                  pl.BlockSpec(wa.shape, const2), pl.BlockSpec(wb.shape, const2),
                  pl.BlockSpec(g.shape, const2), pl.BlockSpec(beta.shape, const2)],
        out_specs=pl.BlockSpec((tm, D_MODEL), row),
        compiler_params=pltpu.CompilerParams(dimension_semantics=("parallel",), vmem_limit_bytes=VMEM_LIMIT),
        name="out_proj_norm",
    )(x2d, a, b, wa, wb, g, beta)


def _ffn_kernel(x_ref, wg_ref, wu_ref, wd_ref, g_ref, beta_ref, o_ref, acc_ref, *, hc):
    x = x_ref[...]
    xb = x.astype(BF16)
    hidden = wg_ref.shape[1]
    for c in range(hidden // hc):
        cs = slice(c * hc, (c + 1) * hc)
        gate = _dot(xb, wg_ref[:, cs])
        up = _dot(xb, wu_ref[:, cs])
        part = _dot((jax.nn.silu(gate) * up).astype(BF16), wd_ref[cs, :])
        if c == 0:
            acc_ref[...] = part
        else:
            acc_ref[...] += part
    o_ref[...] = _layer_norm(ALPHA * x + acc_ref[...], g_ref[...], beta_ref[...])


def _ffn(x2d, wg, wu, wd, g, beta, *, tm=512, hc=256):
    t = x2d.shape[0]
    row = lambda i: (i, 0)
    const2 = lambda i: (0, 0)
    once = pl.Buffered(1)
    return pl.pallas_call(
        functools.partial(_ffn_kernel, hc=hc),
        out_shape=jax.ShapeDtypeStruct((t, D_MODEL), F32),
        grid=(t // tm,),
        in_specs=[pl.BlockSpec((tm, D_MODEL), row),
                  pl.BlockSpec(wg.shape, const2, pipeline_mode=once),
                  pl.BlockSpec(wu.shape, const2, pipeline_mode=once),
                  pl.BlockSpec(wd.shape, const2, pipeline_mode=once),
                  pl.BlockSpec(g.shape, const2), pl.BlockSpec(beta.shape, const2)],
        out_specs=pl.BlockSpec((tm, D_MODEL), row),
        scratch_shapes=[pltpu.VMEM((tm, D_MODEL), F32)],
        compiler_params=pltpu.CompilerParams(dimension_semantics=("parallel",), vmem_limit_bytes=VMEM_LIMIT),
        name="swiglu_ffn_norm",
    )(x2d, wg, wu, wd, g, beta)


def _conv_kernel(x_ref, win_ref, bin_ref, dww_ref, dwb_ref, lng_ref, lnb_ref, wout_ref, bout_ref,
                 g_ref, beta_ref, o_ref, buf_ref, cv_ref, *, tm, rc):
    @pl.when(pl.program_id(1) == 0)
    def _():
        buf_ref[0:HALO, :] = jnp.zeros((HALO, D_MODEL), F32)

    x = x_ref[...]
    h = _dot(x.astype(BF16), win_ref[...]) + bin_ref[...]
    buf_ref[HALO:HALO + tm, :] = h[:, :D_MODEL] * jax.nn.sigmoid(h[:, D_MODEL:])

    lead = HALO - (CONV_WIDTH - 1)

    def conv_rows(r, carry):
        r0 = pl.multiple_of(r * rc, rc)
        for lc in range(D_MODEL // LANES):
            ls = slice(lc * LANES, (lc + 1) * LANES)
            acc = jnp.broadcast_to(dwb_ref[:, ls], (rc, LANES))
            za = buf_ref[pl.ds(r0, rc + HALO), ls]
            for b in range(8):
                taps = range(b, CONV_WIDTH, 8)
                zb = za[lead + b:lead + b + rc + 8 * (len(taps) - 1)]
                for a, j in enumerate(taps):
                    acc = acc + zb[8 * a:8 * a + rc] * dww_ref[j:j + 1, ls]
            cv_ref[pl.ds(r0, rc), ls] = acc
        return carry

    lax.fori_loop(0, tm // rc, conv_rows, 0)
    buf_ref[0:HALO, :] = buf_ref[tm:tm + HALO, :]

    y = jax.nn.silu(_layer_norm(cv_ref[...], lng_ref[...], lnb_ref[...]))
    y = _dot(y.astype(BF16), wout_ref[...]) + bout_ref[...]
    o_ref[...] = _layer_norm(ALPHA * x + y, g_ref[...], beta_ref[...])


def _conv(x2d, w_in, b_in, dw_w, dw_b, ln_g, ln_b, w_out, b_out, g, beta, *, batch, seq, tm=512, rc=64):
    nt = seq // tm
    row = lambda b, i: (b * nt + i, 0)
    const2 = lambda b, i: (0, 0)
    consts = (w_in, b_in, dw_w, dw_b, ln_g, ln_b, w_out, b_out, g, beta)
    return pl.pallas_call(
        functools.partial(_conv_kernel, tm=tm, rc=rc),
        out_shape=jax.ShapeDtypeStruct((batch * seq, D_MODEL), F32),
        grid=(batch, nt),
        in_specs=[pl.BlockSpec((tm, D_MODEL), row)] + [pl.BlockSpec(c.shape, const2) for c in consts],
        out_specs=pl.BlockSpec((tm, D_MODEL), row),
        scratch_shapes=[pltpu.VMEM((HALO + tm, D_MODEL), F32), pltpu.VMEM((tm, D_MODEL), F32)],
        compiler_params=pltpu.CompilerParams(dimension_semantics=("parallel", "arbitrary"),
                                             vmem_limit_bytes=VMEM_LIMIT),
        name="conv_module_norm",
    )(x2d, *consts)


def _head_perm():
    p = np.arange(HEADS * HEAD_DIM)
    g, half, d = p // LANES, (p % LANES) // HEAD_DIM, p % HEAD_DIM
    return (g + GROUP * half) * HEAD_DIM + d


def _compress_weights(pe_k, w1_k, w2_k, pe_v, w1_v, w2_v):
    eye = jnp.eye(KV_HEADS, dtype=F32)
    w1 = jnp.stack([w1_k, w1_v]).reshape(2, CMP_BLOCK, HEAD_DIM, HEAD_DIM)
    w1x = jnp.einsum('klde,km,hn->lkhdmne', w1, eye, eye).reshape(CMP_BLOCK * 256, 256)
    half = CMP_STRIDE * 256
    pe = jnp.stack([pe_k, pe_v])
    pex = jnp.broadcast_to(pe.transpose(1, 0, 2)[:, :, None, :], (CMP_BLOCK, 2, KV_HEADS, HEAD_DIM))
    pex = pex.reshape(2, half)
    w2 = jnp.stack([w2_k, w2_v])
    w2x = jnp.einsum('kde,km,hn->khdmne', w2, eye, eye).reshape(256, 256)
    pea = jnp.broadcast_to(pex[0:1], (8, half)).astype(BF16)
    peb = jnp.broadcast_to(pex[1:2], (8, half)).astype(BF16)
    return w1x[:half].astype(BF16), w1x[half:].astype(BF16), pea, peb, w2x.astype(BF16)


def _overlap_matrix(nc):
    c0 = np.arange(nc)[None, :] * CMP_STRIDE
    s0 = np.arange(LANES)[:, None] * SEL_BLOCK
    ov = (c0 < s0 + SEL_BLOCK) & (c0 + CMP_BLOCK > s0) & (np.arange(nc)[None, :] < nc - 1)
    return jnp.asarray(ov, dtype=BF16)


def _even_layer(x2d, rel_bias, w_in, w_out, ln_g, ln_b, w_s, b_s, pe_k, w1_k, w2_k, pe_v, w1_v, w2_v,
                norm_g, norm_b, *, batch, seq):
    nc = seq // CMP_STRIDE
    assert seq % FAR_TILE == 0 and seq >= NEAR and seq // SEL_BLOCK <= LANES
    perm = _head_perm()
    qw = HEADS * HEAD_DIM
    o = 2 * A_WIDTH
    wuv = w_in[:, :o].astype(BF16)
    wq = w_in[:, o:o + qw][:, perm]
    wkv = w_in[:, o + qw:o + qw + 768]
    wgt = jnp.pad(w_in[:, o + qw + 768:], ((0, 0), (0, LANES - 3 * HEADS)))
    wqkv = jnp.concatenate([wq, wkv, wgt], axis=1).astype(BF16)
    bs = jnp.broadcast_to(b_s[:, :, None], (A_GROUPS, CHUNK, A_WIDTH // A_GROUPS))
    a_out, qs, cmp2, kaug, vt, wink, wvt, gates = _proj(
        x2d, wuv, wqkv, ln_g[None, :], ln_b[None, :], w_s, bs, seq=seq)

    kc, vct = _compress(cmp2.reshape(batch, nc, CMP_STRIDE * 256),
                    *_compress_weights(pe_k, w1_k, w2_k, pe_v, w1_v, w2_v))

    fc, dn = _bias_tables(_bucket_thresholds(seq), rel_bias.T.reshape(-1), nc)
    b_out = _nsa(qs, gates, kc, vct, kaug, vt, wink, wvt, fc, dn, _overlap_matrix(nc), batch=batch, seq=seq)

    wo_a = w_out[:A_WIDTH].astype(BF16)
    wo_b = w_out[A_WIDTH:][perm].astype(BF16)
    return _outproj(x2d, a_out, b_out, wo_a, wo_b, norm_g[None, :], norm_b[None, :])


def kernel(x, rel_bias, hyb_w_in, hyb_w_out, gmlp_ln_g, gmlp_ln_b, gmlp_w_s, gmlp_b_s, cmp_pe_k, cmp_w1_k, cmp_w2_k, cmp_pe_v, cmp_w1_v, cmp_w2_v, conv_w_in, conv_b_in, conv_dw_w, conv_dw_b, conv_ln_g, conv_ln_b, conv_w_out, conv_b_out, ffn_w_gate, ffn_w_up, ffn_w_down, norm_mix_g, norm_mix_b, norm_ffn_g, norm_ffn_b):
    batch, seq, d = x.shape
    h = x.reshape(batch * seq, d)
    for layer in range(DEPTH):
        i = layer // 2
        if layer % 2 == 0:
            h = _even_layer(h, rel_bias, hyb_w_in[i], hyb_w_out[i], gmlp_ln_g[i], gmlp_ln_b[i],
                            gmlp_w_s[i], gmlp_b_s[i], cmp_pe_k[i], cmp_w1_k[i], cmp_w2_k[i],
                            cmp_pe_v[i], cmp_w1_v[i], cmp_w2_v[i],
                            norm_mix_g[layer], norm_mix_b[layer], batch=batch, seq=seq)
        else:
            h = _conv(h, conv_w_in[i].astype(BF16), conv_b_in[i][None, :], conv_dw_w[i], conv_dw_b[i][None, :],
                      conv_ln_g[i][None, :], conv_ln_b[i][None, :], conv_w_out[i].astype(BF16),
                      conv_b_out[i][None, :], norm_mix_g[layer][None, :], norm_mix_b[layer][None, :],
                      batch=batch, seq=seq)
        h = _ffn(h, ffn_w_gate[layer].astype(BF16), ffn_w_up[layer].astype(BF16),
                 ffn_w_down[layer].astype(BF16), norm_ffn_g[layer][None, :], norm_ffn_b[layer][None, :])
    return h.reshape(batch, seq, d)
```

```python
import functools
import math

import numpy as np
import jax
import jax.numpy as jnp
from jax import lax
from jax.experimental import pallas as pl
from jax.experimental.pallas import tpu as pltpu

F32 = jnp.float32
BF16 = jnp.bfloat16

D_MODEL = 1024
DEPTH = 2
ALPHA = (2 * DEPTH) ** 0.25
CHUNK = 128
A_WIDTH = D_MODEL // 2
A_GROUPS = 4
HEADS = 8
KV_HEADS = 2
GROUP = HEADS // KV_HEADS
HEAD_DIM = (D_MODEL // 2) // HEADS
CMP_STRIDE = 16
CMP_BLOCK = 32
SEL_BLOCK = 64
N_SELECT = 16
WINDOW = 512
Q_BLOCK = 128
N_BUCKETS = 32
MAX_DISTANCE = 1024
CONV_WIDTH = 31
NEG_INF = -1e30
FORCE = 1e9
LN_EPS = 1e-5

LANES = 128
VMEM_LIMIT = 56 * 1024 * 1024

ROWS = GROUP * Q_BLOCK
FAR_TILE = 512
NEAR_TILES = 8
NEAR = NEAR_TILES * Q_BLOCK
NEAR_BACK = NEAR - Q_BLOCK
WIN_KEYS = WINDOW + Q_BLOCK
DN_WIDTH = NEAR + NEAR_BACK
HALO = 32


def _dot(a, b):
    return jnp.dot(a, b, preferred_element_type=F32)


def _dot_nt(a, b):
    return lax.dot_general(a, b, (((1,), (1,)), ((), ())), preferred_element_type=F32)


def _layer_norm(z, g, b):
    mu = jnp.mean(z, axis=-1, keepdims=True)
    d = z - mu
    var = jnp.mean(d * d, axis=-1, keepdims=True)
    return d * lax.rsqrt(var + LN_EPS) * g + b


def _t5_bucket(dist):
    n = jnp.maximum(dist, 0)
    max_exact = N_BUCKETS // 2
    nf = jnp.maximum(n, 1).astype(jnp.float32)
    large = max_exact + (jnp.log(nf / max_exact) / math.log(MAX_DISTANCE / max_exact)
                         * (N_BUCKETS - max_exact)).astype(jnp.int32)
    large = jnp.minimum(large, N_BUCKETS - 1)
    return jnp.where(n < max_exact, n, large)


def _bucket_thresholds(seq):
    b = _t5_bucket(jnp.arange(seq, dtype=jnp.int32))
    j = jnp.arange(N_BUCKETS, dtype=jnp.int32)
    return jnp.sum((b[None, :] < j[:, None]).astype(jnp.int32), axis=1).astype(jnp.int32)


def _tables_kernel(thr_ref, rb_ref, fc_ref, dn_ref, *, nc):
    h = pl.program_id(0)
    base = h * N_BUCKETS
    far = rb_ref[base + N_BUCKETS - 1]

    def bias_of(dist):
        val = jnp.full(dist.shape, far, F32)
        for j in range(N_BUCKETS - 1, 0, -1):
            val = jnp.where(dist < thr_ref[j], rb_ref[base + j - 1], val)
        return val

    ql = lax.broadcasted_iota(jnp.int32, (Q_BLOCK, LANES), 0)
    ln = lax.broadcasted_iota(jnp.int32, (Q_BLOCK, LANES), 1)
    for c in range(2 * nc // Q_BLOCK):
        dist = ln - CMP_STRIDE * (ql + (c * Q_BLOCK - (nc - 8))) - (CMP_BLOCK - 1)
        fc_ref[0, c * Q_BLOCK:(c + 1) * Q_BLOCK, :] = jnp.where(dist >= 0, bias_of(dist), NEG_INF)
    for c in range(DN_WIDTH // LANES):
        dist = ql + (NEAR_BACK - c * LANES) - ln
        dn_ref[0, :, c * LANES:(c + 1) * LANES] = jnp.where(dist >= 0, bias_of(dist) - far, NEG_INF)


def _bias_tables(thr, rb_flat, nc):
    return pl.pallas_call(
        functools.partial(_tables_kernel, nc=nc),
        out_shape=(jax.ShapeDtypeStruct((HEADS, 2 * nc, Q_BLOCK), F32),
                   jax.ShapeDtypeStruct((HEADS, Q_BLOCK, DN_WIDTH), F32)),
        grid=(HEADS,),
        in_specs=[pl.BlockSpec(memory_space=pltpu.SMEM), pl.BlockSpec(memory_space=pltpu.SMEM)],
        out_specs=(pl.BlockSpec((1, 2 * nc, Q_BLOCK), lambda h: (h, 0, 0)),
                   pl.BlockSpec((1, Q_BLOCK, DN_WIDTH), lambda h: (h, 0, 0))),
        name="bias_tables",
    )(thr, rb_flat)


def _proj_kernel(x_ref, wuv_ref, wqkv_ref, lng_ref, lnb_ref, ws_ref, bs_ref,
                 a_ref, q_ref, cmp_ref, kaug_ref, vaug_ref, wink_ref, winv_ref, gate_ref, *, tm, seq):
    xb = x_ref[...].astype(BF16)
    uv = jax.nn.gelu(_dot(xb, wuv_ref[...]))
    u = uv[:, :A_WIDTH]
    v = _layer_norm(uv[:, A_WIDTH:], lng_ref[...], lnb_ref[...]).astype(BF16)
    row = lax.broadcasted_iota(jnp.int32, (CHUNK, CHUNK), 0)
    col = lax.broadcasted_iota(jnp.int32, (CHUNK, CHUNK), 1)
    gd = A_WIDTH // A_GROUPS
    for g in range(A_GROUPS):
        w = jnp.where(col <= row, ws_ref[g], 0.0).astype(BF16)
        for c in range(tm // CHUNK):
            rs = slice(c * CHUNK, (c + 1) * CHUNK)
            cs = slice(g * gd, (g + 1) * gd)
            s = _dot(w, v[rs, cs]) + bs_ref[g]
            a_ref[rs, cs] = (u[rs, cs] * s).astype(BF16)

    h = _dot(xb, wqkv_ref[...])
    q_ref[...] = (h[:, 0:512] * (HEAD_DIM ** -0.5)).astype(BF16)
    cmp_ref[...] = h[:, 512:768].astype(BF16)
    kaug_ref[:, 0:LANES] = h[:, 768:896].astype(BF16)
    pos = (pl.program_id(0) * tm) % seq + lax.broadcasted_iota(jnp.int32, (tm, LANES), 0)
    blk = lax.broadcasted_iota(jnp.int32, (tm, LANES), 1)
    kaug_ref[:, LANES:2 * LANES] = jnp.where(pos // SEL_BLOCK == blk, 1.0, 0.0).astype(BF16)
    ones = jnp.ones((tm, LANES), BF16)
    vaug_ref[:, 0:LANES] = h[:, 896:1024].astype(BF16)
    vaug_ref[:, LANES:2 * LANES] = ones
    wink_ref[...] = h[:, 1024:1152].astype(BF16)
    winv_ref[:, 0:LANES] = h[:, 1152:1280].astype(BF16)
    winv_ref[:, LANES:2 * LANES] = ones
    gate_ref[...] = jax.nn.sigmoid(h[:, 1280:1408])


def _proj(x2d, wuv, wqkv, lng, lnb, ws, bs, *, seq, tm=512):
    t = x2d.shape[0]
    row = lambda i: (i, 0)
    const2 = lambda i: (0, 0)
    const3 = lambda i: (0, 0, 0)
    outs = [(A_WIDTH, BF16), (512, BF16), (256, BF16), (256, BF16), (256, BF16), (128, BF16), (256, BF16),
            (128, F32)]
    return pl.pallas_call(
        functools.partial(_proj_kernel, tm=tm, seq=seq),
        out_shape=tuple(jax.ShapeDtypeStruct((t, w), dt) for w, dt in outs),
        grid=(t // tm,),
        in_specs=[pl.BlockSpec((tm, D_MODEL), row),
                  pl.BlockSpec(wuv.shape, const2),
                  pl.BlockSpec(wqkv.shape, const2),
                  pl.BlockSpec(lng.shape, const2),
                  pl.BlockSpec(lnb.shape, const2),
                  pl.BlockSpec(ws.shape, const3),
                  pl.BlockSpec(bs.shape, const3)],
        out_specs=tuple(pl.BlockSpec((tm, w), row) for w, _ in outs),
        compiler_params=pltpu.CompilerParams(dimension_semantics=("parallel",), vmem_limit_bytes=VMEM_LIMIT),
        name="in_proj_gmlp",
    )(x2d, wuv, wqkv, lng, lnb, ws, bs)


def _compress_kernel(r_ref, wa_ref, wb_ref, pea_ref, peb_ref, w2_ref, kc_ref, vct_ref):
    r = r_ref[0]
    top = _dot(r, wa_ref[...])
    bot = _dot(r, wb_ref[...])
    pe = _dot(pea_ref[...], wa_ref[...]) + _dot(peb_ref[...], wb_ref[...])
    hid = top + pltpu.roll(bot, bot.shape[0] - 1, 0) + pe[0:1]
    out = _dot(jax.nn.gelu(hid).astype(BF16), w2_ref[...])
    kc_ref[0] = out[:, 0:LANES].astype(BF16)
    vct_ref[0] = out[:, LANES:2 * LANES].T.astype(BF16)


def _compress(cmp3, wa, wb, pea, peb, w2):
    b, nc, width = cmp3.shape
    const2 = lambda i: (0, 0)
    return pl.pallas_call(
        _compress_kernel,
        out_shape=(jax.ShapeDtypeStruct((b, nc, LANES), BF16), jax.ShapeDtypeStruct((b, LANES, nc), BF16)),
        grid=(b,),
        in_specs=[pl.BlockSpec((1, nc, width), lambda i: (i, 0, 0)),
                  pl.BlockSpec(wa.shape, const2), pl.BlockSpec(wb.shape, const2),
                  pl.BlockSpec(pea.shape, const2), pl.BlockSpec(peb.shape, const2),
                  pl.BlockSpec(w2.shape, const2)],
        out_specs=(pl.BlockSpec((1, nc, LANES), lambda i: (i, 0, 0)),
                   pl.BlockSpec((1, LANES, nc), lambda i: (i, 0, 0))),
        compiler_params=pltpu.CompilerParams(dimension_semantics=("parallel",), vmem_limit_bytes=VMEM_LIMIT),
        name="kv_compress",
    )(cmp3, wa, wb, pea, peb, w2)


def _softmax_tile(s, m_sc):
    cols = [s[:, j * LANES:(j + 1) * LANES] for j in range(s.shape[1] // LANES)]
    m_old = m_sc[...]
    m_new = jnp.maximum(m_old, jnp.max(functools.reduce(jnp.maximum, cols), axis=-1, keepdims=True))
    m_sc[...] = m_new
    p = jnp.concatenate([jnp.exp(c - m_new) for c in cols], axis=1).astype(BF16)
    return p, jnp.exp(m_old - m_new)


def _scale_both(acc, alpha):
    return jnp.concatenate([acc[:, :LANES] * alpha, acc[:, LANES:] * alpha], axis=1)


def _nsa_kernel(q_ref, gate_ref, kc_ref, vct_ref, kaug_ref, vaug_ref, wink_ref, winv_ref, fc_ref, dn_ref, ovt_ref,
                o_ref, qa_all, s_all, p_all, m_all, acc_all, *, nc, n_tiles):
    ib = pl.program_id(1)
    t0 = ib * Q_BLOCK
    lane = lax.broadcasted_iota(jnp.int32, (Q_BLOCK, LANES), 1)
    qrow = lax.broadcasted_iota(jnp.int32, (Q_BLOCK, LANES), 0)
    q = q_ref[...]
    gates = gate_ref[...]
    kc = kc_ref[0]
    vct = vct_ref[0]
    ovt = ovt_ref[...]

    def stack_heads(fn):
        return jnp.concatenate([fn(g) for g in range(GROUP)], axis=0)

    near0 = 2 * ib - NEAR_BACK // SEL_BLOCK
    n_far = jnp.maximum(ib - 4, 0) // 4
    n_pairs = (n_far + 1) // 2
    kstart = jnp.maximum(t0 - NEAR_BACK, 0)
    j0 = pl.multiple_of(kstart - (t0 - NEAR_BACK), LANES)
    kstart = pl.multiple_of(kstart, LANES)
    wstart = jnp.maximum(t0 - WINDOW, 0)
    jw = pl.multiple_of(wstart - (t0 - NEAR_BACK), LANES)
    wstart = pl.multiple_of(wstart, LANES)
    win_edge = jnp.where((lane > qrow) | (t0 < WINDOW), 0.0, NEG_INF)

    fc_row = pl.multiple_of((nc - 8) - 8 * ib, 8)
    qcol = lax.broadcasted_iota(jnp.int32, (1, ROWS), 1) % Q_BLOCK
    has_cmp = jnp.where(t0 + qcol >= CMP_BLOCK - 1, 1.0, 0.0)
    qhs, o_cts, imps = [], [], []
    for hkv in range(KV_HEADS):
        mine = (lane >= HEAD_DIM) if hkv else (lane < HEAD_DIM)
        qh = stack_heads(lambda g: jnp.where(mine, q[:, g * LANES:(g + 1) * LANES], 0.0).astype(BF16))
        lct = _dot_nt(kc, qh)
        lct = lct + jnp.concatenate([fc_ref[hkv * GROUP + g, pl.ds(fc_row, nc), :] for g in range(GROUP)], axis=1)
        e = jnp.exp(lct - jnp.max(lct, axis=0, keepdims=True))
        pct = e * (has_cmp / jnp.sum(e, axis=0, keepdims=True))
        o_cts.append(_dot(vct, pct.astype(BF16)))
        psum = pct[:, 0:LANES] + pct[:, LANES:2 * LANES] + pct[:, 2 * LANES:3 * LANES] + pct[:, 3 * LANES:]
        p_hi = psum.astype(BF16)
        p_lo = (psum - p_hi.astype(F32)).astype(BF16)
        imps.append(_dot(ovt, p_hi) + _dot(ovt, p_lo))
        qhs.append(qh)

    blk = lax.broadcasted_iota(jnp.int32, (LANES, KV_HEADS * Q_BLOCK), 0)
    col = lax.broadcasted_iota(jnp.int32, (LANES, KV_HEADS * Q_BLOCK), 1)
    jq = 2 * ib + jnp.where(col % Q_BLOCK >= SEL_BLOCK, 1, 0)
    forced = (blk == 0) | (blk == jq) | (blk == jq - 1)
    cand = jnp.where(forced, -3e38, jnp.where(blk > jq, NEG_INF, jnp.concatenate(imps, axis=1)))
    blk_f = blk.astype(F32)

    def pick_one(_, carry):
        cur, chosen = carry
        mx = jnp.max(cur, axis=0, keepdims=True)
        first = jnp.min(jnp.where(cur == mx, blk_f, float(LANES)), axis=0, keepdims=True)
        pick = blk_f == first
        return jnp.where(pick, -3e38, cur), jnp.where(pick, 1.0, chosen)

    _, sel_t = lax.fori_loop(0, N_SELECT - 3, pick_one, (cand, jnp.where(forced, 1.0, 0.0)))

    outs = []
    for hkv in range(KV_HEADS):
        qa_sc, s_buf, p_buf = qa_all.at[hkv], s_all.at[hkv], p_all.at[hkv]
        m_sc, acc_sc = m_all.at[hkv], acc_all.at[hkv]
        qh = qhs[hkv]
        sel = sel_t[:, hkv * Q_BLOCK:(hkv + 1) * Q_BLOCK].T

        m_near = jnp.where(sel > 0, 0.0, NEG_INF).astype(BF16)
        m_far = jnp.where((sel > 0) & (lane < near0), 0.0, NEG_INF).astype(BF16)
        qa_sc[0, :, 0:LANES] = qh
        qa_sc[0, :, LANES:2 * LANES] = jnp.concatenate([m_far] * GROUP, axis=0)
        qa_sc[1, :, 0:LANES] = qh
        qa_sc[1, :, LANES:2 * LANES] = jnp.concatenate([m_near] * GROUP, axis=0)
        m_sc[...] = jnp.full(m_sc.shape, -jnp.inf, F32)
        acc_sc[...] = jnp.zeros(acc_sc.shape, F32)
        p_buf[1] = jnp.zeros((ROWS, FAR_TILE), BF16)

        def far_logits(tile, slot):
            ks = pl.multiple_of(tile * FAR_TILE, FAR_TILE)
            s_buf[slot] = _dot_nt(qa_sc[0], kaug_ref[pl.ds(ks, FAR_TILE), :])

        def far_values(tile, slot):
            ks = pl.multiple_of(tile * FAR_TILE, FAR_TILE)
            return _dot(p_buf[slot], vaug_ref[pl.ds(ks, FAR_TILE), :])

        def absorb(slot, pending):
            p, alpha = _softmax_tile(s_buf[slot], m_sc)
            p_buf[slot] = p
            acc_sc[...] = _scale_both(acc_sc[...] + pending, alpha)

        far_logits(0, 0)

        sw = _dot_nt(qh, wink_ref[pl.ds(wstart, WIN_KEYS), :])
        sw = sw + stack_heads(lambda g: dn_ref[hkv * GROUP + g, :, pl.ds(jw, WIN_KEYS)])
        sw = jnp.concatenate([sw[:, :LANES] + jnp.concatenate([win_edge] * GROUP, axis=0), sw[:, LANES:]], axis=1)
        pw = jnp.exp(sw - jnp.max(sw, axis=-1, keepdims=True)).astype(BF16)
        ow = _dot(pw, winv_ref[pl.ds(wstart, WIN_KEYS), :])
        o_w = ow[:, :LANES] / ow[:, LANES:]

        def pair(i, carry):
            t = 2 * i
            far_logits(t + 1, 1)
            absorb(0, far_values(jnp.maximum(t - 1, 0), 1))
            far_logits(jnp.minimum(t + 2, n_tiles - 1), 0)
            absorb(1, far_values(t, 0))
            return carry

        lax.fori_loop(0, n_pairs, pair, 0)

        pending = far_values(jnp.maximum(2 * n_pairs - 1, 0), 1)
        acc = acc_sc[...]
        for i in range(NEAR // FAR_TILE):
            ks = pl.multiple_of(kstart + i * FAR_TILE, LANES)
            js = pl.multiple_of(j0 + i * FAR_TILE, LANES)
            s = _dot_nt(qa_sc[1], kaug_ref[pl.ds(ks, FAR_TILE), :])
            s = s + stack_heads(lambda g: dn_ref[hkv * GROUP + g, :, pl.ds(js, FAR_TILE)])
            p, alpha = _softmax_tile(s, m_sc)
            acc = _scale_both(acc + pending, alpha)
            pending = _dot(p, vaug_ref[pl.ds(ks, FAR_TILE), :])
        acc = acc + pending
        o_s = acc[:, :LANES] / acc[:, LANES:]

        per_head = []
        for g in range(GROUP):
            c = 3 * (hkv * GROUP + g)
            rs = slice(g * Q_BLOCK, (g + 1) * Q_BLOCK)
            o_c = o_cts[hkv][:, rs].T
            per_head.append(gates[:, c:c + 1] * o_c + gates[:, c + 1:c + 2] * o_s[rs]
                            + gates[:, c + 2:c + 3] * o_w[rs])
        outs.append(per_head)

    for g in range(GROUP):
        o_ref[:, g * LANES:(g + 1) * LANES] = jnp.where(lane < HEAD_DIM, outs[0][g], outs[1][g]).astype(BF16)


def _nsa(qs, gates, kc, vct, kaug, vaug, wink, winv, fc, dn, ovt, *, batch, seq):
    nc = seq // CMP_STRIDE
    nq = seq // Q_BLOCK
    qrow = lambda b, i: (b * nq + i, 0)
    per_batch = lambda b, i: (b, 0)
    const2 = lambda b, i: (0, 0)
    const3 = lambda b, i: (0, 0, 0)
    once = pl.Buffered(1)
    return pl.pallas_call(
        functools.partial(_nsa_kernel, nc=nc, n_tiles=seq // FAR_TILE),
        out_shape=jax.ShapeDtypeStruct((batch * seq, 4 * LANES), BF16),
        grid=(batch, nq),
        in_specs=[pl.BlockSpec((Q_BLOCK, 4 * LANES), qrow),
                  pl.BlockSpec((Q_BLOCK, LANES), qrow),
                  pl.BlockSpec((1, nc, LANES), lambda b, i: (b, 0, 0)),
                  pl.BlockSpec((1, LANES, nc), lambda b, i: (b, 0, 0)),
                  pl.BlockSpec((seq, 256), per_batch, pipeline_mode=once),
                  pl.BlockSpec((seq, 256), per_batch, pipeline_mode=once),
                  pl.BlockSpec((seq, LANES), per_batch, pipeline_mode=once),
                  pl.BlockSpec((seq, 256), per_batch, pipeline_mode=once),
                  pl.BlockSpec(fc.shape, const3, pipeline_mode=once),
                  pl.BlockSpec(dn.shape, const3, pipeline_mode=once),
                  pl.BlockSpec(ovt.shape, const2)],
        out_specs=pl.BlockSpec((Q_BLOCK, 4 * LANES), qrow),
        scratch_shapes=[pltpu.VMEM((KV_HEADS, 2, ROWS, 2 * LANES), BF16),
                        pltpu.VMEM((KV_HEADS, 2, ROWS, FAR_TILE), F32),
                        pltpu.VMEM((KV_HEADS, 2, ROWS, FAR_TILE), BF16),
                        pltpu.VMEM((KV_HEADS, ROWS, LANES), F32),
                        pltpu.VMEM((KV_HEADS, ROWS, 2 * LANES), F32)],
        compiler_params=pltpu.CompilerParams(dimension_semantics=("parallel", "arbitrary"),
                                             vmem_limit_bytes=VMEM_LIMIT),
        name="sparse_attention",
    )(qs, gates, kc, vct, kaug, vaug, wink, winv, fc, dn, ovt)


def _outproj_kernel(x_ref, a_ref, b_ref, wa_ref, wb_ref, g_ref, beta_ref, o_ref):
    y = _dot(a_ref[...], wa_ref[...]) + _dot(b_ref[...], wb_ref[...])
    o_ref[...] = _layer_norm(ALPHA * x_ref[...] + y, g_ref[...], beta_ref[...])


def _outproj(x2d, a, b, wa, wb, g, beta, *, tm=512):
    t = x2d.shape[0]
    row = lambda i: (i, 0)
    const2 = lambda i: (0, 0)
    return pl.pallas_call(
        _outproj_kernel,
        out_shape=jax.ShapeDtypeStruct((t, D_MODEL), F32),
        grid=(t // tm,),
        in_specs=[pl.BlockSpec((tm, D_MODEL), row), pl.BlockSpec((tm, A_WIDTH), row), pl.BlockSpec((tm, 512), row),
                  pl.BlockSpec(wa.shape, const2), pl.BlockSpec(wb.shape, const2),
                  pl.BlockSpec(g.shape, const2), pl.BlockSpec(beta.shape, const2)],
        out_specs=pl.BlockSpec((tm, D_MODEL), row),
        compiler_params=pltpu.CompilerParams(dimension_semantics=("parallel",), vmem_limit_bytes=VMEM_LIMIT),
        name="out_proj_norm",
    )(x2d, a, b, wa, wb, g, beta)


def _ffn_kernel(x_ref, wg_ref, wu_ref, wd_ref, g_ref, beta_ref, o_ref, acc_ref, *, hc):
    x = x_ref[...]
    xb = x.astype(BF16)
    hidden = wg_ref.shape[1]
    for c in range(hidden // hc):
        cs = slice(c * hc, (c + 1) * hc)
        gate = _dot(xb, wg_ref[:, cs])
        up = _dot(xb, wu_ref[:, cs])
        part = _dot((jax.nn.silu(gate) * up).astype(BF16), wd_ref[cs, :])
        if c == 0:
            acc_ref[...] = part
        else:
            acc_ref[...] += part
    o_ref[...] = _layer_norm(ALPHA * x + acc_ref[...], g_ref[...], beta_ref[...])


def _ffn(x2d, wg, wu, wd, g, beta, *, tm=512, hc=256):
    t = x2d.shape[0]
    row = lambda i: (i, 0)
    const2 = lambda i: (0, 0)
    once = pl.Buffered(1)
    return pl.pallas_call(
        functools.partial(_ffn_kernel, hc=hc),
        out_shape=jax.ShapeDtypeStruct((t, D_MODEL), F32),
        grid=(t // tm,),
        in_specs=[pl.BlockSpec((tm, D_MODEL), row),
                  pl.BlockSpec(wg.shape, const2, pipeline_mode=once),
                  pl.BlockSpec(wu.shape, const2, pipeline_mode=once),
                  pl.BlockSpec(wd.shape, const2, pipeline_mode=once),
                  pl.BlockSpec(g.shape, const2), pl.BlockSpec(beta.shape, const2)],
        out_specs=pl.BlockSpec((tm, D_MODEL), row),
        scratch_shapes=[pltpu.VMEM((tm, D_MODEL), F32)],
        compiler_params=pltpu.CompilerParams(dimension_semantics=("parallel",), vmem_limit_bytes=VMEM_LIMIT),
        name="swiglu_ffn_norm",
    )(x2d, wg, wu, wd, g, beta)


def _conv_kernel(x_ref, win_ref, bin_ref, dww_ref, dwb_ref, lng_ref, lnb_ref, wout_ref, bout_ref,
                 g_ref, beta_ref, o_ref, buf_ref, cv_ref, *, tm, rc):
    @pl.when(pl.program_id(1) == 0)
    def _():
        buf_ref[0:HALO, :] = jnp.zeros((HALO, D_MODEL), F32)

    x = x_ref[...]
    h = _dot(x.astype(BF16), win_ref[...]) + bin_ref[...]
    buf_ref[HALO:HALO + tm, :] = h[:, :D_MODEL] * jax.nn.sigmoid(h[:, D_MODEL:])

    lead = HALO - (CONV_WIDTH - 1)

    def conv_rows(r, carry):
        r0 = pl.multiple_of(r * rc, rc)
        for lc in range(D_MODEL // LANES):
            ls = slice(lc * LANES, (lc + 1) * LANES)
            acc = jnp.broadcast_to(dwb_ref[:, ls], (rc, LANES))
            za = buf_ref[pl.ds(r0, rc + HALO), ls]
            for b in range(8):
                base, shift = 8 * ((lead + b) // 8), (lead + b) % 8
                span = rc + (8 if shift else 0)
                yb = None
                for a, j in enumerate(range(b, CONV_WIDTH, 8)):
                    term = za[base + 8 * a:base + 8 * a + span] * dww_ref[j:j + 1, ls]
                    yb = term if yb is None else yb + term
                acc = acc + yb[shift:shift + rc]
            cv_ref[pl.ds(r0, rc), ls] = acc
        return carry

    lax.fori_loop(0, tm // rc, conv_rows, 0)
    buf_ref[0:HALO, :] = buf_ref[tm:tm + HALO, :]

    y = jax.nn.silu(_layer_norm(cv_ref[...], lng_ref[...], lnb_ref[...]))
    y = _dot(y.astype(BF16), wout_ref[...]) + bout_ref[...]
    o_ref[...] = _layer_norm(ALPHA * x + y, g_ref[...], beta_ref[...])


def _conv(x2d, w_in, b_in, dw_w, dw_b, ln_g, ln_b, w_out, b_out, g, beta, *, batch, seq, tm=512, rc=64):
    nt = seq // tm
    row = lambda b, i: (b * nt + i, 0)
    const2 = lambda b, i: (0, 0)
    consts = (w_in, b_in, dw_w, dw_b, ln_g, ln_b, w_out, b_out, g, beta)
    return pl.pallas_call(
        functools.partial(_conv_kernel, tm=tm, rc=rc),
        out_shape=jax.ShapeDtypeStruct((batch * seq, D_MODEL), F32),
        grid=(batch, nt),
        in_specs=[pl.BlockSpec((tm, D_MODEL), row)] + [pl.BlockSpec(c.shape, const2) for c in consts],
        out_specs=pl.BlockSpec((tm, D_MODEL), row),
        scratch_shapes=[pltpu.VMEM((HALO + tm, D_MODEL), F32), pltpu.VMEM((tm, D_MODEL), F32)],
        compiler_params=pltpu.CompilerParams(dimension_semantics=("parallel", "arbitrary"),
                                             vmem_limit_bytes=VMEM_LIMIT),
        name="conv_module_norm",
    )(x2d, *consts)


def _head_perm():
    p = np.arange(HEADS * HEAD_DIM)
    g, half, d = p // LANES, (p % LANES) // HEAD_DIM, p % HEAD_DIM
    return (g + GROUP * half) * HEAD_DIM + d


def _compress_weights(pe_k, w1_k, w2_k, pe_v, w1_v, w2_v):
    eye = jnp.eye(KV_HEADS, dtype=F32)
    w1 = jnp.stack([w1_k, w1_v]).reshape(2, CMP_BLOCK, HEAD_DIM, HEAD_DIM)
    w1x = jnp.einsum('klde,km,hn->lkhdmne', w1, eye, eye).reshape(CMP_BLOCK * 256, 256)
    half = CMP_STRIDE * 256
    pe = jnp.stack([pe_k, pe_v])
    pex = jnp.broadcast_to(pe.transpose(1, 0, 2)[:, :, None, :], (CMP_BLOCK, 2, KV_HEADS, HEAD_DIM))
    pex = pex.reshape(2, half)
    w2 = jnp.stack([w2_k, w2_v])
    w2x = jnp.einsum('kde,km,hn->khdmne', w2, eye, eye).reshape(256, 256)
    pea = jnp.broadcast_to(pex[0:1], (8, half)).astype(BF16)
    peb = jnp.broadcast_to(pex[1:2], (8, half)).astype(BF16)
    return w1x[:half].astype(BF16), w1x[half:].astype(BF16), pea, peb, w2x.astype(BF16)


def _overlap_matrix(nc):
    c0 = np.arange(nc)[None, :] * CMP_STRIDE
    s0 = np.arange(LANES)[:, None] * SEL_BLOCK
    ov = (c0 < s0 + SEL_BLOCK) & (c0 + CMP_BLOCK > s0) & (np.arange(nc)[None, :] < nc - 1)
    return jnp.asarray(ov, dtype=BF16)


def _even_layer(x2d, rel_bias, w_in, w_out, ln_g, ln_b, w_s, b_s, pe_k, w1_k, w2_k, pe_v, w1_v, w2_v,
                norm_g, norm_b, *, batch, seq):
    nc = seq // CMP_STRIDE
    assert seq % FAR_TILE == 0 and seq >= NEAR and seq // SEL_BLOCK <= LANES
    perm = _head_perm()
    qw = HEADS * HEAD_DIM
    o = 2 * A_WIDTH
    wuv = w_in[:, :o].astype(BF16)
    wq = w_in[:, o:o + qw][:, perm]
    wkv = w_in[:, o + qw:o + qw + 768]
    wgt = jnp.pad(w_in[:, o + qw + 768:], ((0, 0), (0, LANES - 3 * HEADS)))
    wqkv = jnp.concatenate([wq, wkv, wgt], axis=1).astype(BF16)
    bs = jnp.broadcast_to(b_s[:, :, None], (A_GROUPS, CHUNK, A_WIDTH // A_GROUPS))
    a_out, qs, cmp2, kaug, vaug, wink, winv, gates = _proj(
        x2d, wuv, wqkv, ln_g[None, :], ln_b[None, :], w_s, bs, seq=seq)

    kc, vct = _compress(cmp2.reshape(batch, nc, CMP_STRIDE * 256),
                    *_compress_weights(pe_k, w1_k, w2_k, pe_v, w1_v, w2_v))

    fc, dn = _bias_tables(_bucket_thresholds(seq), rel_bias.T.reshape(-1), nc)
    b_out = _nsa(qs, gates, kc, vct, kaug, vaug, wink, winv, fc, dn, _overlap_matrix(nc), batch=batch, seq=seq)

    wo_a = w_out[:A_WIDTH].astype(BF16)
    wo_b = w_out[A_WIDTH:][perm].astype(BF16)
    return _outproj(x2d, a_out, b_out, wo_a, wo_b, norm_g[None, :], norm_b[None, :])


def kernel(x, rel_bias, hyb_w_in, hyb_w_out, gmlp_ln_g, gmlp_ln_b, gmlp_w_s, gmlp_b_s, cmp_pe_k, cmp_w1_k, cmp_w2_k, cmp_pe_v, cmp_w1_v, cmp_w2_v, conv_w_in, conv_b_in, conv_dw_w, conv_dw_b, conv_ln_g, conv_ln_b, conv_w_out, conv_b_out, ffn_w_gate, ffn_w_up, ffn_w_down, norm_mix_g, norm_mix_b, norm_ffn_g, norm_ffn_b):
    batch, seq, d = x.shape
    h = x.reshape(batch * seq, d)
    for layer in range(DEPTH):
        i = layer // 2
        if layer % 2 == 0:
            h = _even_layer(h, rel_bias, hyb_w_in[i], hyb_w_out[i], gmlp_ln_g[i], gmlp_ln_b[i],
                            gmlp_w_s[i], gmlp_b_s[i], cmp_pe_k[i], cmp_w1_k[i], cmp_w2_k[i],
                            cmp_pe_v[i], cmp_w1_v[i], cmp_w2_v[i],
                            norm_mix_g[layer], norm_mix_b[layer], batch=batch, seq=seq)
        else:
            h = _conv(h, conv_w_in[i].astype(BF16), conv_b_in[i][None, :], conv_dw_w[i], conv_dw_b[i][None, :],
                      conv_ln_g[i][None, :], conv_ln_b[i][None, :], conv_w_out[i].astype(BF16),
                      conv_b_out[i][None, :], norm_mix_g[layer][None, :], norm_mix_b[layer][None, :],
                      batch=batch, seq=seq)
        h = _ffn(h, ffn_w_gate[layer].astype(BF16), ffn_w_up[layer].astype(BF16),
                 ffn_w_down[layer].astype(BF16), norm_ffn_g[layer][None, :], norm_ffn_b[layer][None, :])
    return h.reshape(batch, seq, d)
```

```python
import functools
import math

import numpy as np
import jax
import jax.numpy as jnp
from jax import lax
from jax.experimental import pallas as pl
from jax.experimental.pallas import tpu as pltpu

F32 = jnp.float32
BF16 = jnp.bfloat16

D_MODEL = 1024
DEPTH = 2
ALPHA = (2 * DEPTH) ** 0.25
CHUNK = 128
A_WIDTH = D_MODEL // 2
A_GROUPS = 4
HEADS = 8
KV_HEADS = 2
GROUP = HEADS // KV_HEADS
HEAD_DIM = (D_MODEL // 2) // HEADS
CMP_STRIDE = 16
CMP_BLOCK = 32
SEL_BLOCK = 64
N_SELECT = 16
WINDOW = 512
Q_BLOCK = 128
N_BUCKETS = 32
MAX_DISTANCE = 1024
CONV_WIDTH = 31
NEG_INF = -1e30
FORCE = 1e9
LN_EPS = 1e-5

LANES = 128
VMEM_LIMIT = 56 * 1024 * 1024

ROWS = GROUP * Q_BLOCK
FAR_TILE = 512
NEAR_TILES = 8
NEAR = NEAR_TILES * Q_BLOCK
NEAR_BACK = NEAR - Q_BLOCK
WIN_KEYS = WINDOW + Q_BLOCK
DN_WIDTH = NEAR + NEAR_BACK
HALO = 32


def _dot(a, b):
    return jnp.dot(a, b, preferred_element_type=F32)


def _dot_nt(a, b):
    return lax.dot_general(a, b, (((1,), (1,)), ((), ())), preferred_element_type=F32)


def _layer_norm(z, g, b):
    mu = jnp.mean(z, axis=-1, keepdims=True)
    d = z - mu
    var = jnp.mean(d * d, axis=-1, keepdims=True)
    return d * lax.rsqrt(var + LN_EPS) * g + b


def _t5_bucket(dist):
    n = jnp.maximum(dist, 0)
    max_exact = N_BUCKETS // 2
    nf = jnp.maximum(n, 1).astype(jnp.float32)
    large = max_exact + (jnp.log(nf / max_exact) / math.log(MAX_DISTANCE / max_exact)
                         * (N_BUCKETS - max_exact)).astype(jnp.int32)
    large = jnp.minimum(large, N_BUCKETS - 1)
    return jnp.where(n < max_exact, n, large)


def _bucket_thresholds(seq):
    b = _t5_bucket(jnp.arange(seq, dtype=jnp.int32))
    j = jnp.arange(N_BUCKETS, dtype=jnp.int32)
    return jnp.sum((b[None, :] < j[:, None]).astype(jnp.int32), axis=1).astype(jnp.int32)


def _tables_kernel(thr_ref, rb_ref, fc_ref, dn_ref, *, nc):
    h = pl.program_id(0)
    base = h * N_BUCKETS
    far = rb_ref[base + N_BUCKETS - 1]

    def bias_of(dist):
        val = jnp.full(dist.shape, far, F32)
        for j in range(N_BUCKETS - 1, 0, -1):
            val = jnp.where(dist < thr_ref[j], rb_ref[base + j - 1], val)
        return val

    ql = lax.broadcasted_iota(jnp.int32, (Q_BLOCK, LANES), 0)
    ln = lax.broadcasted_iota(jnp.int32, (Q_BLOCK, LANES), 1)
    for c in range(2 * nc // Q_BLOCK):
        dist = ln - CMP_STRIDE * (ql + (c * Q_BLOCK - (nc - 8))) - (CMP_BLOCK - 1)
        fc_ref[0, c * Q_BLOCK:(c + 1) * Q_BLOCK, :] = jnp.where(dist >= 0, bias_of(dist), NEG_INF)
    for c in range(DN_WIDTH // LANES):
        dist = ql + (NEAR_BACK - c * LANES) - ln
        dn_ref[0, :, c * LANES:(c + 1) * LANES] = jnp.where(dist >= 0, bias_of(dist) - far, NEG_INF)


def _bias_tables(thr, rb_flat, nc):
    return pl.pallas_call(
        functools.partial(_tables_kernel, nc=nc),
        out_shape=(jax.ShapeDtypeStruct((HEADS, 2 * nc, Q_BLOCK), F32),
                   jax.ShapeDtypeStruct((HEADS, Q_BLOCK, DN_WIDTH), F32)),
        grid=(HEADS,),
        in_specs=[pl.BlockSpec(memory_space=pltpu.SMEM), pl.BlockSpec(memory_space=pltpu.SMEM)],
        out_specs=(pl.BlockSpec((1, 2 * nc, Q_BLOCK), lambda h: (h, 0, 0)),
                   pl.BlockSpec((1, Q_BLOCK, DN_WIDTH), lambda h: (h, 0, 0))),
        name="bias_tables",
    )(thr, rb_flat)


def _proj_kernel(x_ref, wuv_ref, wqkv_ref, lng_ref, lnb_ref, ws_ref, bs_ref,
                 a_ref, q_ref, cmp_ref, kaug_ref, vaug_ref, wink_ref, winv_ref, gate_ref, cmp_sc, *, tm, seq):
    xb = x_ref[...].astype(BF16)
    uv = jax.nn.gelu(_dot(xb, wuv_ref[...]))
    u = uv[:, :A_WIDTH]
    v = _layer_norm(uv[:, A_WIDTH:], lng_ref[...], lnb_ref[...]).astype(BF16)
    row = lax.broadcasted_iota(jnp.int32, (CHUNK, CHUNK), 0)
    col = lax.broadcasted_iota(jnp.int32, (CHUNK, CHUNK), 1)
    gd = A_WIDTH // A_GROUPS
    for g in range(A_GROUPS):
        w = jnp.where(col <= row, ws_ref[g], 0.0).astype(BF16)
        for c in range(tm // CHUNK):
            rs = slice(c * CHUNK, (c + 1) * CHUNK)
            cs = slice(g * gd, (g + 1) * gd)
            s = _dot(w, v[rs, cs]) + bs_ref[g]
            a_ref[rs, cs] = (u[rs, cs] * s).astype(BF16)

    h = _dot(xb, wqkv_ref[...])
    q_ref[...] = (h[:, 0:512] * (HEAD_DIM ** -0.5)).astype(BF16)
    for half in range(2):
        cmp_sc[half] = h[:, 512 + half * LANES:512 + (half + 1) * LANES]
        for l in range(CMP_STRIDE):
            rows = cmp_sc[half, pl.ds(l, tm // CMP_STRIDE, stride=CMP_STRIDE), :]
            cmp_ref[:, l * 256 + half * LANES:l * 256 + (half + 1) * LANES] = rows.astype(BF16)
    kaug_ref[:, 0:LANES] = h[:, 768:896].astype(BF16)
    pos = (pl.program_id(0) * tm) % seq + lax.broadcasted_iota(jnp.int32, (tm, LANES), 0)
    blk = lax.broadcasted_iota(jnp.int32, (tm, LANES), 1)
    kaug_ref[:, LANES:2 * LANES] = jnp.where(pos // SEL_BLOCK == blk, 1.0, 0.0).astype(BF16)
    ones = jnp.ones((tm, LANES), BF16)
    vaug_ref[:, 0:LANES] = h[:, 896:1024].astype(BF16)
    vaug_ref[:, LANES:2 * LANES] = ones
    wink_ref[...] = h[:, 1024:1152].astype(BF16)
    winv_ref[:, 0:LANES] = h[:, 1152:1280].astype(BF16)
    winv_ref[:, LANES:2 * LANES] = ones
    gate_ref[...] = jax.nn.sigmoid(h[:, 1280:1408])


def _proj(x2d, wuv, wqkv, lng, lnb, ws, bs, *, seq, tm=512):
    t = x2d.shape[0]
    row = lambda i: (i, 0)
    const2 = lambda i: (0, 0)
    const3 = lambda i: (0, 0, 0)
    outs = [(1, A_WIDTH, BF16), (1, 512, BF16), (CMP_STRIDE, CMP_STRIDE * 256, BF16), (1, 256, BF16),
            (1, 256, BF16), (1, 128, BF16), (1, 256, BF16), (1, 128, F32)]
    return pl.pallas_call(
        functools.partial(_proj_kernel, tm=tm, seq=seq),
        out_shape=tuple(jax.ShapeDtypeStruct((t // r, w), dt) for r, w, dt in outs),
        grid=(t // tm,),
        in_specs=[pl.BlockSpec((tm, D_MODEL), row),
                  pl.BlockSpec(wuv.shape, const2),
                  pl.BlockSpec(wqkv.shape, const2),
                  pl.BlockSpec(lng.shape, const2),
                  pl.BlockSpec(lnb.shape, const2),
                  pl.BlockSpec(ws.shape, const3),
                  pl.BlockSpec(bs.shape, const3)],
        out_specs=tuple(pl.BlockSpec((tm // r, w), row) for r, w, _ in outs),
        scratch_shapes=[pltpu.VMEM((2, tm, LANES), F32)],
        compiler_params=pltpu.CompilerParams(dimension_semantics=("parallel",), vmem_limit_bytes=VMEM_LIMIT),
        name="in_proj_gmlp",
    )(x2d, wuv, wqkv, lng, lnb, ws, bs)


def _compress_kernel(r_ref, wa_ref, wb_ref, pea_ref, peb_ref, w2_ref, kc_ref, vct_ref):
    r = r_ref[0]
    top = _dot(r, wa_ref[...])
    bot = _dot(r, wb_ref[...])
    pe = _dot(pea_ref[...], wa_ref[...]) + _dot(peb_ref[...], wb_ref[...])
    hid = top + pltpu.roll(bot, bot.shape[0] - 1, 0) + pe[0:1]
    out = _dot(jax.nn.gelu(hid).astype(BF16), w2_ref[...])
    kc_ref[0] = out[:, 0:LANES].astype(BF16)
    vct_ref[0] = out[:, LANES:2 * LANES].T.astype(BF16)


def _compress(cmp3, wa, wb, pea, peb, w2):
    b, nc, width = cmp3.shape
    const2 = lambda i: (0, 0)
    return pl.pallas_call(
        _compress_kernel,
        out_shape=(jax.ShapeDtypeStruct((b, nc, LANES), BF16), jax.ShapeDtypeStruct((b, LANES, nc), BF16)),
        grid=(b,),
        in_specs=[pl.BlockSpec((1, nc, width), lambda i: (i, 0, 0)),
                  pl.BlockSpec(wa.shape, const2), pl.BlockSpec(wb.shape, const2),
                  pl.BlockSpec(pea.shape, const2), pl.BlockSpec(peb.shape, const2),
                  pl.BlockSpec(w2.shape, const2)],
        out_specs=(pl.BlockSpec((1, nc, LANES), lambda i: (i, 0, 0)),
                   pl.BlockSpec((1, LANES, nc), lambda i: (i, 0, 0))),
        compiler_params=pltpu.CompilerParams(dimension_semantics=("parallel",), vmem_limit_bytes=VMEM_LIMIT),
        name="kv_compress",
    )(cmp3, wa, wb, pea, peb, w2)


def _softmax_tile(s, m_sc):
    cols = [s[:, j * LANES:(j + 1) * LANES] for j in range(s.shape[1] // LANES)]
    m_old = m_sc[...]
    m_new = jnp.maximum(m_old, jnp.max(functools.reduce(jnp.maximum, cols), axis=-1, keepdims=True))
    m_sc[...] = m_new
    p = jnp.concatenate([jnp.exp(c - m_new) for c in cols], axis=1).astype(BF16)
    return p, jnp.exp(m_old - m_new)


def _scale_both(acc, alpha):
    return jnp.concatenate([acc[:, :LANES] * alpha, acc[:, LANES:] * alpha], axis=1)


def _nsa_kernel(q_ref, gate_ref, kc_ref, vct_ref, kaug_ref, vaug_ref, wink_ref, winv_ref, fc_ref, dn_ref, ovt_ref,
                o_ref, qa_all, s_all, p_all, m_all, acc_all, *, nc, n_tiles):
    ib = pl.program_id(1)
    t0 = ib * Q_BLOCK
    lane = lax.broadcasted_iota(jnp.int32, (Q_BLOCK, LANES), 1)
    qrow = lax.broadcasted_iota(jnp.int32, (Q_BLOCK, LANES), 0)
    q = q_ref[...]
    gates = gate_ref[...]
    kc = kc_ref[0]
    vct = vct_ref[0]
    ovt = ovt_ref[...]

    def stack_heads(fn):
        return jnp.concatenate([fn(g) for g in range(GROUP)], axis=0)

    near0 = 2 * ib - NEAR_BACK // SEL_BLOCK
    n_far = jnp.maximum(ib - 4, 0) // 4
    n_pairs = (n_far + 1) // 2
    kstart = jnp.maximum(t0 - NEAR_BACK, 0)
    j0 = pl.multiple_of(kstart - (t0 - NEAR_BACK), LANES)
    kstart = pl.multiple_of(kstart, LANES)
    wstart = jnp.maximum(t0 - WINDOW, 0)
    jw = pl.multiple_of(wstart - (t0 - NEAR_BACK), LANES)
    wstart = pl.multiple_of(wstart, LANES)
    win_edge = jnp.where((lane > qrow) | (t0 < WINDOW), 0.0, NEG_INF)

    fc_row = pl.multiple_of((nc - 8) - 8 * ib, 8)
    qcol = lax.broadcasted_iota(jnp.int32, (1, ROWS), 1) % Q_BLOCK
    has_cmp = jnp.where(t0 + qcol >= CMP_BLOCK - 1, 1.0, 0.0)
    qhs, o_cts, imps = [], [], []
    for hkv in range(KV_HEADS):
        mine = (lane >= HEAD_DIM) if hkv else (lane < HEAD_DIM)
        qh = stack_heads(lambda g: jnp.where(mine, q[:, g * LANES:(g + 1) * LANES], 0.0).astype(BF16))
        lct = _dot_nt(kc, qh)
        lct = lct + jnp.concatenate([fc_ref[hkv * GROUP + g, pl.ds(fc_row, nc), :] for g in range(GROUP)], axis=1)
        e = jnp.exp(lct - jnp.max(lct, axis=0, keepdims=True))
        pct = e * (has_cmp / jnp.sum(e, axis=0, keepdims=True))
        o_cts.append(_dot(vct, pct.astype(BF16)))
        psum = pct[:, 0:LANES] + pct[:, LANES:2 * LANES] + pct[:, 2 * LANES:3 * LANES] + pct[:, 3 * LANES:]
        p_hi = psum.astype(BF16)
        p_lo = (psum - p_hi.astype(F32)).astype(BF16)
        imps.append(_dot(ovt, p_hi) + _dot(ovt, p_lo))
        qhs.append(qh)

    blk = lax.broadcasted_iota(jnp.int32, (LANES, KV_HEADS * Q_BLOCK), 0)
    col = lax.broadcasted_iota(jnp.int32, (LANES, KV_HEADS * Q_BLOCK), 1)
    jq = 2 * ib + jnp.where(col % Q_BLOCK >= SEL_BLOCK, 1, 0)
    forced = (blk == 0) | (blk == jq) | (blk == jq - 1)
    cand = jnp.where(forced, -3e38, jnp.where(blk > jq, NEG_INF, jnp.concatenate(imps, axis=1)))
    blk_f = blk.astype(F32)

    def pick_one(_, carry):
        cur, chosen = carry
        mx = jnp.max(cur, axis=0, keepdims=True)
        first = jnp.min(jnp.where(cur == mx, blk_f, float(LANES)), axis=0, keepdims=True)
        pick = blk_f == first
        return jnp.where(pick, -3e38, cur), jnp.where(pick, 1.0, chosen)

    _, sel_t = lax.fori_loop(0, N_SELECT - 3, pick_one, (cand, jnp.where(forced, 1.0, 0.0)))

    outs = []
    for hkv in range(KV_HEADS):
        qa_sc, s_buf, p_buf = qa_all.at[hkv], s_all.at[hkv], p_all.at[hkv]
        m_sc, acc_sc = m_all.at[hkv], acc_all.at[hkv]
        qh = qhs[hkv]
        sel = sel_t[:, hkv * Q_BLOCK:(hkv + 1) * Q_BLOCK].T

        m_near = jnp.where(sel > 0, 0.0, NEG_INF).astype(BF16)
        m_far = jnp.where((sel > 0) & (lane < near0), 0.0, NEG_INF).astype(BF16)
        qa_sc[0, :, 0:LANES] = qh
        qa_sc[0, :, LANES:2 * LANES] = jnp.concatenate([m_far] * GROUP, axis=0)
        qa_sc[1, :, 0:LANES] = qh
        qa_sc[1, :, LANES:2 * LANES] = jnp.concatenate([m_near] * GROUP, axis=0)
        m_sc[...] = jnp.full(m_sc.shape, -jnp.inf, F32)
        acc_sc[...] = jnp.zeros(acc_sc.shape, F32)
        p_buf[1] = jnp.zeros((ROWS, FAR_TILE), BF16)

        def tile_keys(tile):
            near_idx = tile - 2 * n_pairs
            ks = jnp.where(near_idx >= 0, kstart + near_idx * FAR_TILE, tile * FAR_TILE)
            ks = jnp.clip(ks, 0, (n_tiles - 1) * FAR_TILE)
            return pl.multiple_of(ks, LANES), jnp.where(near_idx >= 0, 1, 0)

        def logits(tile, slot):
            ks, variant = tile_keys(tile)
            s_buf[slot] = _dot_nt(qa_sc[variant], kaug_ref[pl.ds(ks, FAR_TILE), :])

        def values(tile, slot):
            ks, _ = tile_keys(tile)
            return _dot(p_buf[slot], vaug_ref[pl.ds(ks, FAR_TILE), :])

        def absorb(slot, pending, table_col=None):
            s = s_buf[slot]
            if table_col is not None:
                js = pl.multiple_of(table_col, LANES)
                s = s + stack_heads(lambda g: dn_ref[hkv * GROUP + g, :, pl.ds(js, FAR_TILE)])
            p, alpha = _softmax_tile(s, m_sc)
            p_buf[slot] = p
            acc_sc[...] = _scale_both(acc_sc[...] + pending, alpha)

        logits(0, 0)

        sw = _dot_nt(qh, wink_ref[pl.ds(wstart, WIN_KEYS), :])
        sw = sw + stack_heads(lambda g: dn_ref[hkv * GROUP + g, :, pl.ds(jw, WIN_KEYS)])
        sw = jnp.concatenate([sw[:, :LANES] + jnp.concatenate([win_edge] * GROUP, axis=0), sw[:, LANES:]], axis=1)
        pw = jnp.exp(sw - jnp.max(sw, axis=-1, keepdims=True)).astype(BF16)
        ow = _dot(pw, winv_ref[pl.ds(wstart, WIN_KEYS), :])
        o_w = ow[:, :LANES] / ow[:, LANES:]

        def far_pair(i, carry):
            t = 2 * i
            logits(t + 1, 1)
            absorb(0, values(t - 1, 1))
            logits(t + 2, 0)
            absorb(1, values(t, 0))
            return carry

        lax.fori_loop(0, n_pairs, far_pair, 0)
        t = 2 * n_pairs
        logits(t + 1, 1)
        absorb(0, values(t - 1, 1), j0)
        absorb(1, values(t, 0), j0 + FAR_TILE)
        acc = acc_sc[...] + values(t + 1, 1)
        o_s = acc[:, :LANES] / acc[:, LANES:]

        per_head = []
        for g in range(GROUP):
            c = 3 * (hkv * GROUP + g)
            rs = slice(g * Q_BLOCK, (g + 1) * Q_BLOCK)
            o_c = o_cts[hkv][:, rs].T
            per_head.append(gates[:, c:c + 1] * o_c + gates[:, c + 1:c + 2] * o_s[rs]
                            + gates[:, c + 2:c + 3] * o_w[rs])
        outs.append(per_head)

    for g in range(GROUP):
        o_ref[:, g * LANES:(g + 1) * LANES] = jnp.where(lane < HEAD_DIM, outs[0][g], outs[1][g]).astype(BF16)


def _nsa(qs, gates, kc, vct, kaug, vaug, wink, winv, fc, dn, ovt, *, batch, seq):
    nc = seq // CMP_STRIDE
    nq = seq // Q_BLOCK
    qrow = lambda b, i: (b * nq + i, 0)
    per_batch = lambda b, i: (b, 0)
    const2 = lambda b, i: (0, 0)
    const3 = lambda b, i: (0, 0, 0)
    once = pl.Buffered(1)
    return pl.pallas_call(
        functools.partial(_nsa_kernel, nc=nc, n_tiles=seq // FAR_TILE),
        out_shape=jax.ShapeDtypeStruct((batch * seq, 4 * LANES), BF16),
        grid=(batch, nq),
        in_specs=[pl.BlockSpec((Q_BLOCK, 4 * LANES), qrow),
                  pl.BlockSpec((Q_BLOCK, LANES), qrow),
                  pl.BlockSpec((1, nc, LANES), lambda b, i: (b, 0, 0)),
                  pl.BlockSpec((1, LANES, nc), lambda b, i: (b, 0, 0)),
                  pl.BlockSpec((seq, 256), per_batch, pipeline_mode=once),
                  pl.BlockSpec((seq, 256), per_batch, pipeline_mode=once),
                  pl.BlockSpec((seq, LANES), per_batch, pipeline_mode=once),
                  pl.BlockSpec((seq, 256), per_batch, pipeline_mode=once),
                  pl.BlockSpec(fc.shape, const3, pipeline_mode=once),
                  pl.BlockSpec(dn.shape, const3, pipeline_mode=once),
                  pl.BlockSpec(ovt.shape, const2)],
        out_specs=pl.BlockSpec((Q_BLOCK, 4 * LANES), qrow),
        scratch_shapes=[pltpu.VMEM((KV_HEADS, 2, ROWS, 2 * LANES), BF16),
                        pltpu.VMEM((KV_HEADS, 2, ROWS, FAR_TILE), F32),
                        pltpu.VMEM((KV_HEADS, 2, ROWS, FAR_TILE), BF16),
                        pltpu.VMEM((KV_HEADS, ROWS, LANES), F32),
                        pltpu.VMEM((KV_HEADS, ROWS, 2 * LANES), F32)],
        compiler_params=pltpu.CompilerParams(dimension_semantics=("parallel", "arbitrary"),
                                             vmem_limit_bytes=VMEM_LIMIT),
        name="sparse_attention",
    )(qs, gates, kc, vct, kaug, vaug, wink, winv, fc, dn, ovt)


def _outproj_kernel(x_ref, a_ref, b_ref, wa_ref, wb_ref, g_ref, beta_ref, o_ref):
    y = _dot(a_ref[...], wa_ref[...]) + _dot(b_ref[...], wb_ref[...])
    o_ref[...] = _layer_norm(ALPHA * x_ref[...] + y, g_ref[...], beta_ref[...])


def _outproj(x2d, a, b, wa, wb, g, beta, *, tm=512):
    t = x2d.shape[0]
    row = lambda i: (i, 0)
    const2 = lambda i: (0, 0)
    return pl.pallas_call(
        _outproj_kernel,
        out_shape=jax.ShapeDtypeStruct((t, D_MODEL), F32),
        grid=(t // tm,),
        in_specs=[pl.BlockSpec((tm, D_MODEL), row), pl.BlockSpec((tm, A_WIDTH), row), pl.BlockSpec((tm, 512), row),
                  pl.BlockSpec(wa.shape, const2), pl.BlockSpec(wb.shape, const2),
                  pl.BlockSpec(g.shape, const2), pl.BlockSpec(beta.shape, const2)],
        out_specs=pl.BlockSpec((tm, D_MODEL), row),
        compiler_params=pltpu.CompilerParams(dimension_semantics=("parallel",), vmem_limit_bytes=VMEM_LIMIT),
        name="out_proj_norm",
    )(x2d, a, b, wa, wb, g, beta)


def _ffn_kernel(x_ref, wg_ref, wu_ref, wd_ref, g_ref, beta_ref, o_ref, acc_ref, *, hc):
    x = x_ref[...]
    xb = x.astype(BF16)
    hidden = wg_ref.shape[1]
    for c in range(hidden // hc):
        cs = slice(c * hc, (c + 1) * hc)
        gate = _dot(xb, wg_ref[:, cs])
        up = _dot(xb, wu_ref[:, cs])
        part = _dot((jax.nn.silu(gate) * up).astype(BF16), wd_ref[cs, :])
        if c == 0:
            acc_ref[...] = part
        else:
            acc_ref[...] += part
    o_ref[...] = _layer_norm(ALPHA * x + acc_ref[...], g_ref[...], beta_ref[...])


def _ffn(x2d, wg, wu, wd, g, beta, *, tm=512, hc=256):
    t = x2d.shape[0]
    row = lambda i: (i, 0)
    const2 = lambda i: (0, 0)
    once = pl.Buffered(1)
    return pl.pallas_call(
        functools.partial(_ffn_kernel, hc=hc),
        out_shape=jax.ShapeDtypeStruct((t, D_MODEL), F32),
        grid=(t // tm,),
        in_specs=[pl.BlockSpec((tm, D_MODEL), row),
                  pl.BlockSpec(wg.shape, const2, pipeline_mode=once),
                  pl.BlockSpec(wu.shape, const2, pipeline_mode=once),
                  pl.BlockSpec(wd.shape, const2, pipeline_mode=once),
                  pl.BlockSpec(g.shape, const2), pl.BlockSpec(beta.shape, const2)],
        out_specs=pl.BlockSpec((tm, D_MODEL), row),
        scratch_shapes=[pltpu.VMEM((tm, D_MODEL), F32)],
        compiler_params=pltpu.CompilerParams(dimension_semantics=("parallel",), vmem_limit_bytes=VMEM_LIMIT),
        name="swiglu_ffn_norm",
    )(x2d, wg, wu, wd, g, beta)


def _conv_kernel(x_ref, win_ref, bin_ref, dww_ref, dwb_ref, lng_ref, lnb_ref, wout_ref, bout_ref,
                 g_ref, beta_ref, o_ref, buf_ref, cv_ref, *, tm, rc):
    @pl.when(pl.program_id(1) == 0)
    def _():
        buf_ref[0:HALO, :] = jnp.zeros((HALO, D_MODEL), F32)

    x = x_ref[...]
    h = _dot(x.astype(BF16), win_ref[...]) + bin_ref[...]
    buf_ref[HALO:HALO + tm, :] = h[:, :D_MODEL] * jax.nn.sigmoid(h[:, D_MODEL:])

    lead = HALO - (CONV_WIDTH - 1)

    def conv_rows(r, carry):
        r0 = pl.multiple_of(r * rc, rc)
        for lc in range(D_MODEL // LANES):
            ls = slice(lc * LANES, (lc + 1) * LANES)
            acc = jnp.broadcast_to(dwb_ref[:, ls], (rc, LANES))
            za = buf_ref[pl.ds(r0, rc + HALO), ls]
            for b in range(8):
                base, shift = 8 * ((lead + b) // 8), (lead + b) % 8
                span = rc + (8 if shift else 0)
                yb = None
                for a, j in enumerate(range(b, CONV_WIDTH, 8)):
                    term = za[base + 8 * a:base + 8 * a + span] * dww_ref[j:j + 1, ls]
                    yb = term if yb is None else yb + term
                acc = acc + yb[shift:shift + rc]
            cv_ref[pl.ds(r0, rc), ls] = acc
        return carry

    lax.fori_loop(0, tm // rc, conv_rows, 0)
    buf_ref[0:HALO, :] = buf_ref[tm:tm + HALO, :]

    y = jax.nn.silu(_layer_norm(cv_ref[...], lng_ref[...], lnb_ref[...]))
    y = _dot(y.astype(BF16), wout_ref[...]) + bout_ref[...]
    o_ref[...] = _layer_norm(ALPHA * x + y, g_ref[...], beta_ref[...])


def _conv(x2d, w_in, b_in, dw_w, dw_b, ln_g, ln_b, w_out, b_out, g, beta, *, batch, seq, tm=512, rc=64):
    nt = seq // tm
    row = lambda b, i: (b * nt + i, 0)
    const2 = lambda b, i: (0, 0)
    consts = (w_in, b_in, dw_w, dw_b, ln_g, ln_b, w_out, b_out, g, beta)
    return pl.pallas_call(
        functools.partial(_conv_kernel, tm=tm, rc=rc),
        out_shape=jax.ShapeDtypeStruct((batch * seq, D_MODEL), F32),
        grid=(batch, nt),
        in_specs=[pl.BlockSpec((tm, D_MODEL), row)] + [pl.BlockSpec(c.shape, const2) for c in consts],
        out_specs=pl.BlockSpec((tm, D_MODEL), row),
        scratch_shapes=[pltpu.VMEM((HALO + tm, D_MODEL), F32), pltpu.VMEM((tm, D_MODEL), F32)],
        compiler_params=pltpu.CompilerParams(dimension_semantics=("parallel", "arbitrary"),
                                             vmem_limit_bytes=VMEM_LIMIT),
        name="conv_module_norm",
    )(x2d, *consts)


def _head_perm():
    p = np.arange(HEADS * HEAD_DIM)
    g, half, d = p // LANES, (p % LANES) // HEAD_DIM, p % HEAD_DIM
    return (g + GROUP * half) * HEAD_DIM + d


def _compress_weights(pe_k, w1_k, w2_k, pe_v, w1_v, w2_v):
    eye = jnp.eye(KV_HEADS, dtype=F32)
    w1 = jnp.stack([w1_k, w1_v]).reshape(2, CMP_BLOCK, HEAD_DIM, HEAD_DIM)
    w1x = jnp.einsum('klde,km,hn->lkhdmne', w1, eye, eye).reshape(CMP_BLOCK * 256, 256)
    half = CMP_STRIDE * 256
    pe = jnp.stack([pe_k, pe_v])
    pex = jnp.broadcast_to(pe.transpose(1, 0, 2)[:, :, None, :], (CMP_BLOCK, 2, KV_HEADS, HEAD_DIM))
    pex = pex.reshape(2, half)
    w2 = jnp.stack([w2_k, w2_v])
    w2x = jnp.einsum('kde,km,hn->khdmne', w2, eye, eye).reshape(256, 256)
    pea = jnp.broadcast_to(pex[0:1], (8, half)).astype(BF16)
    peb = jnp.broadcast_to(pex[1:2], (8, half)).astype(BF16)
    return w1x[:half].astype(BF16), w1x[half:].astype(BF16), pea, peb, w2x.astype(BF16)


def _overlap_matrix(nc):
    c0 = np.arange(nc)[None, :] * CMP_STRIDE
    s0 = np.arange(LANES)[:, None] * SEL_BLOCK
    ov = (c0 < s0 + SEL_BLOCK) & (c0 + CMP_BLOCK > s0) & (np.arange(nc)[None, :] < nc - 1)
    return jnp.asarray(ov, dtype=BF16)


def _even_layer(x2d, rel_bias, w_in, w_out, ln_g, ln_b, w_s, b_s, pe_k, w1_k, w2_k, pe_v, w1_v, w2_v,
                norm_g, norm_b, *, batch, seq):
    nc = seq // CMP_STRIDE
    assert seq % FAR_TILE == 0 and seq >= NEAR and seq // SEL_BLOCK <= LANES
    perm = _head_perm()
    qw = HEADS * HEAD_DIM
    o = 2 * A_WIDTH
    wuv = w_in[:, :o].astype(BF16)
    wq = w_in[:, o:o + qw][:, perm]
    wkv = w_in[:, o + qw:o + qw + 768]
    wgt = jnp.pad(w_in[:, o + qw + 768:], ((0, 0), (0, LANES - 3 * HEADS)))
    wqkv = jnp.concatenate([wq, wkv, wgt], axis=1).astype(BF16)
    bs = jnp.broadcast_to(b_s[:, :, None], (A_GROUPS, CHUNK, A_WIDTH // A_GROUPS))
    a_out, qs, cmp2, kaug, vaug, wink, winv, gates = _proj(
        x2d, wuv, wqkv, ln_g[None, :], ln_b[None, :], w_s, bs, seq=seq)

    kc, vct = _compress(cmp2.reshape(batch, nc, CMP_STRIDE * 256),
                    *_compress_weights(pe_k, w1_k, w2_k, pe_v, w1_v, w2_v))

    fc, dn = _bias_tables(_bucket_thresholds(seq), rel_bias.T.reshape(-1), nc)
    b_out = _nsa(qs, gates, kc, vct, kaug, vaug, wink, winv, fc, dn, _overlap_matrix(nc), batch=batch, seq=seq)

    wo_a = w_out[:A_WIDTH].astype(BF16)
    wo_b = w_out[A_WIDTH:][perm].astype(BF16)
    return _outproj(x2d, a_out, b_out, wo_a, wo_b, norm_g[None, :], norm_b[None, :])


def kernel(x, rel_bias, hyb_w_in, hyb_w_out, gmlp_ln_g, gmlp_ln_b, gmlp_w_s, gmlp_b_s, cmp_pe_k, cmp_w1_k, cmp_w2_k, cmp_pe_v, cmp_w1_v, cmp_w2_v, conv_w_in, conv_b_in, conv_dw_w, conv_dw_b, conv_ln_g, conv_ln_b, conv_w_out, conv_b_out, ffn_w_gate, ffn_w_up, ffn_w_down, norm_mix_g, norm_mix_b, norm_ffn_g, norm_ffn_b):
    batch, seq, d = x.shape
    h = x.reshape(batch * seq, d)
    for layer in range(DEPTH):
        i = layer // 2
        if layer % 2 == 0:
            h = _even_layer(h, rel_bias, hyb_w_in[i], hyb_w_out[i], gmlp_ln_g[i], gmlp_ln_b[i],
                            gmlp_w_s[i], gmlp_b_s[i], cmp_pe_k[i], cmp_w1_k[i], cmp_w2_k[i],
                            cmp_pe_v[i], cmp_w1_v[i], cmp_w2_v[i],
                            norm_mix_g[layer], norm_mix_b[layer], batch=batch, seq=seq)
        else:
            h = _conv(h, conv_w_in[i].astype(BF16), conv_b_in[i][None, :], conv_dw_w[i], conv_dw_b[i][None, :],
                      conv_ln_g[i][None, :], conv_ln_b[i][None, :], conv_w_out[i].astype(BF16),
                      conv_b_out[i][None, :], norm_mix_g[layer][None, :], norm_mix_b[layer][None, :],
                      batch=batch, seq=seq)
        h = _ffn(h, ffn_w_gate[layer].astype(BF16), ffn_w_up[layer].astype(BF16),
                 ffn_w_down[layer].astype(BF16), norm_ffn_g[layer][None, :], norm_ffn_b[layer][None, :])
    return h.reshape(batch, seq, d)
```

```python
import functools
import math

import numpy as np
import jax
import jax.numpy as jnp
from jax import lax
from jax.experimental import pallas as pl
from jax.experimental.pallas import tpu as pltpu

F32 = jnp.float32
BF16 = jnp.bfloat16

D_MODEL = 1024
DEPTH = 2
ALPHA = (2 * DEPTH) ** 0.25
CHUNK = 128
A_WIDTH = D_MODEL // 2
A_GROUPS = 4
HEADS = 8
KV_HEADS = 2
GROUP = HEADS // KV_HEADS
HEAD_DIM = (D_MODEL // 2) // HEADS
CMP_STRIDE = 16
CMP_BLOCK = 32
SEL_BLOCK = 64
N_SELECT = 16
WINDOW = 512
Q_BLOCK = 128
N_BUCKETS = 32
MAX_DISTANCE = 1024
CONV_WIDTH = 31
NEG_INF = -1e30
FORCE = 1e9
LN_EPS = 1e-5

LANES = 128
VMEM_LIMIT = 56 * 1024 * 1024

ROWS = GROUP * Q_BLOCK
FAR_TILE = 512
NEAR_TILES = 8
NEAR = NEAR_TILES * Q_BLOCK
NEAR_BACK = NEAR - Q_BLOCK
WIN_KEYS = WINDOW + Q_BLOCK
DN_WIDTH = NEAR + NEAR_BACK
HALO = 32


def _dot(a, b):
    return jnp.dot(a, b, preferred_element_type=F32)


def _dot_nt(a, b):
    return lax.dot_general(a, b, (((1,), (1,)), ((), ())), preferred_element_type=F32)


def _layer_norm(z, g, b):
    mu = jnp.mean(z, axis=-1, keepdims=True)
    d = z - mu
    var = jnp.mean(d * d, axis=-1, keepdims=True)
    return d * lax.rsqrt(var + LN_EPS) * g + b


def _t5_bucket(dist):
    n = jnp.maximum(dist, 0)
    max_exact = N_BUCKETS // 2
    nf = jnp.maximum(n, 1).astype(jnp.float32)
    large = max_exact + (jnp.log(nf / max_exact) / math.log(MAX_DISTANCE / max_exact)
                         * (N_BUCKETS - max_exact)).astype(jnp.int32)
    large = jnp.minimum(large, N_BUCKETS - 1)
    return jnp.where(n < max_exact, n, large)


def _bucket_thresholds(seq):
    b = _t5_bucket(jnp.arange(seq, dtype=jnp.int32))
    j = jnp.arange(N_BUCKETS, dtype=jnp.int32)
    return jnp.sum((b[None, :] < j[:, None]).astype(jnp.int32), axis=1).astype(jnp.int32)


def _tables_kernel(thr_ref, rb_ref, fc_ref, dn_ref, *, nc):
    h = pl.program_id(0)
    base = h * N_BUCKETS
    far = rb_ref[base + N_BUCKETS - 1]

    def bias_of(dist):
        val = jnp.full(dist.shape, far, F32)
        for j in range(N_BUCKETS - 1, 0, -1):
            val = jnp.where(dist < thr_ref[j], rb_ref[base + j - 1], val)
        return val

    ql = lax.broadcasted_iota(jnp.int32, (Q_BLOCK, LANES), 0)
    ln = lax.broadcasted_iota(jnp.int32, (Q_BLOCK, LANES), 1)
    for c in range(2 * nc // Q_BLOCK):
        dist = ln - CMP_STRIDE * (ql + (c * Q_BLOCK - (nc - 8))) - (CMP_BLOCK - 1)
        fc_ref[0, c * Q_BLOCK:(c + 1) * Q_BLOCK, :] = jnp.where(dist >= 0, bias_of(dist), NEG_INF)
    for c in range(DN_WIDTH // LANES):
        dist = ql + (NEAR_BACK - c * LANES) - ln
        dn_ref[0, :, c * LANES:(c + 1) * LANES] = jnp.where(dist >= 0, bias_of(dist) - far, NEG_INF)


def _bias_tables(thr, rb_flat, nc):
    return pl.pallas_call(
        functools.partial(_tables_kernel, nc=nc),
        out_shape=(jax.ShapeDtypeStruct((HEADS, 2 * nc, Q_BLOCK), F32),
                   jax.ShapeDtypeStruct((HEADS, Q_BLOCK, DN_WIDTH), F32)),
        grid=(HEADS,),
        in_specs=[pl.BlockSpec(memory_space=pltpu.SMEM), pl.BlockSpec(memory_space=pltpu.SMEM)],
        out_specs=(pl.BlockSpec((1, 2 * nc, Q_BLOCK), lambda h: (h, 0, 0)),
                   pl.BlockSpec((1, Q_BLOCK, DN_WIDTH), lambda h: (h, 0, 0))),
        name="bias_tables",
    )(thr, rb_flat)


def _proj_kernel(x_ref, wuv_ref, wqkv_ref, lng_ref, lnb_ref, ws_ref, bs_ref,
                 a_ref, q_ref, cmp_ref, kaug_ref, vaug_ref, wink_ref, winv_ref, gate_ref, cmp_sc, *, tm, seq):
    xb = x_ref[...].astype(BF16)
    uv = jax.nn.gelu(_dot(xb, wuv_ref[...]))
    u = uv[:, :A_WIDTH]
    v = _layer_norm(uv[:, A_WIDTH:], lng_ref[...], lnb_ref[...]).astype(BF16)
    row = lax.broadcasted_iota(jnp.int32, (CHUNK, CHUNK), 0)
    col = lax.broadcasted_iota(jnp.int32, (CHUNK, CHUNK), 1)
    gd = A_WIDTH // A_GROUPS
    for g in range(A_GROUPS):
        w = jnp.where(col <= row, ws_ref[g], 0.0).astype(BF16)
        for c in range(tm // CHUNK):
            rs = slice(c * CHUNK, (c + 1) * CHUNK)
            cs = slice(g * gd, (g + 1) * gd)
            s = _dot(w, v[rs, cs]) + bs_ref[g]
            a_ref[rs, cs] = (u[rs, cs] * s).astype(BF16)

    h = _dot(xb, wqkv_ref[...])
    q_ref[...] = (h[:, 0:512] * (HEAD_DIM ** -0.5)).astype(BF16)
    for half in range(2):
        cmp_sc[half] = h[:, 512 + half * LANES:512 + (half + 1) * LANES]
        for l in range(CMP_STRIDE):
            rows = cmp_sc[half, pl.ds(l, tm // CMP_STRIDE, stride=CMP_STRIDE), :]
            cmp_ref[:, l * 256 + half * LANES:l * 256 + (half + 1) * LANES] = rows.astype(BF16)
    kaug_ref[:, 0:LANES] = h[:, 768:896].astype(BF16)
    pos = (pl.program_id(0) * tm) % seq + lax.broadcasted_iota(jnp.int32, (tm, LANES), 0)
    blk = lax.broadcasted_iota(jnp.int32, (tm, LANES), 1)
    kaug_ref[:, LANES:2 * LANES] = jnp.where(pos // SEL_BLOCK == blk, 1.0, 0.0).astype(BF16)
    ones = jnp.ones((tm, LANES), BF16)
    vaug_ref[:, 0:LANES] = h[:, 896:1024].astype(BF16)
    vaug_ref[:, LANES:2 * LANES] = ones
    wink_ref[...] = h[:, 1024:1152].astype(BF16)
    winv_ref[:, 0:LANES] = h[:, 1152:1280].astype(BF16)
    winv_ref[:, LANES:2 * LANES] = ones
    gate_ref[...] = jax.nn.sigmoid(h[:, 1280:1408])


def _proj(x2d, wuv, wqkv, lng, lnb, ws, bs, *, seq, tm=512):
    t = x2d.shape[0]
    row = lambda i: (i, 0)
    const2 = lambda i: (0, 0)
    const3 = lambda i: (0, 0, 0)
    outs = [(1, A_WIDTH, BF16), (1, 512, BF16), (CMP_STRIDE, CMP_STRIDE * 256, BF16), (1, 256, BF16),
            (1, 256, BF16), (1, 128, BF16), (1, 256, BF16), (1, 128, F32)]
    return pl.pallas_call(
        functools.partial(_proj_kernel, tm=tm, seq=seq),
        out_shape=tuple(jax.ShapeDtypeStruct((t // r, w), dt) for r, w, dt in outs),
        grid=(t // tm,),
        in_specs=[pl.BlockSpec((tm, D_MODEL), row),
                  pl.BlockSpec(wuv.shape, const2),
                  pl.BlockSpec(wqkv.shape, const2),
                  pl.BlockSpec(lng.shape, const2),
                  pl.BlockSpec(lnb.shape, const2),
                  pl.BlockSpec(ws.shape, const3),
                  pl.BlockSpec(bs.shape, const3)],
        out_specs=tuple(pl.BlockSpec((tm // r, w), row) for r, w, _ in outs),
        scratch_shapes=[pltpu.VMEM((2, tm, LANES), F32)],
        compiler_params=pltpu.CompilerParams(dimension_semantics=("parallel",), vmem_limit_bytes=VMEM_LIMIT),
        name="in_proj_gmlp",
    )(x2d, wuv, wqkv, lng, lnb, ws, bs)


def _compress_kernel(r_ref, wa_ref, wb_ref, pea_ref, peb_ref, w2_ref, kc_ref, vct_ref):
    r = r_ref[0]
    top = _dot(r, wa_ref[...])
    bot = _dot(r, wb_ref[...])
    pe = _dot(pea_ref[...], wa_ref[...]) + _dot(peb_ref[...], wb_ref[...])
    hid = top + pltpu.roll(bot, bot.shape[0] - 1, 0) + pe[0:1]
    out = _dot(jax.nn.gelu(hid).astype(BF16), w2_ref[...])
    kc_ref[0] = out[:, 0:LANES].astype(BF16)
    vct_ref[0] = out[:, LANES:2 * LANES].T.astype(BF16)


def _compress(cmp3, wa, wb, pea, peb, w2):
    b, nc, width = cmp3.shape
    const2 = lambda i: (0, 0)
    return pl.pallas_call(
        _compress_kernel,
        out_shape=(jax.ShapeDtypeStruct((b, nc, LANES), BF16), jax.ShapeDtypeStruct((b, LANES, nc), BF16)),
        grid=(b,),
        in_specs=[pl.BlockSpec((1, nc, width), lambda i: (i, 0, 0)),
                  pl.BlockSpec(wa.shape, const2), pl.BlockSpec(wb.shape, const2),
                  pl.BlockSpec(pea.shape, const2), pl.BlockSpec(peb.shape, const2),
                  pl.BlockSpec(w2.shape, const2)],
        out_specs=(pl.BlockSpec((1, nc, LANES), lambda i: (i, 0, 0)),
                   pl.BlockSpec((1, LANES, nc), lambda i: (i, 0, 0))),
        compiler_params=pltpu.CompilerParams(dimension_semantics=("parallel",), vmem_limit_bytes=VMEM_LIMIT),
        name="kv_compress",
    )(cmp3, wa, wb, pea, peb, w2)


def _softmax_tile(s, m_sc):
    cols = [s[:, j * LANES:(j + 1) * LANES] for j in range(s.shape[1] // LANES)]
    m_old = m_sc[...]
    m_new = jnp.maximum(m_old, jnp.max(functools.reduce(jnp.maximum, cols), axis=-1, keepdims=True))
    m_sc[...] = m_new
    p = jnp.concatenate([jnp.exp(c - m_new) for c in cols], axis=1).astype(BF16)
    return p, jnp.exp(m_old - m_new)


def _scale_both(acc, alpha):
    return jnp.concatenate([acc[:, :LANES] * alpha, acc[:, LANES:] * alpha], axis=1)


def _nsa_kernel(q_ref, gate_ref, kc_ref, vct_ref, kaug_ref, vaug_ref, wink_ref, winv_ref, fc_ref, dn_ref, ovt_ref,
                o_ref, qa_all, s_all, p_all, m_all, acc_all, *, nc, n_tiles):
    ib = pl.program_id(1)
    t0 = ib * Q_BLOCK
    lane = lax.broadcasted_iota(jnp.int32, (Q_BLOCK, LANES), 1)
    qrow = lax.broadcasted_iota(jnp.int32, (Q_BLOCK, LANES), 0)
    q = q_ref[...]
    gates = gate_ref[...]
    kc = kc_ref[0]
    vct = vct_ref[0]
    ovt = ovt_ref[...]

    def stack_heads(fn):
        return jnp.concatenate([fn(g) for g in range(GROUP)], axis=0)

    near0 = 2 * ib - NEAR_BACK // SEL_BLOCK
    n_far = jnp.maximum(ib - 4, 0) // 4
    n_pairs = (n_far + 1) // 2
    kstart = jnp.maximum(t0 - NEAR_BACK, 0)
    j0 = pl.multiple_of(kstart - (t0 - NEAR_BACK), LANES)
    kstart = pl.multiple_of(kstart, LANES)
    wstart = jnp.maximum(t0 - WINDOW, 0)
    jw = pl.multiple_of(wstart - (t0 - NEAR_BACK), LANES)
    wstart = pl.multiple_of(wstart, LANES)
    win_edge = jnp.where((lane > qrow) | (t0 < WINDOW), 0.0, NEG_INF)

    fc_row = pl.multiple_of((nc - 8) - 8 * ib, 8)
    qcol = lax.broadcasted_iota(jnp.int32, (1, ROWS), 1) % Q_BLOCK
    has_cmp = jnp.where(t0 + qcol >= CMP_BLOCK - 1, 1.0, 0.0)
    qhs, o_cts, imps = [], [], []
    for hkv in range(KV_HEADS):
        mine = (lane >= HEAD_DIM) if hkv else (lane < HEAD_DIM)
        qh = stack_heads(lambda g: jnp.where(mine, q[:, g * LANES:(g + 1) * LANES], 0.0).astype(BF16))
        lct = _dot_nt(kc, qh)
        lct = lct + jnp.concatenate([fc_ref[hkv * GROUP + g, pl.ds(fc_row, nc), :] for g in range(GROUP)], axis=1)
        e = jnp.exp(lct - jnp.max(lct, axis=0, keepdims=True))
        pct = e * (has_cmp / jnp.sum(e, axis=0, keepdims=True))
        o_cts.append(_dot(vct, pct.astype(BF16)))
        psum = pct[:, 0:LANES] + pct[:, LANES:2 * LANES] + pct[:, 2 * LANES:3 * LANES] + pct[:, 3 * LANES:]
        p_hi = psum.astype(BF16)
        p_lo = (psum - p_hi.astype(F32)).astype(BF16)
        imps.append(_dot(ovt, p_hi) + _dot(ovt, p_lo))
        qhs.append(qh)

    blk = lax.broadcasted_iota(jnp.int32, (LANES, KV_HEADS * Q_BLOCK), 0)
    col = lax.broadcasted_iota(jnp.int32, (LANES, KV_HEADS * Q_BLOCK), 1)
    jq = 2 * ib + jnp.where(col % Q_BLOCK >= SEL_BLOCK, 1, 0)
    forced = (blk == 0) | (blk == jq) | (blk == jq - 1)
    cand = jnp.where(forced, -3e38, jnp.where(blk > jq, NEG_INF, jnp.concatenate(imps, axis=1)))
    blk_f = blk.astype(F32)

    def pick_one(_, carry):
        cur, chosen = carry
        mx = jnp.max(cur, axis=0, keepdims=True)
        first = jnp.min(jnp.where(cur == mx, blk_f, float(LANES)), axis=0, keepdims=True)
        pick = blk_f == first
        return jnp.where(pick, -3e38, cur), jnp.where(pick, 1.0, chosen)

    _, sel_t = lax.fori_loop(0, N_SELECT - 3, pick_one, (cand, jnp.where(forced, 1.0, 0.0)))

    outs = []
    for hkv in range(KV_HEADS):
        qa_sc, s_buf, p_buf = qa_all.at[hkv], s_all.at[hkv], p_all.at[hkv]
        m_sc, acc_sc = m_all.at[hkv], acc_all.at[hkv]
        qh = qhs[hkv]
        sel = sel_t[:, hkv * Q_BLOCK:(hkv + 1) * Q_BLOCK].T

        m_near = jnp.where(sel > 0, 0.0, NEG_INF).astype(BF16)
        m_far = jnp.where((sel > 0) & (lane < near0), 0.0, NEG_INF).astype(BF16)
        qa_sc[0, :, 0:LANES] = qh
        qa_sc[0, :, LANES:2 * LANES] = jnp.concatenate([m_far] * GROUP, axis=0)
        qa_sc[1, :, 0:LANES] = qh
        qa_sc[1, :, LANES:2 * LANES] = jnp.concatenate([m_near] * GROUP, axis=0)
        m_sc[...] = jnp.full(m_sc.shape, -jnp.inf, F32)
        acc_sc[...] = jnp.zeros(acc_sc.shape, F32)
        p_buf[1] = jnp.zeros((ROWS, FAR_TILE), BF16)

        def tile_keys(tile):
            near_idx = tile - 2 * n_pairs
            ks = jnp.where(near_idx >= 0, kstart + near_idx * FAR_TILE, tile * FAR_TILE)
            ks = jnp.clip(ks, 0, (n_tiles - 1) * FAR_TILE)
            return pl.multiple_of(ks, LANES), jnp.where(near_idx >= 0, 1, 0)

        def logits(tile, slot):
            ks, variant = tile_keys(tile)
            s_buf[slot] = _dot_nt(qa_sc[variant], kaug_ref[pl.ds(ks, FAR_TILE), :])

        def values(tile, slot):
            ks, _ = tile_keys(tile)
            return _dot(p_buf[slot], vaug_ref[pl.ds(ks, FAR_TILE), :])

        def absorb(slot, pending, table_col=None):
            s = s_buf[slot]
            if table_col is not None:
                js = pl.multiple_of(table_col, LANES)
                s = s + stack_heads(lambda g: dn_ref[hkv * GROUP + g, :, pl.ds(js, FAR_TILE)])
            p, alpha = _softmax_tile(s, m_sc)
            p_buf[slot] = p
            acc_sc[...] = _scale_both(acc_sc[...] + pending, alpha)

        logits(0, 0)

        sw = _dot_nt(qh, wink_ref[pl.ds(wstart, WIN_KEYS), :])
        sw = sw + stack_heads(lambda g: dn_ref[hkv * GROUP + g, :, pl.ds(jw, WIN_KEYS)])
        sw = jnp.concatenate([sw[:, :LANES] + jnp.concatenate([win_edge] * GROUP, axis=0), sw[:, LANES:]], axis=1)
        pw = jnp.exp(sw - jnp.max(sw, axis=-1, keepdims=True)).astype(BF16)
        ow = _dot(pw, winv_ref[pl.ds(wstart, WIN_KEYS), :])
        o_w = ow[:, :LANES] / ow[:, LANES:]

        def far_pair(i, carry):
            t = 2 * i
            logits(t + 1, 1)
            absorb(0, values(t - 1, 1))
            logits(t + 2, 0)
            absorb(1, values(t, 0))
            return carry

        lax.fori_loop(0, n_pairs, far_pair, 0)
        t = 2 * n_pairs
        logits(t + 1, 1)
        absorb(0, values(t - 1, 1), j0)
        absorb(1, values(t, 0), j0 + FAR_TILE)
        acc = acc_sc[...] + values(t + 1, 1)
        o_s = acc[:, :LANES] / acc[:, LANES:]

        per_head = []
        for g in range(GROUP):
            c = 3 * (hkv * GROUP + g)
            rs = slice(g * Q_BLOCK, (g + 1) * Q_BLOCK)
            o_c = o_cts[hkv][:, rs].T
            per_head.append(gates[:, c:c + 1] * o_c + gates[:, c + 1:c + 2] * o_s[rs]
                            + gates[:, c + 2:c + 3] * o_w[rs])
        outs.append(per_head)

    for g in range(GROUP):
        o_ref[:, g * LANES:(g + 1) * LANES] = jnp.where(lane < HEAD_DIM, outs[0][g], outs[1][g]).astype(BF16)


def _nsa(qs, gates, kc, vct, kaug, vaug, wink, winv, fc, dn, ovt, *, batch, seq):
    nc = seq // CMP_STRIDE
    nq = seq // Q_BLOCK
    qrow = lambda b, i: (b * nq + i, 0)
    per_batch = lambda b, i: (b, 0)
    const2 = lambda b, i: (0, 0)
    const3 = lambda b, i: (0, 0, 0)
    once = pl.Buffered(1)
    return pl.pallas_call(
        functools.partial(_nsa_kernel, nc=nc, n_tiles=seq // FAR_TILE),
        out_shape=jax.ShapeDtypeStruct((batch * seq, 4 * LANES), BF16),
        grid=(batch, nq),
        in_specs=[pl.BlockSpec((Q_BLOCK, 4 * LANES), qrow),
                  pl.BlockSpec((Q_BLOCK, LANES), qrow),
                  pl.BlockSpec((1, nc, LANES), lambda b, i: (b, 0, 0)),
                  pl.BlockSpec((1, LANES, nc), lambda b, i: (b, 0, 0)),
                  pl.BlockSpec((seq, 256), per_batch, pipeline_mode=once),
                  pl.BlockSpec((seq, 256), per_batch, pipeline_mode=once),
                  pl.BlockSpec((seq, LANES), per_batch, pipeline_mode=once),
                  pl.BlockSpec((seq, 256), per_batch, pipeline_mode=once),
                  pl.BlockSpec(fc.shape, const3, pipeline_mode=once),
                  pl.BlockSpec(dn.shape, const3, pipeline_mode=once),
                  pl.BlockSpec(ovt.shape, const2)],
        out_specs=pl.BlockSpec((Q_BLOCK, 4 * LANES), qrow),
        scratch_shapes=[pltpu.VMEM((KV_HEADS, 2, ROWS, 2 * LANES), BF16),
                        pltpu.VMEM((KV_HEADS, 2, ROWS, FAR_TILE), F32),
                        pltpu.VMEM((KV_HEADS, 2, ROWS, FAR_TILE), BF16),
                        pltpu.VMEM((KV_HEADS, ROWS, LANES), F32),
                        pltpu.VMEM((KV_HEADS, ROWS, 2 * LANES), F32)],
        compiler_params=pltpu.CompilerParams(dimension_semantics=("parallel", "arbitrary"),
                                             vmem_limit_bytes=VMEM_LIMIT),
        name="sparse_attention",
    )(qs, gates, kc, vct, kaug, vaug, wink, winv, fc, dn, ovt)


def _swiglu_norm(x, wg_ref, wu_ref, wd_ref, g_ref, beta_ref, acc_ref, hc):
    xb = x.astype(BF16)
    hidden = wg_ref.shape[1]
    for c in range(hidden // hc):
        cs = slice(c * hc, (c + 1) * hc)
        gate = _dot(xb, wg_ref[:, cs])
        up = _dot(xb, wu_ref[:, cs])
        part = _dot((jax.nn.silu(gate) * up).astype(BF16), wd_ref[cs, :])
        if c == 0:
            acc_ref[...] = part
        else:
            acc_ref[...] += part
    return _layer_norm(ALPHA * x + acc_ref[...], g_ref[...], beta_ref[...])


def _outproj_ffn_kernel(x_ref, a_ref, b_ref, wa_ref, wb_ref, g1_ref, beta1_ref,
                        wg_ref, wu_ref, wd_ref, g2_ref, beta2_ref, o_ref, acc_ref, *, hc):
    y = _dot(a_ref[...], wa_ref[...]) + _dot(b_ref[...], wb_ref[...])
    x1 = _layer_norm(ALPHA * x_ref[...] + y, g1_ref[...], beta1_ref[...])
    o_ref[...] = _swiglu_norm(x1, wg_ref, wu_ref, wd_ref, g2_ref, beta2_ref, acc_ref, hc)


def _outproj_ffn(x2d, a, b, wa, wb, g1, beta1, wg, wu, wd, g2, beta2, *, tm=512, hc=256):
    t = x2d.shape[0]
    row = lambda i: (i, 0)
    const2 = lambda i: (0, 0)
    consts = (wa, wb, g1, beta1, wg, wu, wd, g2, beta2)
    return pl.pallas_call(
        functools.partial(_outproj_ffn_kernel, hc=hc),
        out_shape=jax.ShapeDtypeStruct((t, D_MODEL), F32),
        grid=(t // tm,),
        in_specs=[pl.BlockSpec((tm, D_MODEL), row), pl.BlockSpec((tm, a.shape[1]), row),
                  pl.BlockSpec((tm, b.shape[1]), row)]
        + [pl.BlockSpec(c.shape, const2, pipeline_mode=pl.Buffered(1)) for c in consts],
        out_specs=pl.BlockSpec((tm, D_MODEL), row),
        scratch_shapes=[pltpu.VMEM((tm, D_MODEL), F32)],
        compiler_params=pltpu.CompilerParams(dimension_semantics=("parallel",), vmem_limit_bytes=VMEM_LIMIT),
        name="out_proj_swiglu_ffn_norm",
    )(x2d, a, b, *consts)


def _conv_kernel(x_ref, win_ref, bin_ref, dww_ref, dwb_ref, lng_ref, lnb_ref, wout_ref, bout_ref,
                 g_ref, beta_ref, wg_ref, wu_ref, wd_ref, g2_ref, beta2_ref, o_ref, buf_ref, cv_ref, *, tm, rc, hc):
    @pl.when(pl.program_id(1) == 0)
    def _():
        buf_ref[0:HALO, :] = jnp.zeros((HALO, D_MODEL), F32)

    x = x_ref[...]
    h = _dot(x.astype(BF16), win_ref[...]) + bin_ref[...]
    buf_ref[HALO:HALO + tm, :] = h[:, :D_MODEL] * jax.nn.sigmoid(h[:, D_MODEL:])

    lead = HALO - (CONV_WIDTH - 1)

    def conv_rows(r, carry):
        r0 = pl.multiple_of(r * rc, rc)
        for lc in range(D_MODEL // LANES):
            ls = slice(lc * LANES, (lc + 1) * LANES)
            acc = jnp.broadcast_to(dwb_ref[:, ls], (rc, LANES))
            za = buf_ref[pl.ds(r0, rc + HALO), ls]
            for b in range(8):
                base, shift = 8 * ((lead + b) // 8), (lead + b) % 8
                span = rc + (8 if shift else 0)
                yb = None
                for a, j in enumerate(range(b, CONV_WIDTH, 8)):
                    term = za[base + 8 * a:base + 8 * a + span] * dww_ref[j:j + 1, ls]
                    yb = term if yb is None else yb + term
                acc = acc + yb[shift:shift + rc]
            cv_ref[pl.ds(r0, rc), ls] = acc
        return carry

    lax.fori_loop(0, tm // rc, conv_rows, 0)
    buf_ref[0:HALO, :] = buf_ref[tm:tm + HALO, :]

    y = jax.nn.silu(_layer_norm(cv_ref[...], lng_ref[...], lnb_ref[...]))
    y = _dot(y.astype(BF16), wout_ref[...]) + bout_ref[...]
    x1 = _layer_norm(ALPHA * x + y, g_ref[...], beta_ref[...])
    o_ref[...] = _swiglu_norm(x1, wg_ref, wu_ref, wd_ref, g2_ref, beta2_ref, cv_ref, hc)


def _conv_ffn(x2d, w_in, b_in, dw_w, dw_b, ln_g, ln_b, w_out, b_out, g, beta, wg, wu, wd, g2, beta2, *,
              batch, seq, tm=512, rc=64, hc=256):
    nt = seq // tm
    row = lambda b, i: (b * nt + i, 0)
    const2 = lambda b, i: (0, 0)
    consts = (w_in, b_in, dw_w, dw_b, ln_g, ln_b, w_out, b_out, g, beta, wg, wu, wd, g2, beta2)
    return pl.pallas_call(
        functools.partial(_conv_kernel, tm=tm, rc=rc, hc=hc),
        out_shape=jax.ShapeDtypeStruct((batch * seq, D_MODEL), F32),
        grid=(batch, nt),
        in_specs=[pl.BlockSpec((tm, D_MODEL), row)]
        + [pl.BlockSpec(c.shape, const2, pipeline_mode=pl.Buffered(1)) for c in consts],
        out_specs=pl.BlockSpec((tm, D_MODEL), row),
        scratch_shapes=[pltpu.VMEM((HALO + tm, D_MODEL), F32), pltpu.VMEM((tm, D_MODEL), F32)],
        compiler_params=pltpu.CompilerParams(dimension_semantics=("parallel", "arbitrary"),
                                             vmem_limit_bytes=VMEM_LIMIT),
        name="conv_module_swiglu_ffn_norm",
    )(x2d, *consts)


def _head_perm():
    p = np.arange(HEADS * HEAD_DIM)
    g, half, d = p // LANES, (p % LANES) // HEAD_DIM, p % HEAD_DIM
    return (g + GROUP * half) * HEAD_DIM + d


def _compress_weights(pe_k, w1_k, w2_k, pe_v, w1_v, w2_v):
    def block_diag(compact):
        row_c = (jnp.arange(compact.shape[0]) % 256) // HEAD_DIM
        col_c = jnp.arange(256) // HEAD_DIM
        return jnp.where(row_c[:, None] == col_c[None, :], jnp.tile(compact, (1, 4)), 0.0).astype(BF16)

    w1k = w1_k.reshape(CMP_BLOCK, 1, HEAD_DIM, HEAD_DIM)
    w1v = w1_v.reshape(CMP_BLOCK, 1, HEAD_DIM, HEAD_DIM)
    w1x = block_diag(jnp.concatenate([w1k, w1k, w1v, w1v], axis=1).reshape(CMP_BLOCK * 256, HEAD_DIM))
    w2x = block_diag(jnp.concatenate([w2_k, w2_k, w2_v, w2_v], axis=0))
    half = CMP_STRIDE * 256
    pe = jnp.stack([pe_k, pe_v])
    pex = jnp.broadcast_to(pe.transpose(1, 0, 2)[:, :, None, :], (CMP_BLOCK, 2, KV_HEADS, HEAD_DIM))
    pex = pex.reshape(2, half)
    pea = jnp.broadcast_to(pex[0:1], (8, half)).astype(BF16)
    peb = jnp.broadcast_to(pex[1:2], (8, half)).astype(BF16)
    return w1x[:half], w1x[half:], pea, peb, w2x


def _overlap_matrix(nc):
    c0 = np.arange(nc)[None, :] * CMP_STRIDE
    s0 = np.arange(LANES)[:, None] * SEL_BLOCK
    ov = (c0 < s0 + SEL_BLOCK) & (c0 + CMP_BLOCK > s0) & (np.arange(nc)[None, :] < nc - 1)
    return jnp.asarray(ov, dtype=BF16)


def _even_layer(x2d, rel_bias, w_in, w_out, ln_g, ln_b, w_s, b_s, pe_k, w1_k, w2_k, pe_v, w1_v, w2_v,
                norm_g, norm_b, *, batch, seq):
    nc = seq // CMP_STRIDE
    assert seq % FAR_TILE == 0 and seq >= NEAR and seq // SEL_BLOCK <= LANES
    perm = _head_perm()
    qw = HEADS * HEAD_DIM
    o = 2 * A_WIDTH
    wuv = w_in[:, :o].astype(BF16)
    wq = w_in[:, o:o + qw][:, perm]
    wkv = w_in[:, o + qw:o + qw + 768]
    wgt = jnp.pad(w_in[:, o + qw + 768:], ((0, 0), (0, LANES - 3 * HEADS)))
    wqkv = jnp.concatenate([wq, wkv, wgt], axis=1).astype(BF16)
    bs = jnp.broadcast_to(b_s[:, :, None], (A_GROUPS, CHUNK, A_WIDTH // A_GROUPS))
    a_out, qs, cmp2, kaug, vaug, wink, winv, gates = _proj(
        x2d, wuv, wqkv, ln_g[None, :], ln_b[None, :], w_s, bs, seq=seq)

    kc, vct = _compress(cmp2.reshape(batch, nc, CMP_STRIDE * 256),
                    *_compress_weights(pe_k, w1_k, w2_k, pe_v, w1_v, w2_v))

    fc, dn = _bias_tables(_bucket_thresholds(seq), rel_bias.T.reshape(-1), nc)
    b_out = _nsa(qs, gates, kc, vct, kaug, vaug, wink, winv, fc, dn, _overlap_matrix(nc), batch=batch, seq=seq)

    wo_a = w_out[:A_WIDTH].astype(BF16)
    wo_b = w_out[A_WIDTH:][perm].astype(BF16)
    return a_out, b_out, wo_a, wo_b, norm_g[None, :], norm_b[None, :]


def kernel(x, rel_bias, hyb_w_in, hyb_w_out, gmlp_ln_g, gmlp_ln_b, gmlp_w_s, gmlp_b_s, cmp_pe_k, cmp_w1_k, cmp_w2_k, cmp_pe_v, cmp_w1_v, cmp_w2_v, conv_w_in, conv_b_in, conv_dw_w, conv_dw_b, conv_ln_g, conv_ln_b, conv_w_out, conv_b_out, ffn_w_gate, ffn_w_up, ffn_w_down, norm_mix_g, norm_mix_b, norm_ffn_g, norm_ffn_b):
    batch, seq, d = x.shape
    h = x.reshape(batch * seq, d)
    for layer in range(DEPTH):
        i = layer // 2
        ffn = (ffn_w_gate[layer].astype(BF16), ffn_w_up[layer].astype(BF16), ffn_w_down[layer].astype(BF16),
               norm_ffn_g[layer][None, :], norm_ffn_b[layer][None, :])
        if layer % 2 == 0:
            mixer = _even_layer(h, rel_bias, hyb_w_in[i], hyb_w_out[i], gmlp_ln_g[i], gmlp_ln_b[i],
                                gmlp_w_s[i], gmlp_b_s[i], cmp_pe_k[i], cmp_w1_k[i], cmp_w2_k[i],
                                cmp_pe_v[i], cmp_w1_v[i], cmp_w2_v[i],
                                norm_mix_g[layer], norm_mix_b[layer], batch=batch, seq=seq)
            h = _outproj_ffn(h, *mixer, *ffn)
        else:
            h = _conv_ffn(h, conv_w_in[i].astype(BF16), conv_b_in[i][None, :], conv_dw_w[i], conv_dw_b[i][None, :],
                          conv_ln_g[i][None, :], conv_ln_b[i][None, :], conv_w_out[i].astype(BF16),
                          conv_b_out[i][None, :], norm_mix_g[layer][None, :], norm_mix_b[layer][None, :],
                          *ffn, batch=batch, seq=seq)
    return h.reshape(batch, seq, d)
```

```python
import functools
import math

import numpy as np
import jax
import jax.numpy as jnp
from jax import lax
from jax.experimental import pallas as pl
from jax.experimental.pallas import tpu as pltpu

F32 = jnp.float32
BF16 = jnp.bfloat16

D_MODEL = 1024
DEPTH = 2
ALPHA = (2 * DEPTH) ** 0.25
CHUNK = 128
A_WIDTH = D_MODEL // 2
A_GROUPS = 4
HEADS = 8
KV_HEADS = 2
GROUP = HEADS // KV_HEADS
HEAD_DIM = (D_MODEL // 2) // HEADS
CMP_STRIDE = 16
CMP_BLOCK = 32
SEL_BLOCK = 64
N_SELECT = 16
WINDOW = 512
Q_BLOCK = 128
N_BUCKETS = 32
MAX_DISTANCE = 1024
CONV_WIDTH = 31
NEG_INF = -1e30
FORCE = 1e9
LN_EPS = 1e-5

LANES = 128
VMEM_LIMIT = 56 * 1024 * 1024

ROWS = GROUP * Q_BLOCK
FAR_TILE = 512
NEAR_TILES = 8
NEAR = NEAR_TILES * Q_BLOCK
NEAR_BACK = NEAR - Q_BLOCK
WIN_KEYS = WINDOW + Q_BLOCK
DN_WIDTH = NEAR + NEAR_BACK
HALO = 32


def _dot(a, b):
    return jnp.dot(a, b, preferred_element_type=F32)


def _dot_nt(a, b):
    return lax.dot_general(a, b, (((1,), (1,)), ((), ())), preferred_element_type=F32)


def _layer_norm(z, g, b):
    mu = jnp.mean(z, axis=-1, keepdims=True)
    d = z - mu
    var = jnp.mean(d * d, axis=-1, keepdims=True)
    return d * lax.rsqrt(var + LN_EPS) * g + b


def _t5_bucket(dist):
    n = jnp.maximum(dist, 0)
    max_exact = N_BUCKETS // 2
    nf = jnp.maximum(n, 1).astype(jnp.float32)
    large = max_exact + (jnp.log(nf / max_exact) / math.log(MAX_DISTANCE / max_exact)
                         * (N_BUCKETS - max_exact)).astype(jnp.int32)
    large = jnp.minimum(large, N_BUCKETS - 1)
    return jnp.where(n < max_exact, n, large)


def _bucket_thresholds(seq):
    b = _t5_bucket(jnp.arange(seq, dtype=jnp.int32))
    j = jnp.arange(N_BUCKETS, dtype=jnp.int32)
    return jnp.sum((b[None, :] < j[:, None]).astype(jnp.int32), axis=1).astype(jnp.int32)


def _tables_kernel(thr_ref, rb_ref, fc_ref, dn_ref, *, nc):
    h = pl.program_id(0)
    base = h * N_BUCKETS
    far = rb_ref[base + N_BUCKETS - 1]

    def bias_of(dist):
        val = jnp.full(dist.shape, far, F32)
        for j in range(N_BUCKETS - 1, 0, -1):
            val = jnp.where(dist < thr_ref[j], rb_ref[base + j - 1], val)
        return val

    ql = lax.broadcasted_iota(jnp.int32, (Q_BLOCK, LANES), 0)
    ln = lax.broadcasted_iota(jnp.int32, (Q_BLOCK, LANES), 1)
    for c in range(2 * nc // Q_BLOCK):
        dist = ln - CMP_STRIDE * (ql + (c * Q_BLOCK - (nc - 8))) - (CMP_BLOCK - 1)
        fc_ref[0, c * Q_BLOCK:(c + 1) * Q_BLOCK, :] = jnp.where(dist >= 0, bias_of(dist), NEG_INF)
    for c in range(DN_WIDTH // LANES):
        dist = ql + (NEAR_BACK - c * LANES) - ln
        dn_ref[0, :, c * LANES:(c + 1) * LANES] = jnp.where(dist >= 0, bias_of(dist) - far, NEG_INF)


def _bias_tables(thr, rb_flat, nc):
    return pl.pallas_call(
        functools.partial(_tables_kernel, nc=nc),
        out_shape=(jax.ShapeDtypeStruct((HEADS, 2 * nc, Q_BLOCK), F32),
                   jax.ShapeDtypeStruct((HEADS, Q_BLOCK, DN_WIDTH), F32)),
        grid=(HEADS,),
        in_specs=[pl.BlockSpec(memory_space=pltpu.SMEM), pl.BlockSpec(memory_space=pltpu.SMEM)],
        out_specs=(pl.BlockSpec((1, 2 * nc, Q_BLOCK), lambda h: (h, 0, 0)),
                   pl.BlockSpec((1, Q_BLOCK, DN_WIDTH), lambda h: (h, 0, 0))),
        name="bias_tables",
    )(thr, rb_flat)


def _proj_kernel(x_ref, wuv_ref, wqkv_ref, lng_ref, lnb_ref, ws_ref, bs_ref,
                 a_ref, q_ref, cmp_ref, kaug_ref, vaug_ref, wink_ref, winv_ref, gate_ref, cmp_sc, *, tm, seq):
    xb = x_ref[...].astype(BF16)
    uv = jax.nn.gelu(_dot(xb, wuv_ref[...]))
    u = uv[:, :A_WIDTH]
    v = _layer_norm(uv[:, A_WIDTH:], lng_ref[...], lnb_ref[...]).astype(BF16)
    row = lax.broadcasted_iota(jnp.int32, (CHUNK, CHUNK), 0)
    col = lax.broadcasted_iota(jnp.int32, (CHUNK, CHUNK), 1)
    gd = A_WIDTH // A_GROUPS
    for g in range(A_GROUPS):
        w = jnp.where(col <= row, ws_ref[g], 0.0).astype(BF16)
        for c in range(tm // CHUNK):
            rs = slice(c * CHUNK, (c + 1) * CHUNK)
            cs = slice(g * gd, (g + 1) * gd)
            s = _dot(w, v[rs, cs]) + bs_ref[g]
            a_ref[rs, cs] = (u[rs, cs] * s).astype(BF16)

    h = _dot(xb, wqkv_ref[...])
    q_ref[...] = (h[:, 0:512] * (HEAD_DIM ** -0.5)).astype(BF16)
    for half in range(2):
        cmp_sc[half] = h[:, 512 + half * LANES:512 + (half + 1) * LANES]
        for l in range(CMP_STRIDE):
            rows = cmp_sc[half, pl.ds(l, tm // CMP_STRIDE, stride=CMP_STRIDE), :]
            cmp_ref[:, l * 256 + half * LANES:l * 256 + (half + 1) * LANES] = rows.astype(BF16)
    kaug_ref[:, 0:LANES] = h[:, 768:896].astype(BF16)
    pos = (pl.program_id(0) * tm) % seq + lax.broadcasted_iota(jnp.int32, (tm, LANES), 0)
    blk = lax.broadcasted_iota(jnp.int32, (tm, LANES), 1)
    kaug_ref[:, LANES:2 * LANES] = jnp.where(pos // SEL_BLOCK == blk, 1.0, 0.0).astype(BF16)
    ones = jnp.ones((tm, LANES), BF16)
    vaug_ref[:, 0:LANES] = h[:, 896:1024].astype(BF16)
    vaug_ref[:, LANES:2 * LANES] = ones
    wink_ref[...] = h[:, 1024:1152].astype(BF16)
    winv_ref[:, 0:LANES] = h[:, 1152:1280].astype(BF16)
    winv_ref[:, LANES:2 * LANES] = ones
    gate_ref[...] = jax.nn.sigmoid(h[:, 1280:1408])


def _proj(x2d, wuv, wqkv, lng, lnb, ws, bs, *, seq, tm=512):
    t = x2d.shape[0]
    row = lambda i: (i, 0)
    const2 = lambda i: (0, 0)
    const3 = lambda i: (0, 0, 0)
    outs = [(1, A_WIDTH, BF16), (1, 512, BF16), (CMP_STRIDE, CMP_STRIDE * 256, BF16), (1, 256, BF16),
            (1, 256, BF16), (1, 128, BF16), (1, 256, BF16), (1, 128, F32)]
    return pl.pallas_call(
        functools.partial(_proj_kernel, tm=tm, seq=seq),
        out_shape=tuple(jax.ShapeDtypeStruct((t // r, w), dt) for r, w, dt in outs),
        grid=(t // tm,),
        in_specs=[pl.BlockSpec((tm, D_MODEL), row),
                  pl.BlockSpec(wuv.shape, const2),
                  pl.BlockSpec(wqkv.shape, const2),
                  pl.BlockSpec(lng.shape, const2),
                  pl.BlockSpec(lnb.shape, const2),
                  pl.BlockSpec(ws.shape, const3),
                  pl.BlockSpec(bs.shape, const3)],
        out_specs=tuple(pl.BlockSpec((tm // r, w), row) for r, w, _ in outs),
        scratch_shapes=[pltpu.VMEM((2, tm, LANES), F32)],
        compiler_params=pltpu.CompilerParams(dimension_semantics=("parallel",), vmem_limit_bytes=VMEM_LIMIT),
        name="in_proj_gmlp",
    )(x2d, wuv, wqkv, lng, lnb, ws, bs)


def _compress_kernel(r_ref, wa_ref, wb_ref, pea_ref, peb_ref, w2_ref, kc_ref, vct_ref):
    r = r_ref[0]
    top = _dot(r, wa_ref[...])
    bot = _dot(r, wb_ref[...])
    pe = _dot(pea_ref[...], wa_ref[...]) + _dot(peb_ref[...], wb_ref[...])
    hid = top + pltpu.roll(bot, bot.shape[0] - 1, 0) + pe[0:1]
    out = _dot(jax.nn.gelu(hid).astype(BF16), w2_ref[...])
    kc_ref[0] = out[:, 0:LANES].astype(BF16)
    vct_ref[0] = out[:, LANES:2 * LANES].T.astype(BF16)


def _compress(cmp3, wa, wb, pea, peb, w2):
    b, nc, width = cmp3.shape
    const2 = lambda i: (0, 0)
    return pl.pallas_call(
        _compress_kernel,
        out_shape=(jax.ShapeDtypeStruct((b, nc, LANES), BF16), jax.ShapeDtypeStruct((b, LANES, nc), BF16)),
        grid=(b,),
        in_specs=[pl.BlockSpec((1, nc, width), lambda i: (i, 0, 0)),
                  pl.BlockSpec(wa.shape, const2), pl.BlockSpec(wb.shape, const2),
                  pl.BlockSpec(pea.shape, const2), pl.BlockSpec(peb.shape, const2),
                  pl.BlockSpec(w2.shape, const2)],
        out_specs=(pl.BlockSpec((1, nc, LANES), lambda i: (i, 0, 0)),
                   pl.BlockSpec((1, LANES, nc), lambda i: (i, 0, 0))),
        compiler_params=pltpu.CompilerParams(dimension_semantics=("parallel",), vmem_limit_bytes=VMEM_LIMIT),
        name="kv_compress",
    )(cmp3, wa, wb, pea, peb, w2)


def _softmax_tile(s, m_sc):
    cols = [s[:, j * LANES:(j + 1) * LANES] for j in range(s.shape[1] // LANES)]
    m_old = m_sc[...]
    m_new = jnp.maximum(m_old, jnp.max(functools.reduce(jnp.maximum, cols), axis=-1, keepdims=True))
    m_sc[...] = m_new
    p = jnp.concatenate([jnp.exp(c - m_new) for c in cols], axis=1).astype(BF16)
    return p, jnp.exp(m_old - m_new)


def _scale_both(acc, alpha):
    return jnp.concatenate([acc[:, :LANES] * alpha, acc[:, LANES:] * alpha], axis=1)


def _nsa_kernel(q_ref, gate_ref, kc_ref, vct_ref, kaug_ref, vaug_ref, wink_ref, winv_ref, fc_ref, dn_ref, ovt_ref,
                o_ref, qa_all, s_all, p_all, m_all, acc_all, *, nc, n_tiles):
    ib = pl.program_id(1)
    t0 = ib * Q_BLOCK
    lane = lax.broadcasted_iota(jnp.int32, (Q_BLOCK, LANES), 1)
    qrow = lax.broadcasted_iota(jnp.int32, (Q_BLOCK, LANES), 0)
    q = q_ref[...]
    gates = gate_ref[...]
    kc = kc_ref[0]
    vct = vct_ref[0]
    ovt = ovt_ref[...]

    def stack_heads(fn):
        return jnp.concatenate([fn(g) for g in range(GROUP)], axis=0)

    near0 = 2 * ib - NEAR_BACK // SEL_BLOCK
    n_far = jnp.maximum(ib - 4, 0) // 4
    n_pairs = (n_far + 1) // 2
    kstart = jnp.maximum(t0 - NEAR_BACK, 0)
    j0 = pl.multiple_of(kstart - (t0 - NEAR_BACK), LANES)
    kstart = pl.multiple_of(kstart, LANES)
    wstart = jnp.maximum(t0 - WINDOW, 0)
    jw = pl.multiple_of(wstart - (t0 - NEAR_BACK), LANES)
    wstart = pl.multiple_of(wstart, LANES)
    win_edge = jnp.where((lane > qrow) | (t0 < WINDOW), 0.0, NEG_INF)

    fc_row = pl.multiple_of((nc - 8) - 8 * ib, 8)
    qcol = lax.broadcasted_iota(jnp.int32, (1, ROWS), 1) % Q_BLOCK
    has_cmp = jnp.where(t0 + qcol >= CMP_BLOCK - 1, 1.0, 0.0)
    qhs, o_cts, imps = [], [], []
    for hkv in range(KV_HEADS):
        mine = (lane >= HEAD_DIM) if hkv else (lane < HEAD_DIM)
        qh = stack_heads(lambda g: jnp.where(mine, q[:, g * LANES:(g + 1) * LANES], 0.0).astype(BF16))
        lct = _dot_nt(kc, qh)
        lct = lct + jnp.concatenate([fc_ref[hkv * GROUP + g, pl.ds(fc_row, nc), :] for g in range(GROUP)], axis=1)
        e = jnp.exp(lct - jnp.max(lct, axis=0, keepdims=True))
        pct = e * (has_cmp / jnp.sum(e, axis=0, keepdims=True))
        o_cts.append(_dot(vct, pct.astype(BF16)))
        psum = pct[:, 0:LANES] + pct[:, LANES:2 * LANES] + pct[:, 2 * LANES:3 * LANES] + pct[:, 3 * LANES:]
        p_hi = psum.astype(BF16)
        p_lo = (psum - p_hi.astype(F32)).astype(BF16)
        imps.append(_dot(ovt, p_hi) + _dot(ovt, p_lo))
        qhs.append(qh)

    blk = lax.broadcasted_iota(jnp.int32, (LANES, KV_HEADS * Q_BLOCK), 0)
    col = lax.broadcasted_iota(jnp.int32, (LANES, KV_HEADS * Q_BLOCK), 1)
    jq = 2 * ib + jnp.where(col % Q_BLOCK >= SEL_BLOCK, 1, 0)
    forced = (blk == 0) | (blk == jq) | (blk == jq - 1)
    cand = jnp.where(forced, -3e38, jnp.where(blk > jq, NEG_INF, jnp.concatenate(imps, axis=1)))
    blk_f = blk.astype(F32)

    def pick_one(_, carry):
        cur, chosen = carry
        mx = jnp.max(cur, axis=0, keepdims=True)
        first = jnp.min(jnp.where(cur == mx, blk_f, float(LANES)), axis=0, keepdims=True)
        pick = blk_f == first
        return jnp.where(pick, -3e38, cur), jnp.where(pick, 1.0, chosen)

    _, sel_t = lax.fori_loop(0, N_SELECT - 3, pick_one, (cand, jnp.where(forced, 1.0, 0.0)), unroll=True)

    o_ws = []
    for hkv in range(KV_HEADS):
        sw = _dot_nt(qhs[hkv], wink_ref[pl.ds(wstart, WIN_KEYS), :])
        sw = sw + stack_heads(lambda g: dn_ref[hkv * GROUP + g, :, pl.ds(jw, WIN_KEYS)])
        sw = jnp.concatenate([sw[:, :LANES] + jnp.concatenate([win_edge] * GROUP, axis=0), sw[:, LANES:]], axis=1)
        pw = jnp.exp(sw - jnp.max(sw, axis=-1, keepdims=True)).astype(BF16)
        ow = _dot(pw, winv_ref[pl.ds(wstart, WIN_KEYS), :])
        o_ws.append(ow[:, :LANES] / ow[:, LANES:])

    outs = []
    for hkv in range(KV_HEADS):
        qa_sc, s_buf, p_buf = qa_all.at[hkv], s_all.at[hkv], p_all.at[hkv]
        m_sc, acc_sc = m_all.at[hkv], acc_all.at[hkv]
        qh = qhs[hkv]
        sel = sel_t[:, hkv * Q_BLOCK:(hkv + 1) * Q_BLOCK].T

        m_near = jnp.where(sel > 0, 0.0, NEG_INF).astype(BF16)
        m_far = jnp.where((sel > 0) & (lane < near0), 0.0, NEG_INF).astype(BF16)
        qa_sc[0, :, 0:LANES] = qh
        qa_sc[0, :, LANES:2 * LANES] = jnp.concatenate([m_far] * GROUP, axis=0)
        qa_sc[1, :, 0:LANES] = qh
        qa_sc[1, :, LANES:2 * LANES] = jnp.concatenate([m_near] * GROUP, axis=0)
        m_sc[...] = jnp.full(m_sc.shape, -jnp.inf, F32)
        acc_sc[...] = jnp.zeros(acc_sc.shape, F32)
        p_buf[1] = jnp.zeros((ROWS, FAR_TILE), BF16)

        def tile_keys(tile):
            near_idx = tile - 2 * n_pairs
            ks = jnp.where(near_idx >= 0, kstart + near_idx * FAR_TILE, tile * FAR_TILE)
            ks = jnp.clip(ks, 0, (n_tiles - 1) * FAR_TILE)
            return pl.multiple_of(ks, LANES), jnp.where(near_idx >= 0, 1, 0)

        def logits(tile, slot):
            ks, variant = tile_keys(tile)
            s_buf[slot] = _dot_nt(qa_sc[variant], kaug_ref[pl.ds(ks, FAR_TILE), :])

        def values(tile, slot):
            ks, _ = tile_keys(tile)
            return _dot(p_buf[slot], vaug_ref[pl.ds(ks, FAR_TILE), :])

        def absorb(slot, pending, table_col=None):
            s = s_buf[slot]
            if table_col is not None:
                js = pl.multiple_of(table_col, LANES)
                s = s + stack_heads(lambda g: dn_ref[hkv * GROUP + g, :, pl.ds(js, FAR_TILE)])
            p, alpha = _softmax_tile(s, m_sc)
            p_buf[slot] = p
            acc_sc[...] = _scale_both(acc_sc[...] + pending, alpha)

        logits(0, 0)
        o_w = o_ws[hkv]

        def far_pair(i, carry):
            t = 2 * i
            logits(t + 1, 1)
            absorb(0, values(t - 1, 1))
            logits(t + 2, 0)
            absorb(1, values(t, 0))
            return carry

        lax.fori_loop(0, n_pairs, far_pair, 0)
        t = 2 * n_pairs
        logits(t + 1, 1)
        absorb(0, values(t - 1, 1), j0)
        absorb(1, values(t, 0), j0 + FAR_TILE)
        acc = acc_sc[...] + values(t + 1, 1)
        o_s = acc[:, :LANES] / acc[:, LANES:]

        per_head = []
        for g in range(GROUP):
            c = 3 * (hkv * GROUP + g)
            rs = slice(g * Q_BLOCK, (g + 1) * Q_BLOCK)
            o_c = o_cts[hkv][:, rs].T
            per_head.append(gates[:, c:c + 1] * o_c + gates[:, c + 1:c + 2] * o_s[rs]
                            + gates[:, c + 2:c + 3] * o_w[rs])
        outs.append(per_head)

    for g in range(GROUP):
        o_ref[:, g * LANES:(g + 1) * LANES] = jnp.where(lane < HEAD_DIM, outs[0][g], outs[1][g]).astype(BF16)


def _nsa(qs, gates, kc, vct, kaug, vaug, wink, winv, fc, dn, ovt, *, batch, seq):
    nc = seq // CMP_STRIDE
    nq = seq // Q_BLOCK
    qrow = lambda b, i: (b * nq + i, 0)
    per_batch = lambda b, i: (b, 0)
    const2 = lambda b, i: (0, 0)
    const3 = lambda b, i: (0, 0, 0)
    once = pl.Buffered(1)
    return pl.pallas_call(
        functools.partial(_nsa_kernel, nc=nc, n_tiles=seq // FAR_TILE),
        out_shape=jax.ShapeDtypeStruct((batch * seq, 4 * LANES), BF16),
        grid=(batch, nq),
        in_specs=[pl.BlockSpec((Q_BLOCK, 4 * LANES), qrow),
                  pl.BlockSpec((Q_BLOCK, LANES), qrow),
                  pl.BlockSpec((1, nc, LANES), lambda b, i: (b, 0, 0)),
                  pl.BlockSpec((1, LANES, nc), lambda b, i: (b, 0, 0)),
                  pl.BlockSpec((seq, 256), per_batch, pipeline_mode=once),
                  pl.BlockSpec((seq, 256), per_batch, pipeline_mode=once),
                  pl.BlockSpec((seq, LANES), per_batch, pipeline_mode=once),
                  pl.BlockSpec((seq, 256), per_batch, pipeline_mode=once),
                  pl.BlockSpec(fc.shape, const3, pipeline_mode=once),
                  pl.BlockSpec(dn.shape, const3, pipeline_mode=once),
                  pl.BlockSpec(ovt.shape, const2)],
        out_specs=pl.BlockSpec((Q_BLOCK, 4 * LANES), qrow),
        scratch_shapes=[pltpu.VMEM((KV_HEADS, 2, ROWS, 2 * LANES), BF16),
                        pltpu.VMEM((KV_HEADS, 2, ROWS, FAR_TILE), F32),
                        pltpu.VMEM((KV_HEADS, 2, ROWS, FAR_TILE), BF16),
                        pltpu.VMEM((KV_HEADS, ROWS, LANES), F32),
                        pltpu.VMEM((KV_HEADS, ROWS, 2 * LANES), F32)],
        compiler_params=pltpu.CompilerParams(dimension_semantics=("parallel", "arbitrary"),
                                             vmem_limit_bytes=VMEM_LIMIT),
        name="sparse_attention",
    )(qs, gates, kc, vct, kaug, vaug, wink, winv, fc, dn, ovt)


def _swiglu_norm(x, wg_ref, wu_ref, wd_ref, g_ref, beta_ref, acc_ref, hc):
    xb = x.astype(BF16)
    hidden = wg_ref.shape[1]
    for c in range(hidden // hc):
        cs = slice(c * hc, (c + 1) * hc)
        gate = _dot(xb, wg_ref[:, cs])
        up = _dot(xb, wu_ref[:, cs])
        part = _dot((jax.nn.silu(gate) * up).astype(BF16), wd_ref[cs, :])
        if c == 0:
            acc_ref[...] = part
        else:
            acc_ref[...] += part
    return _layer_norm(ALPHA * x + acc_ref[...], g_ref[...], beta_ref[...])


def _outproj_ffn_kernel(x_ref, a_ref, b_ref, wa_ref, wb_ref, g1_ref, beta1_ref,
                        wg_ref, wu_ref, wd_ref, g2_ref, beta2_ref, o_ref, acc_ref, *, hc):
    y = _dot(a_ref[...], wa_ref[...]) + _dot(b_ref[...], wb_ref[...])
    x1 = _layer_norm(ALPHA * x_ref[...] + y, g1_ref[...], beta1_ref[...])
    o_ref[...] = _swiglu_norm(x1, wg_ref, wu_ref, wd_ref, g2_ref, beta2_ref, acc_ref, hc)


def _outproj_ffn(x2d, a, b, wa, wb, g1, beta1, wg, wu, wd, g2, beta2, *, tm=512, hc=256):
    t = x2d.shape[0]
    row = lambda i: (i, 0)
    const2 = lambda i: (0, 0)
    consts = (wa, wb, g1, beta1, wg, wu, wd, g2, beta2)
    return pl.pallas_call(
        functools.partial(_outproj_ffn_kernel, hc=hc),
        out_shape=jax.ShapeDtypeStruct((t, D_MODEL), F32),
        grid=(t // tm,),
        in_specs=[pl.BlockSpec((tm, D_MODEL), row), pl.BlockSpec((tm, a.shape[1]), row),
                  pl.BlockSpec((tm, b.shape[1]), row)]
        + [pl.BlockSpec(c.shape, const2, pipeline_mode=pl.Buffered(1)) for c in consts],
        out_specs=pl.BlockSpec((tm, D_MODEL), row),
        scratch_shapes=[pltpu.VMEM((tm, D_MODEL), F32)],
        compiler_params=pltpu.CompilerParams(dimension_semantics=("parallel",), vmem_limit_bytes=VMEM_LIMIT),
        name="out_proj_swiglu_ffn_norm",
    )(x2d, a, b, *consts)


def _conv_kernel(x_ref, win_ref, bin_ref, dww_ref, dwb_ref, lng_ref, lnb_ref, wout_ref, bout_ref,
                 g_ref, beta_ref, wg_ref, wu_ref, wd_ref, g2_ref, beta2_ref, o_ref, buf_ref, cv_ref, *, tm, rc, hc):
    @pl.when(pl.program_id(1) == 0)
    def _():
        buf_ref[0:HALO, :] = jnp.zeros((HALO, D_MODEL), F32)

    x = x_ref[...]
    h = _dot(x.astype(BF16), win_ref[...]) + bin_ref[...]
    buf_ref[HALO:HALO + tm, :] = h[:, :D_MODEL] * jax.nn.sigmoid(h[:, D_MODEL:])

    lead = HALO - (CONV_WIDTH - 1)

    def conv_rows(r, carry):
        r0 = pl.multiple_of(r * rc, rc)
        for lc in range(D_MODEL // LANES):
            ls = slice(lc * LANES, (lc + 1) * LANES)
            acc = jnp.broadcast_to(dwb_ref[:, ls], (rc, LANES))
            za = buf_ref[pl.ds(r0, rc + HALO), ls]
            for b in range(8):
                base, shift = 8 * ((lead + b) // 8), (lead + b) % 8
                span = rc + (8 if shift else 0)
                yb = None
                for a, j in enumerate(range(b, CONV_WIDTH, 8)):
                    term = za[base + 8 * a:base + 8 * a + span] * dww_ref[j:j + 1, ls]
                    yb = term if yb is None else yb + term
                acc = acc + yb[shift:shift + rc]
            cv_ref[pl.ds(r0, rc), ls] = acc
        return carry

    lax.fori_loop(0, tm // rc, conv_rows, 0)
    buf_ref[0:HALO, :] = buf_ref[tm:tm + HALO, :]

    y = jax.nn.silu(_layer_norm(cv_ref[...], lng_ref[...], lnb_ref[...]))
    y = _dot(y.astype(BF16), wout_ref[...]) + bout_ref[...]
    x1 = _layer_norm(ALPHA * x + y, g_ref[...], beta_ref[...])
    o_ref[...] = _swiglu_norm(x1, wg_ref, wu_ref, wd_ref, g2_ref, beta2_ref, cv_ref, hc)


def _conv_ffn(x2d, w_in, b_in, dw_w, dw_b, ln_g, ln_b, w_out, b_out, g, beta, wg, wu, wd, g2, beta2, *,
              batch, seq, tm=512, rc=64, hc=256):
    nt = seq // tm
    row = lambda b, i: (b * nt + i, 0)
    const2 = lambda b, i: (0, 0)
    consts = (w_in, b_in, dw_w, dw_b, ln_g, ln_b, w_out, b_out, g, beta, wg, wu, wd, g2, beta2)
    return pl.pallas_call(
        functools.partial(_conv_kernel, tm=tm, rc=rc, hc=hc),
        out_shape=jax.ShapeDtypeStruct((batch * seq, D_MODEL), F32),
        grid=(batch, nt),
        in_specs=[pl.BlockSpec((tm, D_MODEL), row)]
        + [pl.BlockSpec(c.shape, const2, pipeline_mode=pl.Buffered(1)) for c in consts],
        out_specs=pl.BlockSpec((tm, D_MODEL), row),
        scratch_shapes=[pltpu.VMEM((HALO + tm, D_MODEL), F32), pltpu.VMEM((tm, D_MODEL), F32)],
        compiler_params=pltpu.CompilerParams(dimension_semantics=("parallel", "arbitrary"),
                                             vmem_limit_bytes=VMEM_LIMIT),
        name="conv_module_swiglu_ffn_norm",
    )(x2d, *consts)


def _head_perm():
    p = np.arange(HEADS * HEAD_DIM)
    g, half, d = p // LANES, (p % LANES) // HEAD_DIM, p % HEAD_DIM
    return (g + GROUP * half) * HEAD_DIM + d


def _compress_weights(pe_k, w1_k, w2_k, pe_v, w1_v, w2_v):
    def block_diag(compact):
        row_c = (jnp.arange(compact.shape[0]) % 256) // HEAD_DIM
        col_c = jnp.arange(256) // HEAD_DIM
        return jnp.where(row_c[:, None] == col_c[None, :], jnp.tile(compact, (1, 4)), 0.0).astype(BF16)

    w1k = w1_k.reshape(CMP_BLOCK, 1, HEAD_DIM, HEAD_DIM)
    w1v = w1_v.reshape(CMP_BLOCK, 1, HEAD_DIM, HEAD_DIM)
    w1x = block_diag(jnp.concatenate([w1k, w1k, w1v, w1v], axis=1).reshape(CMP_BLOCK * 256, HEAD_DIM))
    w2x = block_diag(jnp.concatenate([w2_k, w2_k, w2_v, w2_v], axis=0))
    half = CMP_STRIDE * 256
    pe = jnp.stack([pe_k, pe_v])
    pex = jnp.broadcast_to(pe.transpose(1, 0, 2)[:, :, None, :], (CMP_BLOCK, 2, KV_HEADS, HEAD_DIM))
    pex = pex.reshape(2, half)
    pea = jnp.broadcast_to(pex[0:1], (8, half)).astype(BF16)
    peb = jnp.broadcast_to(pex[1:2], (8, half)).astype(BF16)
    return w1x[:half], w1x[half:], pea, peb, w2x


def _overlap_matrix(nc):
    c0 = np.arange(nc)[None, :] * CMP_STRIDE
    s0 = np.arange(LANES)[:, None] * SEL_BLOCK
    ov = (c0 < s0 + SEL_BLOCK) & (c0 + CMP_BLOCK > s0) & (np.arange(nc)[None, :] < nc - 1)
    return jnp.asarray(ov, dtype=BF16)


def _even_layer(x2d, rel_bias, w_in, w_out, ln_g, ln_b, w_s, b_s, pe_k, w1_k, w2_k, pe_v, w1_v, w2_v,
                norm_g, norm_b, *, batch, seq):
    nc = seq // CMP_STRIDE
    assert seq % FAR_TILE == 0 and seq >= NEAR and seq // SEL_BLOCK <= LANES
    perm = _head_perm()
    qw = HEADS * HEAD_DIM
    o = 2 * A_WIDTH
    wuv = w_in[:, :o].astype(BF16)
    wq = w_in[:, o:o + qw][:, perm]
    wkv = w_in[:, o + qw:o + qw + 768]
    wgt = jnp.pad(w_in[:, o + qw + 768:], ((0, 0), (0, LANES - 3 * HEADS)))
    wqkv = jnp.concatenate([wq, wkv, wgt], axis=1).astype(BF16)
    bs = jnp.broadcast_to(b_s[:, :, None], (A_GROUPS, CHUNK, A_WIDTH // A_GROUPS))
    a_out, qs, cmp2, kaug, vaug, wink, winv, gates = _proj(
        x2d, wuv, wqkv, ln_g[None, :], ln_b[None, :], w_s, bs, seq=seq)

    kc, vct = _compress(cmp2.reshape(batch, nc, CMP_STRIDE * 256),
                    *_compress_weights(pe_k, w1_k, w2_k, pe_v, w1_v, w2_v))

    fc, dn = _bias_tables(_bucket_thresholds(seq), rel_bias.T.reshape(-1), nc)
    b_out = _nsa(qs, gates, kc, vct, kaug, vaug, wink, winv, fc, dn, _overlap_matrix(nc), batch=batch, seq=seq)

    wo_a = w_out[:A_WIDTH].astype(BF16)
    wo_b = w_out[A_WIDTH:][perm].astype(BF16)
    return a_out, b_out, wo_a, wo_b, norm_g[None, :], norm_b[None, :]


def kernel(x, rel_bias, hyb_w_in, hyb_w_out, gmlp_ln_g, gmlp_ln_b, gmlp_w_s, gmlp_b_s, cmp_pe_k, cmp_w1_k, cmp_w2_k, cmp_pe_v, cmp_w1_v, cmp_w2_v, conv_w_in, conv_b_in, conv_dw_w, conv_dw_b, conv_ln_g, conv_ln_b, conv_w_out, conv_b_out, ffn_w_gate, ffn_w_up, ffn_w_down, norm_mix_g, norm_mix_b, norm_ffn_g, norm_ffn_b):
    batch, seq, d = x.shape
    h = x.reshape(batch * seq, d)
    for layer in range(DEPTH):
        i = layer // 2
        ffn = (ffn_w_gate[layer].astype(BF16), ffn_w_up[layer].astype(BF16), ffn_w_down[layer].astype(BF16),
               norm_ffn_g[layer][None, :], norm_ffn_b[layer][None, :])
        if layer % 2 == 0:
            mixer = _even_layer(h, rel_bias, hyb_w_in[i], hyb_w_out[i], gmlp_ln_g[i], gmlp_ln_b[i],
                                gmlp_w_s[i], gmlp_b_s[i], cmp_pe_k[i], cmp_w1_k[i], cmp_w2_k[i],
                                cmp_pe_v[i], cmp_w1_v[i], cmp_w2_v[i],
                                norm_mix_g[layer], norm_mix_b[layer], batch=batch, seq=seq)
            h = _outproj_ffn(h, *mixer, *ffn)
        else:
            h = _conv_ffn(h, conv_w_in[i].astype(BF16), conv_b_in[i][None, :], conv_dw_w[i], conv_dw_b[i][None, :],
                          conv_ln_g[i][None, :], conv_ln_b[i][None, :], conv_w_out[i].astype(BF16),
                          conv_b_out[i][None, :], norm_mix_g[layer][None, :], norm_mix_b[layer][None, :],
                          *ffn, batch=batch, seq=seq)
    return h.reshape(batch, seq, d)
```

```python
import functools
import math

import numpy as np
import jax
import jax.numpy as jnp
from jax import lax
from jax.experimental import pallas as pl
from jax.experimental.pallas import tpu as pltpu

F32 = jnp.float32
BF16 = jnp.bfloat16

D_MODEL = 1024
DEPTH = 2
ALPHA = (2 * DEPTH) ** 0.25
CHUNK = 128
A_WIDTH = D_MODEL // 2
A_GROUPS = 4
HEADS = 8
KV_HEADS = 2
GROUP = HEADS // KV_HEADS
HEAD_DIM = (D_MODEL // 2) // HEADS
CMP_STRIDE = 16
CMP_BLOCK = 32
SEL_BLOCK = 64
N_SELECT = 16
WINDOW = 512
Q_BLOCK = 128
N_BUCKETS = 32
MAX_DISTANCE = 1024
CONV_WIDTH = 31
NEG_INF = -1e30
FORCE = 1e9
LN_EPS = 1e-5

LANES = 128
VMEM_LIMIT = 56 * 1024 * 1024

ROWS = GROUP * Q_BLOCK
FAR_TILE = 512
NEAR_TILES = 8
NEAR = NEAR_TILES * Q_BLOCK
NEAR_BACK = NEAR - Q_BLOCK
WIN_KEYS = WINDOW + Q_BLOCK
DN_WIDTH = NEAR + NEAR_BACK
HALO = 32


def _dot(a, b):
    return jnp.dot(a, b, preferred_element_type=F32)


def _dot_nt(a, b):
    return lax.dot_general(a, b, (((1,), (1,)), ((), ())), preferred_element_type=F32)


def _layer_norm(z, g, b):
    mu = jnp.mean(z, axis=-1, keepdims=True)
    d = z - mu
    var = jnp.mean(d * d, axis=-1, keepdims=True)
    return d * lax.rsqrt(var + LN_EPS) * g + b


def _t5_bucket(dist):
    n = jnp.maximum(dist, 0)
    max_exact = N_BUCKETS // 2
    nf = jnp.maximum(n, 1).astype(jnp.float32)
    large = max_exact + (jnp.log(nf / max_exact) / math.log(MAX_DISTANCE / max_exact)
                         * (N_BUCKETS - max_exact)).astype(jnp.int32)
    large = jnp.minimum(large, N_BUCKETS - 1)
    return jnp.where(n < max_exact, n, large)


def _bucket_thresholds(seq):
    b = _t5_bucket(jnp.arange(seq, dtype=jnp.int32))
    j = jnp.arange(N_BUCKETS, dtype=jnp.int32)
    return jnp.sum((b[None, :] < j[:, None]).astype(jnp.int32), axis=1).astype(jnp.int32)


def _tables_kernel(thr_ref, rb_ref, fc_ref, dn_ref, *, nc):
    h = pl.program_id(0)
    base = h * N_BUCKETS
    far = rb_ref[base + N_BUCKETS - 1]

    def bias_of(dist):
        val = jnp.full(dist.shape, far, F32)
        for j in range(N_BUCKETS - 1, 0, -1):
            val = jnp.where(dist < thr_ref[j], rb_ref[base + j - 1], val)
        return val

    ql = lax.broadcasted_iota(jnp.int32, (Q_BLOCK, LANES), 0)
    ln = lax.broadcasted_iota(jnp.int32, (Q_BLOCK, LANES), 1)
    for c in range(2 * nc // Q_BLOCK):
        dist = ln - CMP_STRIDE * (ql + (c * Q_BLOCK - (nc - 8))) - (CMP_BLOCK - 1)
        fc_ref[0, c * Q_BLOCK:(c + 1) * Q_BLOCK, :] = jnp.where(dist >= 0, bias_of(dist), NEG_INF)
    for c in range(DN_WIDTH // LANES):
        dist = ql + (NEAR_BACK - c * LANES) - ln
        dn_ref[0, :, c * LANES:(c + 1) * LANES] = jnp.where(dist >= 0, bias_of(dist) - far, NEG_INF)


def _bias_tables(thr, rb_flat, nc):
    return pl.pallas_call(
        functools.partial(_tables_kernel, nc=nc),
        out_shape=(jax.ShapeDtypeStruct((HEADS, 2 * nc, Q_BLOCK), F32),
                   jax.ShapeDtypeStruct((HEADS, Q_BLOCK, DN_WIDTH), F32)),
        grid=(HEADS,),
        in_specs=[pl.BlockSpec(memory_space=pltpu.SMEM), pl.BlockSpec(memory_space=pltpu.SMEM)],
        out_specs=(pl.BlockSpec((1, 2 * nc, Q_BLOCK), lambda h: (h, 0, 0)),
                   pl.BlockSpec((1, Q_BLOCK, DN_WIDTH), lambda h: (h, 0, 0))),
        name="bias_tables",
    )(thr, rb_flat)


def _proj_kernel(x_ref, wuv_ref, wqkv_ref, lng_ref, lnb_ref, ws_ref, bs_ref,
                 a_ref, q_ref, cmp_ref, kaug_ref, vaug_ref, wink_ref, winv_ref, gate_ref, cmp_sc, *, tm, seq):
    xb = x_ref[...].astype(BF16)
    uv = jax.nn.gelu(_dot(xb, wuv_ref[...]))
    u = uv[:, :A_WIDTH]
    v = _layer_norm(uv[:, A_WIDTH:], lng_ref[...], lnb_ref[...]).astype(BF16)
    row = lax.broadcasted_iota(jnp.int32, (CHUNK, CHUNK), 0)
    col = lax.broadcasted_iota(jnp.int32, (CHUNK, CHUNK), 1)
    gd = A_WIDTH // A_GROUPS
    for g in range(A_GROUPS):
        w = jnp.where(col <= row, ws_ref[g], 0.0).astype(BF16)
        for c in range(tm // CHUNK):
            rs = slice(c * CHUNK, (c + 1) * CHUNK)
            cs = slice(g * gd, (g + 1) * gd)
            s = _dot(w, v[rs, cs]) + bs_ref[g]
            a_ref[rs, cs] = (u[rs, cs] * s).astype(BF16)

    h = _dot(xb, wqkv_ref[...])
    q_ref[...] = (h[:, 0:512] * (HEAD_DIM ** -0.5)).astype(BF16)
    for half in range(2):
        cmp_sc[half] = h[:, 512 + half * LANES:512 + (half + 1) * LANES]
        for l in range(CMP_STRIDE):
            rows = cmp_sc[half, pl.ds(l, tm // CMP_STRIDE, stride=CMP_STRIDE), :]
            cmp_ref[:, l * 256 + half * LANES:l * 256 + (half + 1) * LANES] = rows.astype(BF16)
    kaug_ref[:, 0:LANES] = h[:, 768:896].astype(BF16)
    pos = (pl.program_id(0) * tm) % seq + lax.broadcasted_iota(jnp.int32, (tm, LANES), 0)
    blk = lax.broadcasted_iota(jnp.int32, (tm, LANES), 1)
    kaug_ref[:, LANES:2 * LANES] = jnp.where(pos // SEL_BLOCK == blk, 1.0, 0.0).astype(BF16)
    ones = jnp.ones((tm, LANES), BF16)
    vaug_ref[:, 0:LANES] = h[:, 896:1024].astype(BF16)
    vaug_ref[:, LANES:2 * LANES] = ones
    wink_ref[...] = h[:, 1024:1152].astype(BF16)
    winv_ref[:, 0:LANES] = h[:, 1152:1280].astype(BF16)
    winv_ref[:, LANES:2 * LANES] = ones
    gate_ref[...] = jax.nn.sigmoid(h[:, 1280:1408])


def _proj(x2d, wuv, wqkv, lng, lnb, ws, bs, *, seq, tm=512):
    t = x2d.shape[0]
    row = lambda i: (i, 0)
    const2 = lambda i: (0, 0)
    const3 = lambda i: (0, 0, 0)
    outs = [(1, A_WIDTH, BF16), (1, 512, BF16), (CMP_STRIDE, CMP_STRIDE * 256, BF16), (1, 256, BF16),
            (1, 256, BF16), (1, 128, BF16), (1, 256, BF16), (1, 128, F32)]
    return pl.pallas_call(
        functools.partial(_proj_kernel, tm=tm, seq=seq),
        out_shape=tuple(jax.ShapeDtypeStruct((t // r, w), dt) for r, w, dt in outs),
        grid=(t // tm,),
        in_specs=[pl.BlockSpec((tm, D_MODEL), row),
                  pl.BlockSpec(wuv.shape, const2),
                  pl.BlockSpec(wqkv.shape, const2),
                  pl.BlockSpec(lng.shape, const2),
                  pl.BlockSpec(lnb.shape, const2),
                  pl.BlockSpec(ws.shape, const3),
                  pl.BlockSpec(bs.shape, const3)],
        out_specs=tuple(pl.BlockSpec((tm // r, w), row) for r, w, _ in outs),
        scratch_shapes=[pltpu.VMEM((2, tm, LANES), F32)],
        compiler_params=pltpu.CompilerParams(dimension_semantics=("parallel",), vmem_limit_bytes=VMEM_LIMIT),
        name="in_proj_gmlp",
    )(x2d, wuv, wqkv, lng, lnb, ws, bs)


def _compress_kernel(r_ref, wa_ref, wb_ref, pea_ref, peb_ref, w2_ref, kc_ref, vct_ref):
    r = r_ref[0]
    top = _dot(r, wa_ref[...])
    bot = _dot(r, wb_ref[...])
    pe = _dot(pea_ref[...], wa_ref[...]) + _dot(peb_ref[...], wb_ref[...])
    hid = top + pltpu.roll(bot, bot.shape[0] - 1, 0) + pe[0:1]
    out = _dot(jax.nn.gelu(hid).astype(BF16), w2_ref[...])
    kc_ref[0] = out[:, 0:LANES].astype(BF16)
    vct_ref[0] = out[:, LANES:2 * LANES].T.astype(BF16)


def _compress(cmp3, wa, wb, pea, peb, w2):
    b, nc, width = cmp3.shape
    const2 = lambda i: (0, 0)
    return pl.pallas_call(
        _compress_kernel,
        out_shape=(jax.ShapeDtypeStruct((b, nc, LANES), BF16), jax.ShapeDtypeStruct((b, LANES, nc), BF16)),
        grid=(b,),
        in_specs=[pl.BlockSpec((1, nc, width), lambda i: (i, 0, 0)),
                  pl.BlockSpec(wa.shape, const2), pl.BlockSpec(wb.shape, const2),
                  pl.BlockSpec(pea.shape, const2), pl.BlockSpec(peb.shape, const2),
                  pl.BlockSpec(w2.shape, const2)],
        out_specs=(pl.BlockSpec((1, nc, LANES), lambda i: (i, 0, 0)),
                   pl.BlockSpec((1, LANES, nc), lambda i: (i, 0, 0))),
        compiler_params=pltpu.CompilerParams(dimension_semantics=("parallel",), vmem_limit_bytes=VMEM_LIMIT),
        name="kv_compress",
    )(cmp3, wa, wb, pea, peb, w2)


def _softmax_tile(s, m_sc):
    cols = [s[:, j * LANES:(j + 1) * LANES] for j in range(s.shape[1] // LANES)]
    m_old = m_sc[...]
    m_new = jnp.maximum(m_old, jnp.max(functools.reduce(jnp.maximum, cols), axis=-1, keepdims=True))
    m_sc[...] = m_new
    p = jnp.concatenate([jnp.exp(c - m_new) for c in cols], axis=1).astype(BF16)
    return p, jnp.exp(m_old - m_new)


def _scale_both(acc, alpha):
    return jnp.concatenate([acc[:, :LANES] * alpha, acc[:, LANES:] * alpha], axis=1)


def _nsa_kernel(q_ref, gate_ref, kc_ref, vct_ref, kaug_ref, vaug_ref, wink_ref, winv_ref, fc_ref, dn_ref, ovt_ref,
                o_ref, qa_all, s_all, p_all, m_all, acc_all, *, nc, n_tiles):
    ib = pl.program_id(1)
    t0 = ib * Q_BLOCK
    lane = lax.broadcasted_iota(jnp.int32, (Q_BLOCK, LANES), 1)
    qrow = lax.broadcasted_iota(jnp.int32, (Q_BLOCK, LANES), 0)
    q = q_ref[...]
    gates = gate_ref[...]
    kc = kc_ref[0]
    vct = vct_ref[0]
    ovt = ovt_ref[...]

    def stack_heads(fn):
        return jnp.concatenate([fn(g) for g in range(GROUP)], axis=0)

    near0 = 2 * ib - NEAR_BACK // SEL_BLOCK
    n_far = jnp.maximum(ib - 4, 0) // 4
    n_pairs = (n_far + 1) // 2
    kstart = jnp.maximum(t0 - NEAR_BACK, 0)
    j0 = pl.multiple_of(kstart - (t0 - NEAR_BACK), LANES)
    kstart = pl.multiple_of(kstart, LANES)
    wstart = jnp.maximum(t0 - WINDOW, 0)
    jw = pl.multiple_of(wstart - (t0 - NEAR_BACK), LANES)
    wstart = pl.multiple_of(wstart, LANES)
    win_edge = jnp.where((lane > qrow) | (t0 < WINDOW), 0.0, NEG_INF)

    fc_row = pl.multiple_of((nc - 8) - 8 * ib, 8)
    qcol = lax.broadcasted_iota(jnp.int32, (1, ROWS), 1) % Q_BLOCK
    has_cmp = jnp.where(t0 + qcol >= CMP_BLOCK - 1, 1.0, 0.0)
    qhs, o_cts, imps = [], [], []
    for hkv in range(KV_HEADS):
        mine = (lane >= HEAD_DIM) if hkv else (lane < HEAD_DIM)
        qh = stack_heads(lambda g: jnp.where(mine, q[:, g * LANES:(g + 1) * LANES], 0.0).astype(BF16))
        lct = _dot_nt(kc, qh)
        lct = lct + jnp.concatenate([fc_ref[hkv * GROUP + g, pl.ds(fc_row, nc), :] for g in range(GROUP)], axis=1)
        e = jnp.exp(lct - jnp.max(lct, axis=0, keepdims=True))
        pct = e * (has_cmp / jnp.sum(e, axis=0, keepdims=True))
        o_cts.append(_dot(vct, pct.astype(BF16)))
        psum = pct[:, 0:LANES] + pct[:, LANES:2 * LANES] + pct[:, 2 * LANES:3 * LANES] + pct[:, 3 * LANES:]
        p_hi = psum.astype(BF16)
        p_lo = (psum - p_hi.astype(F32)).astype(BF16)
        imps.append(_dot(ovt, p_hi) + _dot(ovt, p_lo))
        qhs.append(qh)

    blk = lax.broadcasted_iota(jnp.int32, (LANES, KV_HEADS * Q_BLOCK), 0)
    col = lax.broadcasted_iota(jnp.int32, (LANES, KV_HEADS * Q_BLOCK), 1)
    jq = 2 * ib + jnp.where(col % Q_BLOCK >= SEL_BLOCK, 1, 0)
    forced = (blk == 0) | (blk == jq) | (blk == jq - 1)
    cand = jnp.where(forced, -3e38, jnp.where(blk > jq, NEG_INF, jnp.concatenate(imps, axis=1)))
    blk_f = blk.astype(F32)

    def pick_one(_, carry):
        cur, chosen = carry
        mx = jnp.max(cur, axis=0, keepdims=True)
        first = jnp.min(jnp.where(cur == mx, blk_f, float(LANES)), axis=0, keepdims=True)
        pick = blk_f == first
        return jnp.where(pick, -3e38, cur), jnp.where(pick, 1.0, chosen)

    _, sel_t = lax.fori_loop(0, N_SELECT - 3, pick_one, (cand, jnp.where(forced, 1.0, 0.0)), unroll=True)

    outs = []
    for hkv in range(KV_HEADS):
        qa_sc, s_buf, p_buf = qa_all.at[hkv], s_all.at[hkv], p_all.at[hkv]
        m_sc, acc_sc = m_all.at[hkv], acc_all.at[hkv]
        qh = qhs[hkv]
        sel = sel_t[:, hkv * Q_BLOCK:(hkv + 1) * Q_BLOCK].T

        m_near = jnp.where(sel > 0, 0.0, NEG_INF).astype(BF16)
        m_far = jnp.where((sel > 0) & (lane < near0), 0.0, NEG_INF).astype(BF16)
        qa_sc[0, :, 0:LANES] = qh
        qa_sc[0, :, LANES:2 * LANES] = jnp.concatenate([m_far] * GROUP, axis=0)
        qa_sc[1, :, 0:LANES] = qh
        qa_sc[1, :, LANES:2 * LANES] = jnp.concatenate([m_near] * GROUP, axis=0)
        m_sc[...] = jnp.full(m_sc.shape, -jnp.inf, F32)
        acc_sc[...] = jnp.zeros(acc_sc.shape, F32)
        p_buf[1] = jnp.zeros((ROWS, FAR_TILE), BF16)

        def tile_keys(tile):
            near_idx = tile - 2 * n_pairs
            ks = jnp.where(near_idx >= 0, kstart + near_idx * FAR_TILE, tile * FAR_TILE)
            ks = jnp.clip(ks, 0, (n_tiles - 1) * FAR_TILE)
            return pl.multiple_of(ks, LANES), jnp.where(near_idx >= 0, 1, 0)

        def logits(tile, slot):
            ks, variant = tile_keys(tile)
            s_buf[slot] = _dot_nt(qa_sc[variant], kaug_ref[pl.ds(ks, FAR_TILE), :])

        def values(tile, slot):
            ks, _ = tile_keys(tile)
            return _dot(p_buf[slot], vaug_ref[pl.ds(ks, FAR_TILE), :])

        def absorb(slot, pending, table_col=None):
            s = s_buf[slot]
            if table_col is not None:
                js = pl.multiple_of(table_col, LANES)
                s = s + stack_heads(lambda g: dn_ref[hkv * GROUP + g, :, pl.ds(js, FAR_TILE)])
            p, alpha = _softmax_tile(s, m_sc)
            p_buf[slot] = p
            acc_sc[...] = _scale_both(acc_sc[...] + pending, alpha)

        logits(0, 0)

        sw = _dot_nt(qh, wink_ref[pl.ds(wstart, WIN_KEYS), :])
        sw = sw + stack_heads(lambda g: dn_ref[hkv * GROUP + g, :, pl.ds(jw, WIN_KEYS)])
        sw = jnp.concatenate([sw[:, :LANES] + jnp.concatenate([win_edge] * GROUP, axis=0), sw[:, LANES:]], axis=1)
        pw = jnp.exp(sw - jnp.max(sw, axis=-1, keepdims=True)).astype(BF16)
        ow = _dot(pw, winv_ref[pl.ds(wstart, WIN_KEYS), :])
        o_w = ow[:, :LANES] / ow[:, LANES:]

        def far_pair(i, carry):
            t = 2 * i
            logits(t + 1, 1)
            absorb(0, values(t - 1, 1))
            logits(t + 2, 0)
            absorb(1, values(t, 0))
            return carry

        lax.fori_loop(0, n_pairs, far_pair, 0)
        t = 2 * n_pairs
        logits(t + 1, 1)
        absorb(0, values(t - 1, 1), j0)
        absorb(1, values(t, 0), j0 + FAR_TILE)
        acc = acc_sc[...] + values(t + 1, 1)
        o_s = acc[:, :LANES] / acc[:, LANES:]

        per_head = []
        for g in range(GROUP):
            c = 3 * (hkv * GROUP + g)
            rs = slice(g * Q_BLOCK, (g + 1) * Q_BLOCK)
            o_c = o_cts[hkv][:, rs].T
            per_head.append(gates[:, c:c + 1] * o_c + gates[:, c + 1:c + 2] * o_s[rs]
                            + gates[:, c + 2:c + 3] * o_w[rs])
        outs.append(per_head)

    for g in range(GROUP):
        o_ref[:, g * LANES:(g + 1) * LANES] = jnp.where(lane < HEAD_DIM, outs[0][g], outs[1][g]).astype(BF16)


def _nsa(qs, gates, kc, vct, kaug, vaug, wink, winv, fc, dn, ovt, *, batch, seq):
    nc = seq // CMP_STRIDE
    nq = seq // Q_BLOCK
    qrow = lambda b, i: (b * nq + i, 0)
    per_batch = lambda b, i: (b, 0)
    const2 = lambda b, i: (0, 0)
    const3 = lambda b, i: (0, 0, 0)
    once = pl.Buffered(1)
    return pl.pallas_call(
        functools.partial(_nsa_kernel, nc=nc, n_tiles=seq // FAR_TILE),
        out_shape=jax.ShapeDtypeStruct((batch * seq, 4 * LANES), BF16),
        grid=(batch, nq),
        in_specs=[pl.BlockSpec((Q_BLOCK, 4 * LANES), qrow),
                  pl.BlockSpec((Q_BLOCK, LANES), qrow),
                  pl.BlockSpec((1, nc, LANES), lambda b, i: (b, 0, 0)),
                  pl.BlockSpec((1, LANES, nc), lambda b, i: (b, 0, 0)),
                  pl.BlockSpec((seq, 256), per_batch, pipeline_mode=once),
                  pl.BlockSpec((seq, 256), per_batch, pipeline_mode=once),
                  pl.BlockSpec((seq, LANES), per_batch, pipeline_mode=once),
                  pl.BlockSpec((seq, 256), per_batch, pipeline_mode=once),
                  pl.BlockSpec(fc.shape, const3, pipeline_mode=once),
                  pl.BlockSpec(dn.shape, const3, pipeline_mode=once),
                  pl.BlockSpec(ovt.shape, const2)],
        out_specs=pl.BlockSpec((Q_BLOCK, 4 * LANES), qrow),
        scratch_shapes=[pltpu.VMEM((KV_HEADS, 2, ROWS, 2 * LANES), BF16),
                        pltpu.VMEM((KV_HEADS, 2, ROWS, FAR_TILE), F32),
                        pltpu.VMEM((KV_HEADS, 2, ROWS, FAR_TILE), BF16),
                        pltpu.VMEM((KV_HEADS, ROWS, LANES), F32),
                        pltpu.VMEM((KV_HEADS, ROWS, 2 * LANES), F32)],
        compiler_params=pltpu.CompilerParams(dimension_semantics=("parallel", "arbitrary"),
                                             vmem_limit_bytes=VMEM_LIMIT),
        name="sparse_attention",
    )(qs, gates, kc, vct, kaug, vaug, wink, winv, fc, dn, ovt)


def _swiglu_norm(x, wg_ref, wu_ref, wd_ref, g_ref, beta_ref, acc_ref, hc):
    xb = x.astype(BF16)
    hidden = wg_ref.shape[1]
    for c in range(hidden // hc):
        cs = slice(c * hc, (c + 1) * hc)
        gate = _dot(xb, wg_ref[:, cs])
        up = _dot(xb, wu_ref[:, cs])
        part = _dot((jax.nn.silu(gate) * up).astype(BF16), wd_ref[cs, :])
        if c == 0:
            acc_ref[...] = part
        else:
            acc_ref[...] += part
    return _layer_norm(ALPHA * x + acc_ref[...], g_ref[...], beta_ref[...])


def _outproj_ffn_kernel(x_ref, a_ref, b_ref, wa_ref, wb_ref, g1_ref, beta1_ref,
                        wg_ref, wu_ref, wd_ref, g2_ref, beta2_ref, o_ref, acc_ref, *, hc):
    y = _dot(a_ref[...], wa_ref[...]) + _dot(b_ref[...], wb_ref[...])
    x1 = _layer_norm(ALPHA * x_ref[...] + y, g1_ref[...], beta1_ref[...])
    o_ref[...] = _swiglu_norm(x1, wg_ref, wu_ref, wd_ref, g2_ref, beta2_ref, acc_ref, hc)


def _outproj_ffn(x2d, a, b, wa, wb, g1, beta1, wg, wu, wd, g2, beta2, *, tm=512, hc=256):
    t = x2d.shape[0]
    row = lambda i: (i, 0)
    const2 = lambda i: (0, 0)
    consts = (wa, wb, g1, beta1, wg, wu, wd, g2, beta2)
    return pl.pallas_call(
        functools.partial(_outproj_ffn_kernel, hc=hc),
        out_shape=jax.ShapeDtypeStruct((t, D_MODEL), F32),
        grid=(t // tm,),
        in_specs=[pl.BlockSpec((tm, D_MODEL), row), pl.BlockSpec((tm, a.shape[1]), row),
                  pl.BlockSpec((tm, b.shape[1]), row)]
        + [pl.BlockSpec(c.shape, const2, pipeline_mode=pl.Buffered(1)) for c in consts],
        out_specs=pl.BlockSpec((tm, D_MODEL), row),
        scratch_shapes=[pltpu.VMEM((tm, D_MODEL), F32)],
        compiler_params=pltpu.CompilerParams(dimension_semantics=("parallel",), vmem_limit_bytes=VMEM_LIMIT),
        name="out_proj_swiglu_ffn_norm",
    )(x2d, a, b, *consts)


def _conv_kernel(x_ref, win_ref, bin_ref, dww_ref, dwb_ref, lng_ref, lnb_ref, wout_ref, bout_ref,
                 g_ref, beta_ref, wg_ref, wu_ref, wd_ref, g2_ref, beta2_ref, o_ref, buf_ref, cv_ref, *, tm, rc, hc):
    @pl.when(pl.program_id(1) == 0)
    def _():
        buf_ref[0:HALO, :] = jnp.zeros((HALO, D_MODEL), F32)

    x = x_ref[...]
    h = _dot(x.astype(BF16), win_ref[...]) + bin_ref[...]
    buf_ref[HALO:HALO + tm, :] = h[:, :D_MODEL] * jax.nn.sigmoid(h[:, D_MODEL:])

    lead = HALO - (CONV_WIDTH - 1)

    def conv_rows(r, carry):
        r0 = pl.multiple_of(r * rc, rc)
        for lc in range(D_MODEL // LANES):
            ls = slice(lc * LANES, (lc + 1) * LANES)
            acc = jnp.broadcast_to(dwb_ref[:, ls], (rc, LANES))
            za = buf_ref[pl.ds(r0, rc + HALO), ls]
            for b in range(8):
                base, shift = 8 * ((lead + b) // 8), (lead + b) % 8
                span = rc + (8 if shift else 0)
                yb = None
                for a, j in enumerate(range(b, CONV_WIDTH, 8)):
                    term = za[base + 8 * a:base + 8 * a + span] * dww_ref[j:j + 1, ls]
                    yb = term if yb is None else yb + term
                acc = acc + yb[shift:shift + rc]
            cv_ref[pl.ds(r0, rc), ls] = acc
        return carry

    lax.fori_loop(0, tm // rc, conv_rows, 0)
    buf_ref[0:HALO, :] = buf_ref[tm:tm + HALO, :]

    y = jax.nn.silu(_layer_norm(cv_ref[...], lng_ref[...], lnb_ref[...]))
    y = _dot(y.astype(BF16), wout_ref[...]) + bout_ref[...]
    x1 = _layer_norm(ALPHA * x + y, g_ref[...], beta_ref[...])
    o_ref[...] = _swiglu_norm(x1, wg_ref, wu_ref, wd_ref, g2_ref, beta2_ref, cv_ref, hc)


def _conv_ffn(x2d, w_in, b_in, dw_w, dw_b, ln_g, ln_b, w_out, b_out, g, beta, wg, wu, wd, g2, beta2, *,
              batch, seq, tm=512, rc=64, hc=256):
    nt = seq // tm
    row = lambda b, i: (b * nt + i, 0)
    const2 = lambda b, i: (0, 0)
    consts = (w_in, b_in, dw_w, dw_b, ln_g, ln_b, w_out, b_out, g, beta, wg, wu, wd, g2, beta2)
    return pl.pallas_call(
        functools.partial(_conv_kernel, tm=tm, rc=rc, hc=hc),
        out_shape=jax.ShapeDtypeStruct((batch * seq, D_MODEL), F32),
        grid=(batch, nt),
        in_specs=[pl.BlockSpec((tm, D_MODEL), row)]
        + [pl.BlockSpec(c.shape, const2, pipeline_mode=pl.Buffered(1)) for c in consts],
        out_specs=pl.BlockSpec((tm, D_MODEL), row),
        scratch_shapes=[pltpu.VMEM((HALO + tm, D_MODEL), F32), pltpu.VMEM((tm, D_MODEL), F32)],
        compiler_params=pltpu.CompilerParams(dimension_semantics=("parallel", "arbitrary"),
                                             vmem_limit_bytes=VMEM_LIMIT),
        name="conv_module_swiglu_ffn_norm",
    )(x2d, *consts)


def _head_perm():
    p = np.arange(HEADS * HEAD_DIM)
    g, half, d = p // LANES, (p % LANES) // HEAD_DIM, p % HEAD_DIM
    return (g + GROUP * half) * HEAD_DIM + d


def _compress_weights(pe_k, w1_k, w2_k, pe_v, w1_v, w2_v):
    def block_diag(compact):
        row_c = (jnp.arange(compact.shape[0]) % 256) // HEAD_DIM
        col_c = jnp.arange(256) // HEAD_DIM
        return jnp.where(row_c[:, None] == col_c[None, :], jnp.tile(compact, (1, 4)), 0.0).astype(BF16)

    w1k = w1_k.reshape(CMP_BLOCK, 1, HEAD_DIM, HEAD_DIM)
    w1v = w1_v.reshape(CMP_BLOCK, 1, HEAD_DIM, HEAD_DIM)
    w1x = block_diag(jnp.concatenate([w1k, w1k, w1v, w1v], axis=1).reshape(CMP_BLOCK * 256, HEAD_DIM))
    w2x = block_diag(jnp.concatenate([w2_k, w2_k, w2_v, w2_v], axis=0))
    half = CMP_STRIDE * 256
    pe = jnp.stack([pe_k, pe_v])
    pex = jnp.broadcast_to(pe.transpose(1, 0, 2)[:, :, None, :], (CMP_BLOCK, 2, KV_HEADS, HEAD_DIM))
    pex = pex.reshape(2, half)
    pea = jnp.broadcast_to(pex[0:1], (8, half)).astype(BF16)
    peb = jnp.broadcast_to(pex[1:2], (8, half)).astype(BF16)
    return w1x[:half], w1x[half:], pea, peb, w2x


def _overlap_matrix(nc):
    c0 = np.arange(nc)[None, :] * CMP_STRIDE
    s0 = np.arange(LANES)[:, None] * SEL_BLOCK
    ov = (c0 < s0 + SEL_BLOCK) & (c0 + CMP_BLOCK > s0) & (np.arange(nc)[None, :] < nc - 1)
    return jnp.asarray(ov, dtype=BF16)


def _even_layer(x2d, rel_bias, w_in, w_out, ln_g, ln_b, w_s, b_s, pe_k, w1_k, w2_k, pe_v, w1_v, w2_v,
                norm_g, norm_b, *, batch, seq):
    nc = seq // CMP_STRIDE
    assert seq % FAR_TILE == 0 and seq >= NEAR and seq // SEL_BLOCK <= LANES
    perm = _head_perm()
    qw = HEADS * HEAD_DIM
    o = 2 * A_WIDTH
    wuv = w_in[:, :o].astype(BF16)
    wq = w_in[:, o:o + qw][:, perm]
    wkv = w_in[:, o + qw:o + qw + 768]
    wgt = jnp.pad(w_in[:, o + qw + 768:], ((0, 0), (0, LANES - 3 * HEADS)))
    wqkv = jnp.concatenate([wq, wkv, wgt], axis=1).astype(BF16)
    bs = jnp.broadcast_to(b_s[:, :, None], (A_GROUPS, CHUNK, A_WIDTH // A_GROUPS))
    a_out, qs, cmp2, kaug, vaug, wink, winv, gates = _proj(
        x2d, wuv, wqkv, ln_g[None, :], ln_b[None, :], w_s, bs, seq=seq)

    kc, vct = _compress(cmp2.reshape(batch, nc, CMP_STRIDE * 256),
                    *_compress_weights(pe_k, w1_k, w2_k, pe_v, w1_v, w2_v))

    fc, dn = _bias_tables(_bucket_thresholds(seq), rel_bias.T.reshape(-1), nc)
    b_out = _nsa(qs, gates, kc, vct, kaug, vaug, wink, winv, fc, dn, _overlap_matrix(nc), batch=batch, seq=seq)

    wo_a = w_out[:A_WIDTH].astype(BF16)
    wo_b = w_out[A_WIDTH:][perm].astype(BF16)
    return a_out, b_out, wo_a, wo_b, norm_g[None, :], norm_b[None, :]


def kernel(x, rel_bias, hyb_w_in, hyb_w_out, gmlp_ln_g, gmlp_ln_b, gmlp_w_s, gmlp_b_s, cmp_pe_k, cmp_w1_k, cmp_w2_k, cmp_pe_v, cmp_w1_v, cmp_w2_v, conv_w_in, conv_b_in, conv_dw_w, conv_dw_b, conv_ln_g, conv_ln_b, conv_w_out, conv_b_out, ffn_w_gate, ffn_w_up, ffn_w_down, norm_mix_g, norm_mix_b, norm_ffn_g, norm_ffn_b):
    batch, seq, d = x.shape
    h = x.reshape(batch * seq, d)
    for layer in range(DEPTH):
        i = layer // 2
        ffn = (ffn_w_gate[layer].astype(BF16), ffn_w_up[layer].astype(BF16), ffn_w_down[layer].astype(BF16),
               norm_ffn_g[layer][None, :], norm_ffn_b[layer][None, :])
        if layer % 2 == 0:
            mixer = _even_layer(h, rel_bias, hyb_w_in[i], hyb_w_out[i], gmlp_ln_g[i], gmlp_ln_b[i],
                                gmlp_w_s[i], gmlp_b_s[i], cmp_pe_k[i], cmp_w1_k[i], cmp_w2_k[i],
                                cmp_pe_v[i], cmp_w1_v[i], cmp_w2_v[i],
                                norm_mix_g[layer], norm_mix_b[layer], batch=batch, seq=seq)
            h = _outproj_ffn(h, *mixer, *ffn)
        else:
            h = _conv_ffn(h, conv_w_in[i].astype(BF16), conv_b_in[i][None, :], conv_dw_w[i], conv_dw_b[i][None, :],
                          conv_ln_g[i][None, :], conv_ln_b[i][None, :], conv_w_out[i].astype(BF16),
                          conv_b_out[i][None, :], norm_mix_g[layer][None, :], norm_mix_b[layer][None, :],
                          *ffn, batch=batch, seq=seq)
    return h.reshape(batch, seq, d)
```

```python
import functools
import math

import numpy as np
import jax
import jax.numpy as jnp
from jax import lax
from jax.experimental import pallas as pl
from jax.experimental.pallas import tpu as pltpu

F32 = jnp.float32
BF16 = jnp.bfloat16

D_MODEL = 1024
DEPTH = 2
ALPHA = (2 * DEPTH) ** 0.25
CHUNK = 128
A_WIDTH = D_MODEL // 2
A_GROUPS = 4
HEADS = 8
KV_HEADS = 2
GROUP = HEADS // KV_HEADS
HEAD_DIM = (D_MODEL // 2) // HEADS
CMP_STRIDE = 16
CMP_BLOCK = 32
SEL_BLOCK = 64
N_SELECT = 16
WINDOW = 512
Q_BLOCK = 128
N_BUCKETS = 32
MAX_DISTANCE = 1024
CONV_WIDTH = 31
NEG_INF = -1e30
FORCE = 1e9
LN_EPS = 1e-5

LANES = 128
VMEM_LIMIT = 56 * 1024 * 1024

ROWS = GROUP * Q_BLOCK
FAR_TILE = 512
NEAR_TILES = 8
NEAR = NEAR_TILES * Q_BLOCK
NEAR_BACK = NEAR - Q_BLOCK
WIN_KEYS = WINDOW + Q_BLOCK
DN_WIDTH = NEAR + NEAR_BACK
HALO = 32


def _dot(a, b):
    return jnp.dot(a, b, preferred_element_type=F32)


def _dot_nt(a, b):
    return lax.dot_general(a, b, (((1,), (1,)), ((), ())), preferred_element_type=F32)


def _layer_norm(z, g, b):
    mu = jnp.mean(z, axis=-1, keepdims=True)
    d = z - mu
    var = jnp.mean(d * d, axis=-1, keepdims=True)
    return d * lax.rsqrt(var + LN_EPS) * g + b


def _t5_bucket(dist):
    n = jnp.maximum(dist, 0)
    max_exact = N_BUCKETS // 2
    nf = jnp.maximum(n, 1).astype(jnp.float32)
    large = max_exact + (jnp.log(nf / max_exact) / math.log(MAX_DISTANCE / max_exact)
                         * (N_BUCKETS - max_exact)).astype(jnp.int32)
    large = jnp.minimum(large, N_BUCKETS - 1)
    return jnp.where(n < max_exact, n, large)


def _bucket_thresholds(seq):
    b = _t5_bucket(jnp.arange(seq, dtype=jnp.int32))
    j = jnp.arange(N_BUCKETS, dtype=jnp.int32)
    return jnp.sum((b[None, :] < j[:, None]).astype(jnp.int32), axis=1).astype(jnp.int32)


def _tables_kernel(thr_ref, rb_ref, fc_ref, dn_ref, *, nc):
    h = pl.program_id(0)
    base = h * N_BUCKETS
    far = rb_ref[base + N_BUCKETS - 1]

    def bias_of(dist):
        val = jnp.full(dist.shape, far, F32)
        for j in range(N_BUCKETS - 1, 0, -1):
            val = jnp.where(dist < thr_ref[j], rb_ref[base + j - 1], val)
        return val

    ql = lax.broadcasted_iota(jnp.int32, (Q_BLOCK, LANES), 0)
    ln = lax.broadcasted_iota(jnp.int32, (Q_BLOCK, LANES), 1)
    for c in range(2 * nc // Q_BLOCK):
        dist = ln - CMP_STRIDE * (ql + (c * Q_BLOCK - (nc - 8))) - (CMP_BLOCK - 1)
        fc_ref[0, c * Q_BLOCK:(c + 1) * Q_BLOCK, :] = jnp.where(dist >= 0, bias_of(dist), NEG_INF)
    for c in range(DN_WIDTH // LANES):
        dist = ql + (NEAR_BACK - c * LANES) - ln
        dn_ref[0, :, c * LANES:(c + 1) * LANES] = jnp.where(dist >= 0, bias_of(dist) - far, NEG_INF)


def _bias_tables(thr, rb_flat, nc):
    return pl.pallas_call(
        functools.partial(_tables_kernel, nc=nc),
        out_shape=(jax.ShapeDtypeStruct((HEADS, 2 * nc, Q_BLOCK), F32),
                   jax.ShapeDtypeStruct((HEADS, Q_BLOCK, DN_WIDTH), F32)),
        grid=(HEADS,),
        in_specs=[pl.BlockSpec(memory_space=pltpu.SMEM), pl.BlockSpec(memory_space=pltpu.SMEM)],
        out_specs=(pl.BlockSpec((1, 2 * nc, Q_BLOCK), lambda h: (h, 0, 0)),
                   pl.BlockSpec((1, Q_BLOCK, DN_WIDTH), lambda h: (h, 0, 0))),
        name="bias_tables",
    )(thr, rb_flat)


def _proj_kernel(x_ref, wuv_ref, wqkv_ref, lng_ref, lnb_ref, ws_ref, bs_ref,
                 a_ref, q_ref, cmp_ref, kaug_ref, vaug_ref, wink_ref, winv_ref, gate_ref, cmp_sc, *, tm, seq):
    xb = x_ref[...].astype(BF16)
    uv = jax.nn.gelu(_dot(xb, wuv_ref[...]))
    u = uv[:, :A_WIDTH]
    v = _layer_norm(uv[:, A_WIDTH:], lng_ref[...], lnb_ref[...]).astype(BF16)
    row = lax.broadcasted_iota(jnp.int32, (CHUNK, CHUNK), 0)
    col = lax.broadcasted_iota(jnp.int32, (CHUNK, CHUNK), 1)
    gd = A_WIDTH // A_GROUPS
    for g in range(A_GROUPS):
        w = jnp.where(col <= row, ws_ref[g], 0.0).astype(BF16)
        for c in range(tm // CHUNK):
            rs = slice(c * CHUNK, (c + 1) * CHUNK)
            cs = slice(g * gd, (g + 1) * gd)
            s = _dot(w, v[rs, cs]) + bs_ref[g]
            a_ref[rs, cs] = (u[rs, cs] * s).astype(BF16)

    h = _dot(xb, wqkv_ref[...])
    q_ref[...] = (h[:, 0:512] * (HEAD_DIM ** -0.5)).astype(BF16)
    for half in range(2):
        cmp_sc[half] = h[:, 512 + half * LANES:512 + (half + 1) * LANES]
        for l in range(CMP_STRIDE):
            rows = cmp_sc[half, pl.ds(l, tm // CMP_STRIDE, stride=CMP_STRIDE), :]
            cmp_ref[:, l * 256 + half * LANES:l * 256 + (half + 1) * LANES] = rows.astype(BF16)
    kaug_ref[:, 0:LANES] = h[:, 768:896].astype(BF16)
    pos = (pl.program_id(0) * tm) % seq + lax.broadcasted_iota(jnp.int32, (tm, LANES), 0)
    blk = lax.broadcasted_iota(jnp.int32, (tm, LANES), 1)
    kaug_ref[:, LANES:2 * LANES] = jnp.where(pos // SEL_BLOCK == blk, 1.0, 0.0).astype(BF16)
    ones = jnp.ones((tm, LANES), BF16)
    vaug_ref[:, 0:LANES] = h[:, 896:1024].astype(BF16)
    vaug_ref[:, LANES:2 * LANES] = ones
    wink_ref[...] = h[:, 1024:1152].astype(BF16)
    winv_ref[:, 0:LANES] = h[:, 1152:1280].astype(BF16)
    winv_ref[:, LANES:2 * LANES] = ones
    gate_ref[...] = jax.nn.sigmoid(h[:, 1280:1408])


def _proj(x2d, wuv, wqkv, lng, lnb, ws, bs, *, seq, tm=512):
    t = x2d.shape[0]
    row = lambda i: (i, 0)
    const2 = lambda i: (0, 0)
    const3 = lambda i: (0, 0, 0)
    outs = [(1, A_WIDTH, BF16), (1, 512, BF16), (CMP_STRIDE, CMP_STRIDE * 256, BF16), (1, 256, BF16),
            (1, 256, BF16), (1, 128, BF16), (1, 256, BF16), (1, 128, F32)]
    return pl.pallas_call(
        functools.partial(_proj_kernel, tm=tm, seq=seq),
        out_shape=tuple(jax.ShapeDtypeStruct((t // r, w), dt) for r, w, dt in outs),
        grid=(t // tm,),
        in_specs=[pl.BlockSpec((tm, D_MODEL), row),
                  pl.BlockSpec(wuv.shape, const2),
                  pl.BlockSpec(wqkv.shape, const2),
                  pl.BlockSpec(lng.shape, const2),
                  pl.BlockSpec(lnb.shape, const2),
                  pl.BlockSpec(ws.shape, const3),
                  pl.BlockSpec(bs.shape, const3)],
        out_specs=tuple(pl.BlockSpec((tm // r, w), row) for r, w, _ in outs),
        scratch_shapes=[pltpu.VMEM((2, tm, LANES), F32)],
        compiler_params=pltpu.CompilerParams(dimension_semantics=("parallel",), vmem_limit_bytes=VMEM_LIMIT),
        name="in_proj_gmlp",
    )(x2d, wuv, wqkv, lng, lnb, ws, bs)


def _compress_kernel(r_ref, wa_ref, wb_ref, pea_ref, peb_ref, w2_ref, kc_ref, vct_ref):
    r = r_ref[0]
    top = _dot(r, wa_ref[...])
    bot = _dot(r, wb_ref[...])
    pe = _dot(pea_ref[...], wa_ref[...]) + _dot(peb_ref[...], wb_ref[...])
    hid = top + pltpu.roll(bot, bot.shape[0] - 1, 0) + pe[0:1]
    out = _dot(jax.nn.gelu(hid).astype(BF16), w2_ref[...])
    kc_ref[0] = out[:, 0:LANES].astype(BF16)
    vct_ref[0] = out[:, LANES:2 * LANES].T.astype(BF16)


def _compress(cmp3, wa, wb, pea, peb, w2):
    b, nc, width = cmp3.shape
    const2 = lambda i: (0, 0)
    return pl.pallas_call(
        _compress_kernel,
        out_shape=(jax.ShapeDtypeStruct((b, nc, LANES), BF16), jax.ShapeDtypeStruct((b, LANES, nc), BF16)),
        grid=(b,),
        in_specs=[pl.BlockSpec((1, nc, width), lambda i: (i, 0, 0)),
                  pl.BlockSpec(wa.shape, const2), pl.BlockSpec(wb.shape, const2),
                  pl.BlockSpec(pea.shape, const2), pl.BlockSpec(peb.shape, const2),
                  pl.BlockSpec(w2.shape, const2)],
        out_specs=(pl.BlockSpec((1, nc, LANES), lambda i: (i, 0, 0)),
                   pl.BlockSpec((1, LANES, nc), lambda i: (i, 0, 0))),
        compiler_params=pltpu.CompilerParams(dimension_semantics=("parallel",), vmem_limit_bytes=VMEM_LIMIT),
        name="kv_compress",
    )(cmp3, wa, wb, pea, peb, w2)


def _softmax_tile(s, m_sc):
    cols = [s[:, j * LANES:(j + 1) * LANES] for j in range(s.shape[1] // LANES)]
    m_old = m_sc[...]
    m_new = jnp.maximum(m_old, jnp.max(functools.reduce(jnp.maximum, cols), axis=-1, keepdims=True))
    m_sc[...] = m_new
    p = jnp.concatenate([jnp.exp(c - m_new) for c in cols], axis=1).astype(BF16)
    return p, jnp.exp(m_old - m_new)


def _scale_both(acc, alpha):
    return jnp.concatenate([acc[:, :LANES] * alpha, acc[:, LANES:] * alpha], axis=1)


def _nsa_kernel(q_ref, gate_ref, kc_ref, vct_ref, kaug_ref, vaug_ref, wink_ref, winv_ref, fc_ref, dn_ref, ovt_ref,
                o_ref, qa_all, s_all, p_all, m_all, acc_all, *, nc, n_tiles):
    ib = pl.program_id(1)
    t0 = ib * Q_BLOCK
    lane = lax.broadcasted_iota(jnp.int32, (Q_BLOCK, LANES), 1)
    qrow = lax.broadcasted_iota(jnp.int32, (Q_BLOCK, LANES), 0)
    q = q_ref[...]
    gates = gate_ref[...]
    kc = kc_ref[0]
    vct = vct_ref[0]
    ovt = ovt_ref[...]

    def stack_heads(fn):
        return jnp.concatenate([fn(g) for g in range(GROUP)], axis=0)

    near0 = 2 * ib - NEAR_BACK // SEL_BLOCK
    n_far = jnp.maximum(ib - 4, 0) // 4
    n_pairs = (n_far + 1) // 2
    kstart = jnp.maximum(t0 - NEAR_BACK, 0)
    j0 = pl.multiple_of(kstart - (t0 - NEAR_BACK), LANES)
    kstart = pl.multiple_of(kstart, LANES)
    wstart = jnp.maximum(t0 - WINDOW, 0)
    jw = pl.multiple_of(wstart - (t0 - NEAR_BACK), LANES)
    wstart = pl.multiple_of(wstart, LANES)
    win_edge = jnp.where((lane > qrow) | (t0 < WINDOW), 0.0, NEG_INF)

    fc_row = pl.multiple_of((nc - 8) - 8 * ib, 8)
    qcol = lax.broadcasted_iota(jnp.int32, (1, ROWS), 1) % Q_BLOCK
    has_cmp = jnp.where(t0 + qcol >= CMP_BLOCK - 1, 1.0, 0.0)
    qhs, o_cts, imps = [], [], []
    for hkv in range(KV_HEADS):
        mine = (lane >= HEAD_DIM) if hkv else (lane < HEAD_DIM)
        qh = stack_heads(lambda g: jnp.where(mine, q[:, g * LANES:(g + 1) * LANES], 0.0).astype(BF16))
        lct = _dot_nt(kc, qh)
        lct = lct + jnp.concatenate([fc_ref[hkv * GROUP + g, pl.ds(fc_row, nc), :] for g in range(GROUP)], axis=1)
        e = jnp.exp(lct - jnp.max(lct, axis=0, keepdims=True))
        pct = e * (has_cmp / jnp.sum(e, axis=0, keepdims=True))
        o_cts.append(_dot(vct, pct.astype(BF16)))
        psum = pct[:, 0:LANES] + pct[:, LANES:2 * LANES] + pct[:, 2 * LANES:3 * LANES] + pct[:, 3 * LANES:]
        p_hi = psum.astype(BF16)
        p_lo = (psum - p_hi.astype(F32)).astype(BF16)
        imps.append(_dot(ovt, p_hi) + _dot(ovt, p_lo))
        qhs.append(qh)

    blk = lax.broadcasted_iota(jnp.int32, (LANES, KV_HEADS * Q_BLOCK), 0)
    col = lax.broadcasted_iota(jnp.int32, (LANES, KV_HEADS * Q_BLOCK), 1)
    jq = 2 * ib + jnp.where(col % Q_BLOCK >= SEL_BLOCK, 1, 0)
    forced = (blk == 0) | (blk == jq) | (blk == jq - 1)
    cand = jnp.where(forced, -3e38, jnp.where(blk > jq, NEG_INF, jnp.concatenate(imps, axis=1)))
    blk_f = blk.astype(F32)

    def pick_one(_, carry):
        cur, chosen = carry
        mx = jnp.max(cur, axis=0, keepdims=True)
        first = jnp.min(jnp.where(cur == mx, blk_f, float(LANES)), axis=0, keepdims=True)
        pick = blk_f == first
        return jnp.where(pick, -3e38, cur), jnp.where(pick, 1.0, chosen)

    _, sel_t = lax.fori_loop(0, N_SELECT - 3, pick_one, (cand, jnp.where(forced, 1.0, 0.0)), unroll=True)

    def window_branch(hkv):
        sw = _dot_nt(qhs[hkv], wink_ref[pl.ds(wstart, WIN_KEYS), :])
        sw = sw + stack_heads(lambda g: dn_ref[hkv * GROUP + g, :, pl.ds(jw, WIN_KEYS)])
        sw = jnp.concatenate([sw[:, :LANES] + jnp.concatenate([win_edge] * GROUP, axis=0), sw[:, LANES:]], axis=1)
        pw = jnp.exp(sw - jnp.max(sw, axis=-1, keepdims=True)).astype(BF16)
        ow = _dot(pw, winv_ref[pl.ds(wstart, WIN_KEYS), :])
        return ow[:, :LANES] / ow[:, LANES:]

    o_w_first = window_branch(0)

    outs = []
    for hkv in range(KV_HEADS):
        qa_sc, s_buf, p_buf = qa_all.at[hkv], s_all.at[hkv], p_all.at[hkv]
        m_sc, acc_sc = m_all.at[hkv], acc_all.at[hkv]
        qh = qhs[hkv]
        sel = sel_t[:, hkv * Q_BLOCK:(hkv + 1) * Q_BLOCK].T

        m_near = jnp.where(sel > 0, 0.0, NEG_INF).astype(BF16)
        m_far = jnp.where((sel > 0) & (lane < near0), 0.0, NEG_INF).astype(BF16)
        qa_sc[0, :, 0:LANES] = qh
        qa_sc[0, :, LANES:2 * LANES] = jnp.concatenate([m_far] * GROUP, axis=0)
        qa_sc[1, :, 0:LANES] = qh
        qa_sc[1, :, LANES:2 * LANES] = jnp.concatenate([m_near] * GROUP, axis=0)
        m_sc[...] = jnp.full(m_sc.shape, -jnp.inf, F32)
        acc_sc[...] = jnp.zeros(acc_sc.shape, F32)
        p_buf[1] = jnp.zeros((ROWS, FAR_TILE), BF16)

        def tile_keys(tile):
            near_idx = tile - 2 * n_pairs
            ks = jnp.where(near_idx >= 0, kstart + near_idx * FAR_TILE, tile * FAR_TILE)
            ks = jnp.clip(ks, 0, (n_tiles - 1) * FAR_TILE)
            return pl.multiple_of(ks, LANES), jnp.where(near_idx >= 0, 1, 0)

        def logits(tile, slot):
            ks, variant = tile_keys(tile)
            s_buf[slot] = _dot_nt(qa_sc[variant], kaug_ref[pl.ds(ks, FAR_TILE), :])

        def values(tile, slot):
            ks, _ = tile_keys(tile)
            return _dot(p_buf[slot], vaug_ref[pl.ds(ks, FAR_TILE), :])

        def absorb(slot, pending, table_col=None):
            s = s_buf[slot]
            if table_col is not None:
                js = pl.multiple_of(table_col, LANES)
                s = s + stack_heads(lambda g: dn_ref[hkv * GROUP + g, :, pl.ds(js, FAR_TILE)])
            p, alpha = _softmax_tile(s, m_sc)
            p_buf[slot] = p
            acc_sc[...] = _scale_both(acc_sc[...] + pending, alpha)

        logits(0, 0)

        o_w = o_w_first if hkv == 0 else window_branch(hkv)

        def far_pair(i, carry):
            t = 2 * i
            logits(t + 1, 1)
            absorb(0, values(t - 1, 1))
            logits(t + 2, 0)
            absorb(1, values(t, 0))
            return carry

        lax.fori_loop(0, n_pairs, far_pair, 0)
        t = 2 * n_pairs
        logits(t + 1, 1)
        absorb(0, values(t - 1, 1), j0)
        absorb(1, values(t, 0), j0 + FAR_TILE)
        acc = acc_sc[...] + values(t + 1, 1)
        o_s = acc[:, :LANES] / acc[:, LANES:]

        per_head = []
        for g in range(GROUP):
            c = 3 * (hkv * GROUP + g)
            rs = slice(g * Q_BLOCK, (g + 1) * Q_BLOCK)
            o_c = o_cts[hkv][:, rs].T
            per_head.append(gates[:, c:c + 1] * o_c + gates[:, c + 1:c + 2] * o_s[rs]
                            + gates[:, c + 2:c + 3] * o_w[rs])
        outs.append(per_head)

    for g in range(GROUP):
        o_ref[:, g * LANES:(g + 1) * LANES] = jnp.where(lane < HEAD_DIM, outs[0][g], outs[1][g]).astype(BF16)


def _nsa(qs, gates, kc, vct, kaug, vaug, wink, winv, fc, dn, ovt, *, batch, seq):
    nc = seq // CMP_STRIDE
    nq = seq // Q_BLOCK
    qrow = lambda b, i: (b * nq + i, 0)
    per_batch = lambda b, i: (b, 0)
    const2 = lambda b, i: (0, 0)
    const3 = lambda b, i: (0, 0, 0)
    once = pl.Buffered(1)
    return pl.pallas_call(
        functools.partial(_nsa_kernel, nc=nc, n_tiles=seq // FAR_TILE),
        out_shape=jax.ShapeDtypeStruct((batch * seq, 4 * LANES), BF16),
        grid=(batch, nq),
        in_specs=[pl.BlockSpec((Q_BLOCK, 4 * LANES), qrow),
                  pl.BlockSpec((Q_BLOCK, LANES), qrow),
                  pl.BlockSpec((1, nc, LANES), lambda b, i: (b, 0, 0)),
                  pl.BlockSpec((1, LANES, nc), lambda b, i: (b, 0, 0)),
                  pl.BlockSpec((seq, 256), per_batch, pipeline_mode=once),
                  pl.BlockSpec((seq, 256), per_batch, pipeline_mode=once),
                  pl.BlockSpec((seq, LANES), per_batch, pipeline_mode=once),
                  pl.BlockSpec((seq, 256), per_batch, pipeline_mode=once),
                  pl.BlockSpec(fc.shape, const3, pipeline_mode=once),
                  pl.BlockSpec(dn.shape, const3, pipeline_mode=once),
                  pl.BlockSpec(ovt.shape, const2)],
        out_specs=pl.BlockSpec((Q_BLOCK, 4 * LANES), qrow),
        scratch_shapes=[pltpu.VMEM((KV_HEADS, 2, ROWS, 2 * LANES), BF16),
                        pltpu.VMEM((KV_HEADS, 2, ROWS, FAR_TILE), F32),
                        pltpu.VMEM((KV_HEADS, 2, ROWS, FAR_TILE), BF16),
                        pltpu.VMEM((KV_HEADS, ROWS, LANES), F32),
                        pltpu.VMEM((KV_HEADS, ROWS, 2 * LANES), F32)],
        compiler_params=pltpu.CompilerParams(dimension_semantics=("parallel", "arbitrary"),
                                             vmem_limit_bytes=VMEM_LIMIT),
        name="sparse_attention",
    )(qs, gates, kc, vct, kaug, vaug, wink, winv, fc, dn, ovt)


def _swiglu_norm(x, wg_ref, wu_ref, wd_ref, g_ref, beta_ref, acc_ref, hc):
    xb = x.astype(BF16)
    hidden = wg_ref.shape[1]
    for c in range(hidden // hc):
        cs = slice(c * hc, (c + 1) * hc)
        gate = _dot(xb, wg_ref[:, cs])
        up = _dot(xb, wu_ref[:, cs])
        part = _dot((jax.nn.silu(gate) * up).astype(BF16), wd_ref[cs, :])
        if c == 0:
            acc_ref[...] = part
        else:
            acc_ref[...] += part
    return _layer_norm(ALPHA * x + acc_ref[...], g_ref[...], beta_ref[...])


def _outproj_ffn_kernel(x_ref, a_ref, b_ref, wa_ref, wb_ref, g1_ref, beta1_ref,
                        wg_ref, wu_ref, wd_ref, g2_ref, beta2_ref, o_ref, acc_ref, *, hc):
    y = _dot(a_ref[...], wa_ref[...]) + _dot(b_ref[...], wb_ref[...])
    x1 = _layer_norm(ALPHA * x_ref[...] + y, g1_ref[...], beta1_ref[...])
    o_ref[...] = _swiglu_norm(x1, wg_ref, wu_ref, wd_ref, g2_ref, beta2_ref, acc_ref, hc)


def _outproj_ffn(x2d, a, b, wa, wb, g1, beta1, wg, wu, wd, g2, beta2, *, tm=512, hc=256):
    t = x2d.shape[0]
    row = lambda i: (i, 0)
    const2 = lambda i: (0, 0)
    consts = (wa, wb, g1, beta1, wg, wu, wd, g2, beta2)
    return pl.pallas_call(
        functools.partial(_outproj_ffn_kernel, hc=hc),
        out_shape=jax.ShapeDtypeStruct((t, D_MODEL), F32),
        grid=(t // tm,),
        in_specs=[pl.BlockSpec((tm, D_MODEL), row), pl.BlockSpec((tm, a.shape[1]), row),
                  pl.BlockSpec((tm, b.shape[1]), row)]
        + [pl.BlockSpec(c.shape, const2, pipeline_mode=pl.Buffered(1)) for c in consts],
        out_specs=pl.BlockSpec((tm, D_MODEL), row),
        scratch_shapes=[pltpu.VMEM((tm, D_MODEL), F32)],
        compiler_params=pltpu.CompilerParams(dimension_semantics=("parallel",), vmem_limit_bytes=VMEM_LIMIT),
        name="out_proj_swiglu_ffn_norm",
    )(x2d, a, b, *consts)


def _conv_kernel(x_ref, win_ref, bin_ref, dww_ref, dwb_ref, lng_ref, lnb_ref, wout_ref, bout_ref,
                 g_ref, beta_ref, wg_ref, wu_ref, wd_ref, g2_ref, beta2_ref, o_ref, buf_ref, cv_ref, *, tm, rc, hc):
    @pl.when(pl.program_id(1) == 0)
    def _():
        buf_ref[0:HALO, :] = jnp.zeros((HALO, D_MODEL), F32)

    x = x_ref[...]
    h = _dot(x.astype(BF16), win_ref[...]) + bin_ref[...]
    buf_ref[HALO:HALO + tm, :] = h[:, :D_MODEL] * jax.nn.sigmoid(h[:, D_MODEL:])

    lead = HALO - (CONV_WIDTH - 1)

    def conv_rows(r, carry):
        r0 = pl.multiple_of(r * rc, rc)
        for lc in range(D_MODEL // LANES):
            ls = slice(lc * LANES, (lc + 1) * LANES)
            acc = jnp.broadcast_to(dwb_ref[:, ls], (rc, LANES))
            za = buf_ref[pl.ds(r0, rc + HALO), ls]
            for b in range(8):
                base, shift = 8 * ((lead + b) // 8), (lead + b) % 8
                span = rc + (8 if shift else 0)
                yb = None
                for a, j in enumerate(range(b, CONV_WIDTH, 8)):
                    term = za[base + 8 * a:base + 8 * a + span] * dww_ref[j:j + 1, ls]
                    yb = term if yb is None else yb + term
                acc = acc + yb[shift:shift + rc]
            cv_ref[pl.ds(r0, rc), ls] = acc
        return carry

    lax.fori_loop(0, tm // rc, conv_rows, 0)
    buf_ref[0:HALO, :] = buf_ref[tm:tm + HALO, :]

    y = jax.nn.silu(_layer_norm(cv_ref[...], lng_ref[...], lnb_ref[...]))
    y = _dot(y.astype(BF16), wout_ref[...]) + bout_ref[...]
    x1 = _layer_norm(ALPHA * x + y, g_ref[...], beta_ref[...])
    o_ref[...] = _swiglu_norm(x1, wg_ref, wu_ref, wd_ref, g2_ref, beta2_ref, cv_ref, hc)


def _conv_ffn(x2d, w_in, b_in, dw_w, dw_b, ln_g, ln_b, w_out, b_out, g, beta, wg, wu, wd, g2, beta2, *,
              batch, seq, tm=512, rc=64, hc=256):
    nt = seq // tm
    row = lambda b, i: (b * nt + i, 0)
    const2 = lambda b, i: (0, 0)
    consts = (w_in, b_in, dw_w, dw_b, ln_g, ln_b, w_out, b_out, g, beta, wg, wu, wd, g2, beta2)
    return pl.pallas_call(
        functools.partial(_conv_kernel, tm=tm, rc=rc, hc=hc),
        out_shape=jax.ShapeDtypeStruct((batch * seq, D_MODEL), F32),
        grid=(batch, nt),
        in_specs=[pl.BlockSpec((tm, D_MODEL), row)]
        + [pl.BlockSpec(c.shape, const2, pipeline_mode=pl.Buffered(1)) for c in consts],
        out_specs=pl.BlockSpec((tm, D_MODEL), row),
        scratch_shapes=[pltpu.VMEM((HALO + tm, D_MODEL), F32), pltpu.VMEM((tm, D_MODEL), F32)],
        compiler_params=pltpu.CompilerParams(dimension_semantics=("parallel", "arbitrary"),
                                             vmem_limit_bytes=VMEM_LIMIT),
        name="conv_module_swiglu_ffn_norm",
    )(x2d, *consts)


def _head_perm():
    p = np.arange(HEADS * HEAD_DIM)
    g, half, d = p // LANES, (p % LANES) // HEAD_DIM, p % HEAD_DIM
    return (g + GROUP * half) * HEAD_DIM + d


def _compress_weights(pe_k, w1_k, w2_k, pe_v, w1_v, w2_v):
    def block_diag(compact):
        row_c = (jnp.arange(compact.shape[0]) % 256) // HEAD_DIM
        col_c = jnp.arange(256) // HEAD_DIM
        return jnp.where(row_c[:, None] == col_c[None, :], jnp.tile(compact, (1, 4)), 0.0).astype(BF16)

    w1k = w1_k.reshape(CMP_BLOCK, 1, HEAD_DIM, HEAD_DIM)
    w1v = w1_v.reshape(CMP_BLOCK, 1, HEAD_DIM, HEAD_DIM)
    w1x = block_diag(jnp.concatenate([w1k, w1k, w1v, w1v], axis=1).reshape(CMP_BLOCK * 256, HEAD_DIM))
    w2x = block_diag(jnp.concatenate([w2_k, w2_k, w2_v, w2_v], axis=0))
    half = CMP_STRIDE * 256
    pe = jnp.stack([pe_k, pe_v])
    pex = jnp.broadcast_to(pe.transpose(1, 0, 2)[:, :, None, :], (CMP_BLOCK, 2, KV_HEADS, HEAD_DIM))
    pex = pex.reshape(2, half)
    pea = jnp.broadcast_to(pex[0:1], (8, half)).astype(BF16)
    peb = jnp.broadcast_to(pex[1:2], (8, half)).astype(BF16)
    return w1x[:half], w1x[half:], pea, peb, w2x


def _overlap_matrix(nc):
    c0 = np.arange(nc)[None, :] * CMP_STRIDE
    s0 = np.arange(LANES)[:, None] * SEL_BLOCK
    ov = (c0 < s0 + SEL_BLOCK) & (c0 + CMP_BLOCK > s0) & (np.arange(nc)[None, :] < nc - 1)
    return jnp.asarray(ov, dtype=BF16)


def _even_layer(x2d, rel_bias, w_in, w_out, ln_g, ln_b, w_s, b_s, pe_k, w1_k, w2_k, pe_v, w1_v, w2_v,
                norm_g, norm_b, *, batch, seq):
    nc = seq // CMP_STRIDE
    assert seq % FAR_TILE == 0 and seq >= NEAR and seq // SEL_BLOCK <= LANES
    perm = _head_perm()
    qw = HEADS * HEAD_DIM
    o = 2 * A_WIDTH
    wuv = w_in[:, :o].astype(BF16)
    wq = w_in[:, o:o + qw][:, perm]
    wkv = w_in[:, o + qw:o + qw + 768]
    wgt = jnp.pad(w_in[:, o + qw + 768:], ((0, 0), (0, LANES - 3 * HEADS)))
    wqkv = jnp.concatenate([wq, wkv, wgt], axis=1).astype(BF16)
    bs = jnp.broadcast_to(b_s[:, :, None], (A_GROUPS, CHUNK, A_WIDTH // A_GROUPS))
    a_out, qs, cmp2, kaug, vaug, wink, winv, gates = _proj(
        x2d, wuv, wqkv, ln_g[None, :], ln_b[None, :], w_s, bs, seq=seq)

    kc, vct = _compress(cmp2.reshape(batch, nc, CMP_STRIDE * 256),
                    *_compress_weights(pe_k, w1_k, w2_k, pe_v, w1_v, w2_v))

    fc, dn = _bias_tables(_bucket_thresholds(seq), rel_bias.T.reshape(-1), nc)
    b_out = _nsa(qs, gates, kc, vct, kaug, vaug, wink, winv, fc, dn, _overlap_matrix(nc), batch=batch, seq=seq)

    wo_a = w_out[:A_WIDTH].astype(BF16)
    wo_b = w_out[A_WIDTH:][perm].astype(BF16)
    return a_out, b_out, wo_a, wo_b, norm_g[None, :], norm_b[None, :]


def kernel(x, rel_bias, hyb_w_in, hyb_w_out, gmlp_ln_g, gmlp_ln_b, gmlp_w_s, gmlp_b_s, cmp_pe_k, cmp_w1_k, cmp_w2_k, cmp_pe_v, cmp_w1_v, cmp_w2_v, conv_w_in, conv_b_in, conv_dw_w, conv_dw_b, conv_ln_g, conv_ln_b, conv_w_out, conv_b_out, ffn_w_gate, ffn_w_up, ffn_w_down, norm_mix_g, norm_mix_b, norm_ffn_g, norm_ffn_b):
    batch, seq, d = x.shape
    h = x.reshape(batch * seq, d)
    for layer in range(DEPTH):
        i = layer // 2
        ffn = (ffn_w_gate[layer].astype(BF16), ffn_w_up[layer].astype(BF16), ffn_w_down[layer].astype(BF16),
               norm_ffn_g[layer][None, :], norm_ffn_b[layer][None, :])
        if layer % 2 == 0:
            mixer = _even_layer(h, rel_bias, hyb_w_in[i], hyb_w_out[i], gmlp_ln_g[i], gmlp_ln_b[i],
                                gmlp_w_s[i], gmlp_b_s[i], cmp_pe_k[i], cmp_w1_k[i], cmp_w2_k[i],
                                cmp_pe_v[i], cmp_w1_v[i], cmp_w2_v[i],
                                norm_mix_g[layer], norm_mix_b[layer], batch=batch, seq=seq)
            h = _outproj_ffn(h, *mixer, *ffn)
        else:
            h = _conv_ffn(h, conv_w_in[i].astype(BF16), conv_b_in[i][None, :], conv_dw_w[i], conv_dw_b[i][None, :],
                          conv_ln_g[i][None, :], conv_ln_b[i][None, :], conv_w_out[i].astype(BF16),
                          conv_b_out[i][None, :], norm_mix_g[layer][None, :], norm_mix_b[layer][None, :],
                          *ffn, batch=batch, seq=seq)
    return h.reshape(batch, seq, d)
```

```python
import functools
import math

import numpy as np
import jax
import jax.numpy as jnp
from jax import lax
from jax.experimental import pallas as pl
from jax.experimental.pallas import tpu as pltpu

F32 = jnp.float32
BF16 = jnp.bfloat16

D_MODEL = 1024
DEPTH = 2
ALPHA = (2 * DEPTH) ** 0.25
CHUNK = 128
A_WIDTH = D_MODEL // 2
A_GROUPS = 4
HEADS = 8
KV_HEADS = 2
GROUP = HEADS // KV_HEADS
HEAD_DIM = (D_MODEL // 2) // HEADS
CMP_STRIDE = 16
CMP_BLOCK = 32
SEL_BLOCK = 64
N_SELECT = 16
WINDOW = 512
Q_BLOCK = 128
N_BUCKETS = 32
MAX_DISTANCE = 1024
CONV_WIDTH = 31
NEG_INF = -1e30
FORCE = 1e9
LN_EPS = 1e-5

LANES = 128
VMEM_LIMIT = 56 * 1024 * 1024

ROWS = GROUP * Q_BLOCK
FAR_TILE = 512
NEAR_TILES = 8
NEAR = NEAR_TILES * Q_BLOCK
NEAR_BACK = NEAR - Q_BLOCK
WIN_KEYS = WINDOW + Q_BLOCK
DN_WIDTH = NEAR + NEAR_BACK
HALO = 32


def _dot(a, b):
    return jnp.dot(a, b, preferred_element_type=F32)


def _dot_nt(a, b):
    return lax.dot_general(a, b, (((1,), (1,)), ((), ())), preferred_element_type=F32)


def _layer_norm(z, g, b):
    mu = jnp.mean(z, axis=-1, keepdims=True)
    d = z - mu
    var = jnp.mean(d * d, axis=-1, keepdims=True)
    return d * lax.rsqrt(var + LN_EPS) * g + b


def _t5_bucket(dist):
    n = jnp.maximum(dist, 0)
    max_exact = N_BUCKETS // 2
    nf = jnp.maximum(n, 1).astype(jnp.float32)
    large = max_exact + (jnp.log(nf / max_exact) / math.log(MAX_DISTANCE / max_exact)
                         * (N_BUCKETS - max_exact)).astype(jnp.int32)
    large = jnp.minimum(large, N_BUCKETS - 1)
    return jnp.where(n < max_exact, n, large)


def _bucket_thresholds(seq):
    b = _t5_bucket(jnp.arange(seq, dtype=jnp.int32))
    j = jnp.arange(N_BUCKETS, dtype=jnp.int32)
    return jnp.sum((b[None, :] < j[:, None]).astype(jnp.int32), axis=1).astype(jnp.int32)


def _tables_kernel(thr_ref, rb_ref, fc_ref, dn_ref, *, nc):
    h = pl.program_id(0)
    base = h * N_BUCKETS
    far = rb_ref[base + N_BUCKETS - 1]

    def bias_of(dist):
        val = jnp.full(dist.shape, far, F32)
        for j in range(N_BUCKETS - 1, 0, -1):
            val = jnp.where(dist < thr_ref[j], rb_ref[base + j - 1], val)
        return val

    def tile_values(dist, dmin, dmax, shift):
        if dmax < 0:
            return jnp.full(dist.shape, NEG_INF, F32)
        if dmin >= MAX_DISTANCE:
            return jnp.full(dist.shape, far - shift, F32)
        return jnp.where(dist >= 0, bias_of(dist) - shift, NEG_INF)

    ql = lax.broadcasted_iota(jnp.int32, (Q_BLOCK, LANES), 0)
    ln = lax.broadcasted_iota(jnp.int32, (Q_BLOCK, LANES), 1)
    top = Q_BLOCK - 1
    for c in range(2 * nc // Q_BLOCK):
        off = c * Q_BLOCK - (nc - 8)
        dist = ln - CMP_STRIDE * (ql + off) - (CMP_BLOCK - 1)
        dmin, dmax = -CMP_STRIDE * (top + off) - (CMP_BLOCK - 1), top - CMP_STRIDE * off - (CMP_BLOCK - 1)
        fc_ref[0, c * Q_BLOCK:(c + 1) * Q_BLOCK, :] = tile_values(dist, dmin, dmax, 0.0)
    for c in range(DN_WIDTH // LANES):
        off = NEAR_BACK - c * LANES
        dist = ql + off - ln
        dn_ref[0, :, c * LANES:(c + 1) * LANES] = tile_values(dist, off - top, off + top, far)


def _bias_tables(thr, rb_flat, nc):
    return pl.pallas_call(
        functools.partial(_tables_kernel, nc=nc),
        out_shape=(jax.ShapeDtypeStruct((HEADS, 2 * nc, Q_BLOCK), F32),
                   jax.ShapeDtypeStruct((HEADS, Q_BLOCK, DN_WIDTH), F32)),
        grid=(HEADS,),
        in_specs=[pl.BlockSpec(memory_space=pltpu.SMEM), pl.BlockSpec(memory_space=pltpu.SMEM)],
        out_specs=(pl.BlockSpec((1, 2 * nc, Q_BLOCK), lambda h: (h, 0, 0)),
                   pl.BlockSpec((1, Q_BLOCK, DN_WIDTH), lambda h: (h, 0, 0))),
        name="bias_tables",
    )(thr, rb_flat)


def _proj_kernel(x_ref, wuv_ref, wqkv_ref, lng_ref, lnb_ref, ws_ref, bs_ref,
                 a_ref, q_ref, cmp_ref, kaug_ref, vaug_ref, wink_ref, winv_ref, gate_ref, cmp_sc, *, tm, seq):
    xb = x_ref[...].astype(BF16)
    uv = jax.nn.gelu(_dot(xb, wuv_ref[...]))
    u = uv[:, :A_WIDTH]
    v = _layer_norm(uv[:, A_WIDTH:], lng_ref[...], lnb_ref[...]).astype(BF16)
    row = lax.broadcasted_iota(jnp.int32, (CHUNK, CHUNK), 0)
    col = lax.broadcasted_iota(jnp.int32, (CHUNK, CHUNK), 1)
    gd = A_WIDTH // A_GROUPS
    for g in range(A_GROUPS):
        w = jnp.where(col <= row, ws_ref[g], 0.0).astype(BF16)
        for c in range(tm // CHUNK):
            rs = slice(c * CHUNK, (c + 1) * CHUNK)
            cs = slice(g * gd, (g + 1) * gd)
            s = _dot(w, v[rs, cs]) + bs_ref[g]
            a_ref[rs, cs] = (u[rs, cs] * s).astype(BF16)

    h = _dot(xb, wqkv_ref[...])
    q_ref[...] = (h[:, 0:512] * (HEAD_DIM ** -0.5)).astype(BF16)
    for half in range(2):
        cmp_sc[half] = h[:, 512 + half * LANES:512 + (half + 1) * LANES]
        for l in range(CMP_STRIDE):
            rows = cmp_sc[half, pl.ds(l, tm // CMP_STRIDE, stride=CMP_STRIDE), :]
            cmp_ref[:, l * 256 + half * LANES:l * 256 + (half + 1) * LANES] = rows.astype(BF16)
    kaug_ref[:, 0:LANES] = h[:, 768:896].astype(BF16)
    pos = (pl.program_id(0) * tm) % seq + lax.broadcasted_iota(jnp.int32, (tm, LANES), 0)
    blk = lax.broadcasted_iota(jnp.int32, (tm, LANES), 1)
    kaug_ref[:, LANES:2 * LANES] = jnp.where(pos // SEL_BLOCK == blk, 1.0, 0.0).astype(BF16)
    ones = jnp.ones((tm, LANES), BF16)
    vaug_ref[:, 0:LANES] = h[:, 896:1024].astype(BF16)
    vaug_ref[:, LANES:2 * LANES] = ones
    wink_ref[...] = h[:, 1024:1152].astype(BF16)
    winv_ref[:, 0:LANES] = h[:, 1152:1280].astype(BF16)
    winv_ref[:, LANES:2 * LANES] = ones
    gate_ref[...] = jax.nn.sigmoid(h[:, 1280:1408])


def _proj(x2d, wuv, wqkv, lng, lnb, ws, bs, *, seq, tm=512):
    t = x2d.shape[0]
    row = lambda i: (i, 0)
    const2 = lambda i: (0, 0)
    const3 = lambda i: (0, 0, 0)
    outs = [(1, A_WIDTH, BF16), (1, 512, BF16), (CMP_STRIDE, CMP_STRIDE * 256, BF16), (1, 256, BF16),
            (1, 256, BF16), (1, 128, BF16), (1, 256, BF16), (1, 128, F32)]
    return pl.pallas_call(
        functools.partial(_proj_kernel, tm=tm, seq=seq),
        out_shape=tuple(jax.ShapeDtypeStruct((t // r, w), dt) for r, w, dt in outs),
        grid=(t // tm,),
        in_specs=[pl.BlockSpec((tm, D_MODEL), row),
                  pl.BlockSpec(wuv.shape, const2),
                  pl.BlockSpec(wqkv.shape, const2),
                  pl.BlockSpec(lng.shape, const2),
                  pl.BlockSpec(lnb.shape, const2),
                  pl.BlockSpec(ws.shape, const3),
                  pl.BlockSpec(bs.shape, const3)],
        out_specs=tuple(pl.BlockSpec((tm // r, w), row) for r, w, _ in outs),
        scratch_shapes=[pltpu.VMEM((2, tm, LANES), F32)],
        compiler_params=pltpu.CompilerParams(dimension_semantics=("parallel",), vmem_limit_bytes=VMEM_LIMIT),
        name="in_proj_gmlp",
    )(x2d, wuv, wqkv, lng, lnb, ws, bs)


def _compress_kernel(r_ref, wa_ref, wb_ref, pea_ref, peb_ref, w2_ref, kc_ref, vct_ref):
    r = r_ref[0]
    top = _dot(r, wa_ref[...])
    bot = _dot(r, wb_ref[...])
    pe = _dot(pea_ref[...], wa_ref[...]) + _dot(peb_ref[...], wb_ref[...])
    hid = top + pltpu.roll(bot, bot.shape[0] - 1, 0) + pe[0:1]
    out = _dot(jax.nn.gelu(hid).astype(BF16), w2_ref[...])
    kc_ref[0] = out[:, 0:LANES].astype(BF16)
    vct_ref[0] = out[:, LANES:2 * LANES].T.astype(BF16)


def _compress(cmp3, wa, wb, pea, peb, w2):
    b, nc, width = cmp3.shape
    const2 = lambda i: (0, 0)
    return pl.pallas_call(
        _compress_kernel,
        out_shape=(jax.ShapeDtypeStruct((b, nc, LANES), BF16), jax.ShapeDtypeStruct((b, LANES, nc), BF16)),
        grid=(b,),
        in_specs=[pl.BlockSpec((1, nc, width), lambda i: (i, 0, 0)),
                  pl.BlockSpec(wa.shape, const2), pl.BlockSpec(wb.shape, const2),
                  pl.BlockSpec(pea.shape, const2), pl.BlockSpec(peb.shape, const2),
                  pl.BlockSpec(w2.shape, const2)],
        out_specs=(pl.BlockSpec((1, nc, LANES), lambda i: (i, 0, 0)),
                   pl.BlockSpec((1, LANES, nc), lambda i: (i, 0, 0))),
        compiler_params=pltpu.CompilerParams(dimension_semantics=("parallel",), vmem_limit_bytes=VMEM_LIMIT),
        name="kv_compress",
    )(cmp3, wa, wb, pea, peb, w2)


def _softmax_tile(s, m_sc):
    cols = [s[:, j * LANES:(j + 1) * LANES] for j in range(s.shape[1] // LANES)]
    m_old = m_sc[...]
    m_new = jnp.maximum(m_old, jnp.max(functools.reduce(jnp.maximum, cols), axis=-1, keepdims=True))
    m_sc[...] = m_new
    p = jnp.concatenate([jnp.exp(c - m_new) for c in cols], axis=1).astype(BF16)
    return p, jnp.exp(m_old - m_new)


def _scale_both(acc, alpha):
    return jnp.concatenate([acc[:, :LANES] * alpha, acc[:, LANES:] * alpha], axis=1)


def _nsa_kernel(q_ref, gate_ref, kc_ref, vct_ref, kaug_ref, vaug_ref, wink_ref, winv_ref, fc_ref, dn_ref, ovt_ref,
                o_ref, qa_all, s_all, p_all, m_all, acc_all, *, nc, n_tiles):
    ib = pl.program_id(1)
    t0 = ib * Q_BLOCK
    lane = lax.broadcasted_iota(jnp.int32, (Q_BLOCK, LANES), 1)
    qrow = lax.broadcasted_iota(jnp.int32, (Q_BLOCK, LANES), 0)
    q = q_ref[...]
    gates = gate_ref[...]
    kc = kc_ref[0]
    vct = vct_ref[0]
    ovt = ovt_ref[...]

    def stack_heads(fn):
        return jnp.concatenate([fn(g) for g in range(GROUP)], axis=0)

    near0 = 2 * ib - NEAR_BACK // SEL_BLOCK
    n_far = jnp.maximum(ib - 4, 0) // 4
    n_pairs = (n_far + 1) // 2
    kstart = jnp.maximum(t0 - NEAR_BACK, 0)
    j0 = pl.multiple_of(kstart - (t0 - NEAR_BACK), LANES)
    kstart = pl.multiple_of(kstart, LANES)
    wstart = jnp.maximum(t0 - WINDOW, 0)
    jw = pl.multiple_of(wstart - (t0 - NEAR_BACK), LANES)
    wstart = pl.multiple_of(wstart, LANES)
    win_edge = jnp.where((lane > qrow) | (t0 < WINDOW), 0.0, NEG_INF)

    fc_row = pl.multiple_of((nc - 8) - 8 * ib, 8)
    qcol = lax.broadcasted_iota(jnp.int32, (1, ROWS), 1) % Q_BLOCK
    has_cmp = jnp.where(t0 + qcol >= CMP_BLOCK - 1, 1.0, 0.0)
    qhs, o_cts, imps = [], [], []
    for hkv in range(KV_HEADS):
        mine = (lane >= HEAD_DIM) if hkv else (lane < HEAD_DIM)
        qh = stack_heads(lambda g: jnp.where(mine, q[:, g * LANES:(g + 1) * LANES], 0.0).astype(BF16))
        lct = _dot_nt(kc, qh)
        lct = lct + jnp.concatenate([fc_ref[hkv * GROUP + g, pl.ds(fc_row, nc), :] for g in range(GROUP)], axis=1)
        e = jnp.exp(lct - jnp.max(lct, axis=0, keepdims=True))
        pct = e * (has_cmp / jnp.sum(e, axis=0, keepdims=True))
        o_cts.append(_dot(vct, pct.astype(BF16)))
        psum = pct[:, 0:LANES] + pct[:, LANES:2 * LANES] + pct[:, 2 * LANES:3 * LANES] + pct[:, 3 * LANES:]
        p_hi = psum.astype(BF16)
        p_lo = (psum - p_hi.astype(F32)).astype(BF16)
        imps.append(_dot(ovt, p_hi) + _dot(ovt, p_lo))
        qhs.append(qh)

    blk = lax.broadcasted_iota(jnp.int32, (LANES, KV_HEADS * Q_BLOCK), 0)
    col = lax.broadcasted_iota(jnp.int32, (LANES, KV_HEADS * Q_BLOCK), 1)
    jq = 2 * ib + jnp.where(col % Q_BLOCK >= SEL_BLOCK, 1, 0)
    forced = (blk == 0) | (blk == jq) | (blk == jq - 1)
    cand = jnp.where(forced, -3e38, jnp.where(blk > jq, NEG_INF, jnp.concatenate(imps, axis=1)))
    blk_f = blk.astype(F32)

    def pick_one(_, carry):
        cur, chosen = carry
        mx = jnp.max(cur, axis=0, keepdims=True)
        first = jnp.min(jnp.where(cur == mx, blk_f, float(LANES)), axis=0, keepdims=True)
        pick = blk_f == first
        return jnp.where(pick, -3e38, cur), jnp.where(pick, 1.0, chosen)

    _, sel_t = lax.fori_loop(0, N_SELECT - 3, pick_one, (cand, jnp.where(forced, 1.0, 0.0)), unroll=True)

    def window_branch(hkv):
        sw = _dot_nt(qhs[hkv], wink_ref[pl.ds(wstart, WIN_KEYS), :])
        sw = sw + stack_heads(lambda g: dn_ref[hkv * GROUP + g, :, pl.ds(jw, WIN_KEYS)])
        sw = jnp.concatenate([sw[:, :LANES] + jnp.concatenate([win_edge] * GROUP, axis=0), sw[:, LANES:]], axis=1)
        pw = jnp.exp(sw - jnp.max(sw, axis=-1, keepdims=True)).astype(BF16)
        ow = _dot(pw, winv_ref[pl.ds(wstart, WIN_KEYS), :])
        return ow[:, :LANES] / ow[:, LANES:]

    o_w_first = window_branch(0)

    outs = []
    for hkv in range(KV_HEADS):
        qa_sc, s_buf, p_buf = qa_all.at[hkv], s_all.at[hkv], p_all.at[hkv]
        m_sc, acc_sc = m_all.at[hkv], acc_all.at[hkv]
        qh = qhs[hkv]
        sel = sel_t[:, hkv * Q_BLOCK:(hkv + 1) * Q_BLOCK].T

        m_near = jnp.where(sel > 0, 0.0, NEG_INF).astype(BF16)
        m_far = jnp.where((sel > 0) & (lane < near0), 0.0, NEG_INF).astype(BF16)
        qa_sc[0, :, 0:LANES] = qh
        qa_sc[0, :, LANES:2 * LANES] = jnp.concatenate([m_far] * GROUP, axis=0)
        qa_sc[1, :, 0:LANES] = qh
        qa_sc[1, :, LANES:2 * LANES] = jnp.concatenate([m_near] * GROUP, axis=0)
        m_sc[...] = jnp.full(m_sc.shape, -jnp.inf, F32)
        acc_sc[...] = jnp.zeros(acc_sc.shape, F32)
        p_buf[1] = jnp.zeros((ROWS, FAR_TILE), BF16)

        def tile_keys(tile):
            near_idx = tile - 2 * n_pairs
            ks = jnp.where(near_idx >= 0, kstart + near_idx * FAR_TILE, tile * FAR_TILE)
            ks = jnp.clip(ks, 0, (n_tiles - 1) * FAR_TILE)
            return pl.multiple_of(ks, LANES), jnp.where(near_idx >= 0, 1, 0)

        def logits(tile, slot):
            ks, variant = tile_keys(tile)
            s_buf[slot] = _dot_nt(qa_sc[variant], kaug_ref[pl.ds(ks, FAR_TILE), :])

        def values(tile, slot):
            ks, _ = tile_keys(tile)
            return _dot(p_buf[slot], vaug_ref[pl.ds(ks, FAR_TILE), :])

        def absorb(slot, pending, table_col=None):
            s = s_buf[slot]
            if table_col is not None:
                js = pl.multiple_of(table_col, LANES)
                s = s + stack_heads(lambda g: dn_ref[hkv * GROUP + g, :, pl.ds(js, FAR_TILE)])
            p, alpha = _softmax_tile(s, m_sc)
            p_buf[slot] = p
            acc_sc[...] = _scale_both(acc_sc[...] + pending, alpha)

        logits(0, 0)

        o_w = o_w_first if hkv == 0 else window_branch(hkv)

        def far_pair(i, carry):
            t = 2 * i
            logits(t + 1, 1)
            absorb(0, values(t - 1, 1))
            logits(t + 2, 0)
            absorb(1, values(t, 0))
            return carry

        lax.fori_loop(0, n_pairs, far_pair, 0)
        t = 2 * n_pairs
        logits(t + 1, 1)
        absorb(0, values(t - 1, 1), j0)
        absorb(1, values(t, 0), j0 + FAR_TILE)
        acc = acc_sc[...] + values(t + 1, 1)
        o_s = acc[:, :LANES] / acc[:, LANES:]

        per_head = []
        for g in range(GROUP):
            c = 3 * (hkv * GROUP + g)
            rs = slice(g * Q_BLOCK, (g + 1) * Q_BLOCK)
            o_c = o_cts[hkv][:, rs].T
            per_head.append(gates[:, c:c + 1] * o_c + gates[:, c + 1:c + 2] * o_s[rs]
                            + gates[:, c + 2:c + 3] * o_w[rs])
        outs.append(per_head)

    for g in range(GROUP):
        o_ref[:, g * LANES:(g + 1) * LANES] = jnp.where(lane < HEAD_DIM, outs[0][g], outs[1][g]).astype(BF16)


def _nsa(qs, gates, kc, vct, kaug, vaug, wink, winv, fc, dn, ovt, *, batch, seq):
    nc = seq // CMP_STRIDE
    nq = seq // Q_BLOCK
    qrow = lambda b, i: (b * nq + i, 0)
    per_batch = lambda b, i: (b, 0)
    const2 = lambda b, i: (0, 0)
    const3 = lambda b, i: (0, 0, 0)
    once = pl.Buffered(1)
    return pl.pallas_call(
        functools.partial(_nsa_kernel, nc=nc, n_tiles=seq // FAR_TILE),
        out_shape=jax.ShapeDtypeStruct((batch * seq, 4 * LANES), BF16),
        grid=(batch, nq),
        in_specs=[pl.BlockSpec((Q_BLOCK, 4 * LANES), qrow),
                  pl.BlockSpec((Q_BLOCK, LANES), qrow),
                  pl.BlockSpec((1, nc, LANES), lambda b, i: (b, 0, 0)),
                  pl.BlockSpec((1, LANES, nc), lambda b, i: (b, 0, 0)),
                  pl.BlockSpec((seq, 256), per_batch, pipeline_mode=once),
                  pl.BlockSpec((seq, 256), per_batch, pipeline_mode=once),
                  pl.BlockSpec((seq, LANES), per_batch, pipeline_mode=once),
                  pl.BlockSpec((seq, 256), per_batch, pipeline_mode=once),
                  pl.BlockSpec(fc.shape, const3, pipeline_mode=once),
                  pl.BlockSpec(dn.shape, const3, pipeline_mode=once),
                  pl.BlockSpec(ovt.shape, const2)],
        out_specs=pl.BlockSpec((Q_BLOCK, 4 * LANES), qrow),
        scratch_shapes=[pltpu.VMEM((KV_HEADS, 2, ROWS, 2 * LANES), BF16),
                        pltpu.VMEM((KV_HEADS, 2, ROWS, FAR_TILE), F32),
                        pltpu.VMEM((KV_HEADS, 2, ROWS, FAR_TILE), BF16),
                        pltpu.VMEM((KV_HEADS, ROWS, LANES), F32),
                        pltpu.VMEM((KV_HEADS, ROWS, 2 * LANES), F32)],
        compiler_params=pltpu.CompilerParams(dimension_semantics=("parallel", "arbitrary"),
                                             vmem_limit_bytes=VMEM_LIMIT),
        name="sparse_attention",
    )(qs, gates, kc, vct, kaug, vaug, wink, winv, fc, dn, ovt)


def _swiglu_norm(x, wg_ref, wu_ref, wd_ref, g_ref, beta_ref, acc_ref, hc):
    xb = x.astype(BF16)
    hidden = wg_ref.shape[1]
    for c in range(hidden // hc):
        cs = slice(c * hc, (c + 1) * hc)
        gate = _dot(xb, wg_ref[:, cs])
        up = _dot(xb, wu_ref[:, cs])
        part = _dot((jax.nn.silu(gate) * up).astype(BF16), wd_ref[cs, :])
        if c == 0:
            acc_ref[...] = part
        else:
            acc_ref[...] += part
    return _layer_norm(ALPHA * x + acc_ref[...], g_ref[...], beta_ref[...])


def _outproj_ffn_kernel(x_ref, a_ref, b_ref, wa_ref, wb_ref, g1_ref, beta1_ref,
                        wg_ref, wu_ref, wd_ref, g2_ref, beta2_ref, o_ref, acc_ref, *, hc):
    y = _dot(a_ref[...], wa_ref[...]) + _dot(b_ref[...], wb_ref[...])
    x1 = _layer_norm(ALPHA * x_ref[...] + y, g1_ref[...], beta1_ref[...])
    o_ref[...] = _swiglu_norm(x1, wg_ref, wu_ref, wd_ref, g2_ref, beta2_ref, acc_ref, hc)


def _outproj_ffn(x2d, a, b, wa, wb, g1, beta1, wg, wu, wd, g2, beta2, *, tm=512, hc=256):
    t = x2d.shape[0]
    row = lambda i: (i, 0)
    const2 = lambda i: (0, 0)
    consts = (wa, wb, g1, beta1, wg, wu, wd, g2, beta2)
    return pl.pallas_call(
        functools.partial(_outproj_ffn_kernel, hc=hc),
        out_shape=jax.ShapeDtypeStruct((t, D_MODEL), F32),
        grid=(t // tm,),
        in_specs=[pl.BlockSpec((tm, D_MODEL), row), pl.BlockSpec((tm, a.shape[1]), row),
                  pl.BlockSpec((tm, b.shape[1]), row)]
        + [pl.BlockSpec(c.shape, const2, pipeline_mode=pl.Buffered(1)) for c in consts],
        out_specs=pl.BlockSpec((tm, D_MODEL), row),
        scratch_shapes=[pltpu.VMEM((tm, D_MODEL), F32)],
        compiler_params=pltpu.CompilerParams(dimension_semantics=("parallel",), vmem_limit_bytes=VMEM_LIMIT),
        name="out_proj_swiglu_ffn_norm",
    )(x2d, a, b, *consts)


def _conv_kernel(x_ref, win_ref, bin_ref, dww_ref, dwb_ref, lng_ref, lnb_ref, wout_ref, bout_ref,
                 g_ref, beta_ref, wg_ref, wu_ref, wd_ref, g2_ref, beta2_ref, o_ref, buf_ref, cv_ref, *, tm, rc, hc):
    @pl.when(pl.program_id(1) == 0)
    def _():
        buf_ref[0:HALO, :] = jnp.zeros((HALO, D_MODEL), F32)

    x = x_ref[...]
    h = _dot(x.astype(BF16), win_ref[...]) + bin_ref[...]
    buf_ref[HALO:HALO + tm, :] = h[:, :D_MODEL] * jax.nn.sigmoid(h[:, D_MODEL:])

    lead = HALO - (CONV_WIDTH - 1)

    def conv_rows(r, carry):
        r0 = pl.multiple_of(r * rc, rc)
        for lc in range(D_MODEL // LANES):
            ls = slice(lc * LANES, (lc + 1) * LANES)
            acc = jnp.broadcast_to(dwb_ref[:, ls], (rc, LANES))
            za = buf_ref[pl.ds(r0, rc + HALO), ls]
            for b in range(8):
                base, shift = 8 * ((lead + b) // 8), (lead + b) % 8
                span = rc + (8 if shift else 0)
                yb = None
                for a, j in enumerate(range(b, CONV_WIDTH, 8)):
                    term = za[base + 8 * a:base + 8 * a + span] * dww_ref[j:j + 1, ls]
                    yb = term if yb is None else yb + term
                acc = acc + yb[shift:shift + rc]
            cv_ref[pl.ds(r0, rc), ls] = acc
        return carry

    lax.fori_loop(0, tm // rc, conv_rows, 0)
    buf_ref[0:HALO, :] = buf_ref[tm:tm + HALO, :]

    y = jax.nn.silu(_layer_norm(cv_ref[...], lng_ref[...], lnb_ref[...]))
    y = _dot(y.astype(BF16), wout_ref[...]) + bout_ref[...]
    x1 = _layer_norm(ALPHA * x + y, g_ref[...], beta_ref[...])
    o_ref[...] = _swiglu_norm(x1, wg_ref, wu_ref, wd_ref, g2_ref, beta2_ref, cv_ref, hc)


def _conv_ffn(x2d, w_in, b_in, dw_w, dw_b, ln_g, ln_b, w_out, b_out, g, beta, wg, wu, wd, g2, beta2, *,
              batch, seq, tm=512, rc=64, hc=256):
    nt = seq // tm
    row = lambda b, i: (b * nt + i, 0)
    const2 = lambda b, i: (0, 0)
    consts = (w_in, b_in, dw_w, dw_b, ln_g, ln_b, w_out, b_out, g, beta, wg, wu, wd, g2, beta2)
    return pl.pallas_call(
        functools.partial(_conv_kernel, tm=tm, rc=rc, hc=hc),
        out_shape=jax.ShapeDtypeStruct((batch * seq, D_MODEL), F32),
        grid=(batch, nt),
        in_specs=[pl.BlockSpec((tm, D_MODEL), row)]
        + [pl.BlockSpec(c.shape, const2, pipeline_mode=pl.Buffered(1)) for c in consts],
        out_specs=pl.BlockSpec((tm, D_MODEL), row),
        scratch_shapes=[pltpu.VMEM((HALO + tm, D_MODEL), F32), pltpu.VMEM((tm, D_MODEL), F32)],
        compiler_params=pltpu.CompilerParams(dimension_semantics=("parallel", "arbitrary"),
                                             vmem_limit_bytes=VMEM_LIMIT),
        name="conv_module_swiglu_ffn_norm",
    )(x2d, *consts)


def _head_perm():
    p = np.arange(HEADS * HEAD_DIM)
    g, half, d = p // LANES, (p % LANES) // HEAD_DIM, p % HEAD_DIM
    return (g + GROUP * half) * HEAD_DIM + d


def _compress_weights(pe_k, w1_k, w2_k, pe_v, w1_v, w2_v):
    def block_diag(compact):
        row_c = (jnp.arange(compact.shape[0]) % 256) // HEAD_DIM
        col_c = jnp.arange(256) // HEAD_DIM
        return jnp.where(row_c[:, None] == col_c[None, :], jnp.tile(compact, (1, 4)), 0.0).astype(BF16)

    w1k = w1_k.reshape(CMP_BLOCK, 1, HEAD_DIM, HEAD_DIM)
    w1v = w1_v.reshape(CMP_BLOCK, 1, HEAD_DIM, HEAD_DIM)
    w1x = block_diag(jnp.concatenate([w1k, w1k, w1v, w1v], axis=1).reshape(CMP_BLOCK * 256, HEAD_DIM))
    w2x = block_diag(jnp.concatenate([w2_k, w2_k, w2_v, w2_v], axis=0))
    half = CMP_STRIDE * 256
    pe = jnp.stack([pe_k, pe_v])
    pex = jnp.broadcast_to(pe.transpose(1, 0, 2)[:, :, None, :], (CMP_BLOCK, 2, KV_HEADS, HEAD_DIM))
    pex = pex.reshape(2, half)
    pea = jnp.broadcast_to(pex[0:1], (8, half)).astype(BF16)
    peb = jnp.broadcast_to(pex[1:2], (8, half)).astype(BF16)
    return w1x[:half], w1x[half:], pea, peb, w2x


def _overlap_matrix(nc):
    c0 = np.arange(nc)[None, :] * CMP_STRIDE
    s0 = np.arange(LANES)[:, None] * SEL_BLOCK
    ov = (c0 < s0 + SEL_BLOCK) & (c0 + CMP_BLOCK > s0) & (np.arange(nc)[None, :] < nc - 1)
    return jnp.asarray(ov, dtype=BF16)


def _even_layer(x2d, rel_bias, w_in, w_out, ln_g, ln_b, w_s, b_s, pe_k, w1_k, w2_k, pe_v, w1_v, w2_v,
                norm_g, norm_b, *, batch, seq):
    nc = seq // CMP_STRIDE
    assert seq % FAR_TILE == 0 and seq >= NEAR and seq // SEL_BLOCK <= LANES
    perm = _head_perm()
    qw = HEADS * HEAD_DIM
    o = 2 * A_WIDTH
    wuv = w_in[:, :o].astype(BF16)
    wq = w_in[:, o:o + qw][:, perm]
    wkv = w_in[:, o + qw:o + qw + 768]
    wgt = jnp.pad(w_in[:, o + qw + 768:], ((0, 0), (0, LANES - 3 * HEADS)))
    wqkv = jnp.concatenate([wq, wkv, wgt], axis=1).astype(BF16)
    bs = jnp.broadcast_to(b_s[:, :, None], (A_GROUPS, CHUNK, A_WIDTH // A_GROUPS))
    a_out, qs, cmp2, kaug, vaug, wink, winv, gates = _proj(
        x2d, wuv, wqkv, ln_g[None, :], ln_b[None, :], w_s, bs, seq=seq)

    kc, vct = _compress(cmp2.reshape(batch, nc, CMP_STRIDE * 256),
                    *_compress_weights(pe_k, w1_k, w2_k, pe_v, w1_v, w2_v))

    fc, dn = _bias_tables(_bucket_thresholds(seq), rel_bias.T.reshape(-1), nc)
    b_out = _nsa(qs, gates, kc, vct, kaug, vaug, wink, winv, fc, dn, _overlap_matrix(nc), batch=batch, seq=seq)

    wo_a = w_out[:A_WIDTH].astype(BF16)
    wo_b = w_out[A_WIDTH:][perm].astype(BF16)
    return a_out, b_out, wo_a, wo_b, norm_g[None, :], norm_b[None, :]


def kernel(x, rel_bias, hyb_w_in, hyb_w_out, gmlp_ln_g, gmlp_ln_b, gmlp_w_s, gmlp_b_s, cmp_pe_k, cmp_w1_k, cmp_w2_k, cmp_pe_v, cmp_w1_v, cmp_w2_v, conv_w_in, conv_b_in, conv_dw_w, conv_dw_b, conv_ln_g, conv_ln_b, conv_w_out, conv_b_out, ffn_w_gate, ffn_w_up, ffn_w_down, norm_mix_g, norm_mix_b, norm_ffn_g, norm_ffn_b):
    batch, seq, d = x.shape
    h = x.reshape(batch * seq, d)
    for layer in range(DEPTH):
        i = layer // 2
        ffn = (ffn_w_gate[layer].astype(BF16), ffn_w_up[layer].astype(BF16), ffn_w_down[layer].astype(BF16),
               norm_ffn_g[layer][None, :], norm_ffn_b[layer][None, :])
        if layer % 2 == 0:
            mixer = _even_layer(h, rel_bias, hyb_w_in[i], hyb_w_out[i], gmlp_ln_g[i], gmlp_ln_b[i],
                                gmlp_w_s[i], gmlp_b_s[i], cmp_pe_k[i], cmp_w1_k[i], cmp_w2_k[i],
                                cmp_pe_v[i], cmp_w1_v[i], cmp_w2_v[i],
                                norm_mix_g[layer], norm_mix_b[layer], batch=batch, seq=seq)
            h = _outproj_ffn(h, *mixer, *ffn)
        else:
            h = _conv_ffn(h, conv_w_in[i].astype(BF16), conv_b_in[i][None, :], conv_dw_w[i], conv_dw_b[i][None, :],
                          conv_ln_g[i][None, :], conv_ln_b[i][None, :], conv_w_out[i].astype(BF16),
                          conv_b_out[i][None, :], norm_mix_g[layer][None, :], norm_mix_b[layer][None, :],
                          *ffn, batch=batch, seq=seq)
    return h.reshape(batch, seq, d)
```

```python
import functools
import math

import numpy as np
import jax
import jax.numpy as jnp
from jax import lax
from jax.experimental import pallas as pl
from jax.experimental.pallas import tpu as pltpu

F32 = jnp.float32
BF16 = jnp.bfloat16

D_MODEL = 1024
DEPTH = 2
ALPHA = (2 * DEPTH) ** 0.25
CHUNK = 128
A_WIDTH = D_MODEL // 2
A_GROUPS = 4
HEADS = 8
KV_HEADS = 2
GROUP = HEADS // KV_HEADS
HEAD_DIM = (D_MODEL // 2) // HEADS
CMP_STRIDE = 16
CMP_BLOCK = 32
SEL_BLOCK = 64
N_SELECT = 16
WINDOW = 512
Q_BLOCK = 128
N_BUCKETS = 32
MAX_DISTANCE = 1024
CONV_WIDTH = 31
NEG_INF = -1e30
LN_EPS = 1e-5

LANES = 128
VMEM_LIMIT = 56 * 1024 * 1024

ROWS = GROUP * Q_BLOCK
FAR_TILE = 512
NEAR_TILES = 8
NEAR = NEAR_TILES * Q_BLOCK
NEAR_BACK = NEAR - Q_BLOCK
WIN_KEYS = WINDOW + Q_BLOCK
DN_WIDTH = NEAR + NEAR_BACK
HALO = 32


def _dot(a, b):
    return jnp.dot(a, b, preferred_element_type=F32)


def _dot_nt(a, b):
    return lax.dot_general(a, b, (((1,), (1,)), ((), ())), preferred_element_type=F32)


def _layer_norm(z, g, b):
    mu = jnp.mean(z, axis=-1, keepdims=True)
    d = z - mu
    var = jnp.mean(d * d, axis=-1, keepdims=True)
    return d * lax.rsqrt(var + LN_EPS) * g + b


def _t5_bucket(dist):
    n = jnp.maximum(dist, 0)
    max_exact = N_BUCKETS // 2
    nf = jnp.maximum(n, 1).astype(jnp.float32)
    large = max_exact + (jnp.log(nf / max_exact) / math.log(MAX_DISTANCE / max_exact)
                         * (N_BUCKETS - max_exact)).astype(jnp.int32)
    large = jnp.minimum(large, N_BUCKETS - 1)
    return jnp.where(n < max_exact, n, large)


def _bucket_thresholds(seq):
    b = _t5_bucket(jnp.arange(seq, dtype=jnp.int32))
    j = jnp.arange(N_BUCKETS, dtype=jnp.int32)
    return jnp.sum((b[None, :] < j[:, None]).astype(jnp.int32), axis=1).astype(jnp.int32)


def _tables_kernel(thr_ref, rb_ref, fc_ref, dn_ref, *, nc):
    h = pl.program_id(0)
    base = h * N_BUCKETS
    far = rb_ref[base + N_BUCKETS - 1]

    def bias_of(dist):
        val = jnp.full(dist.shape, far, F32)
        for j in range(N_BUCKETS - 1, 0, -1):
            val = jnp.where(dist < thr_ref[j], rb_ref[base + j - 1], val)
        return val

    def tile_values(dist, dmin, dmax, shift):
        if dmax < 0:
            return jnp.full(dist.shape, NEG_INF, F32)
        if dmin >= MAX_DISTANCE:
            return jnp.full(dist.shape, far - shift, F32)
        return jnp.where(dist >= 0, bias_of(dist) - shift, NEG_INF)

    ql = lax.broadcasted_iota(jnp.int32, (Q_BLOCK, LANES), 0)
    ln = lax.broadcasted_iota(jnp.int32, (Q_BLOCK, LANES), 1)
    top = Q_BLOCK - 1
    for c in range(2 * nc // Q_BLOCK):
        off = c * Q_BLOCK - (nc - 8)
        dist = ln - CMP_STRIDE * (ql + off) - (CMP_BLOCK - 1)
        dmin, dmax = -CMP_STRIDE * (top + off) - (CMP_BLOCK - 1), top - CMP_STRIDE * off - (CMP_BLOCK - 1)
        fc_ref[0, c * Q_BLOCK:(c + 1) * Q_BLOCK, :] = tile_values(dist, dmin, dmax, 0.0)
    for c in range(DN_WIDTH // LANES):
        off = NEAR_BACK - c * LANES
        dist = ql + off - ln
        dn_ref[0, :, c * LANES:(c + 1) * LANES] = tile_values(dist, off - top, off + top, far)


def _bias_tables(thr, rb_flat, nc):
    return pl.pallas_call(
        functools.partial(_tables_kernel, nc=nc),
        out_shape=(jax.ShapeDtypeStruct((HEADS, 2 * nc, Q_BLOCK), F32),
                   jax.ShapeDtypeStruct((HEADS, Q_BLOCK, DN_WIDTH), F32)),
        grid=(HEADS,),
        in_specs=[pl.BlockSpec(memory_space=pltpu.SMEM), pl.BlockSpec(memory_space=pltpu.SMEM)],
        out_specs=(pl.BlockSpec((1, 2 * nc, Q_BLOCK), lambda h: (h, 0, 0)),
                   pl.BlockSpec((1, Q_BLOCK, DN_WIDTH), lambda h: (h, 0, 0))),
        name="bias_tables",
    )(thr, rb_flat)


def _proj_kernel(x_ref, wuv_ref, wqkv_ref, lng_ref, lnb_ref, ws_ref, bs_ref,
                 a_ref, q_ref, cmp_ref, kaug_ref, vaug_ref, wink_ref, winv_ref, gate_ref, cmp_sc, *, tm, seq):
    xb = x_ref[...].astype(BF16)
    uv = jax.nn.gelu(_dot(xb, wuv_ref[...]))
    u = uv[:, :A_WIDTH]
    v = _layer_norm(uv[:, A_WIDTH:], lng_ref[...], lnb_ref[...]).astype(BF16)
    row = lax.broadcasted_iota(jnp.int32, (CHUNK, CHUNK), 0)
    col = lax.broadcasted_iota(jnp.int32, (CHUNK, CHUNK), 1)
    gd = A_WIDTH // A_GROUPS
    for g in range(A_GROUPS):
        w = jnp.where(col <= row, ws_ref[g], 0.0).astype(BF16)
        for c in range(tm // CHUNK):
            rs = slice(c * CHUNK, (c + 1) * CHUNK)
            cs = slice(g * gd, (g + 1) * gd)
            s = _dot(w, v[rs, cs]) + bs_ref[g]
            a_ref[rs, cs] = (u[rs, cs] * s).astype(BF16)

    h = _dot(xb, wqkv_ref[...])
    q_ref[...] = (h[:, 0:512] * (HEAD_DIM ** -0.5)).astype(BF16)
    for half in range(2):
        cmp_sc[half] = h[:, 512 + half * LANES:512 + (half + 1) * LANES]
        for l in range(CMP_STRIDE):
            rows = cmp_sc[half, pl.ds(l, tm // CMP_STRIDE, stride=CMP_STRIDE), :]
            cmp_ref[:, l * 256 + half * LANES:l * 256 + (half + 1) * LANES] = rows.astype(BF16)
    kaug_ref[:, 0:LANES] = h[:, 768:896].astype(BF16)
    pos = (pl.program_id(0) * tm) % seq + lax.broadcasted_iota(jnp.int32, (tm, LANES), 0)
    blk = lax.broadcasted_iota(jnp.int32, (tm, LANES), 1)
    kaug_ref[:, LANES:2 * LANES] = jnp.where(pos // SEL_BLOCK == blk, 1.0, 0.0).astype(BF16)
    ones = jnp.ones((tm, LANES), BF16)
    vaug_ref[:, 0:LANES] = h[:, 896:1024].astype(BF16)
    vaug_ref[:, LANES:2 * LANES] = ones
    wink_ref[...] = h[:, 1024:1152].astype(BF16)
    winv_ref[:, 0:LANES] = h[:, 1152:1280].astype(BF16)
    winv_ref[:, LANES:2 * LANES] = ones
    gate_ref[...] = jax.nn.sigmoid(h[:, 1280:1408])


def _proj(x2d, wuv, wqkv, lng, lnb, ws, bs, *, seq, tm=512):
    t = x2d.shape[0]
    row = lambda i: (i, 0)
    const2 = lambda i: (0, 0)
    const3 = lambda i: (0, 0, 0)
    outs = [(1, A_WIDTH, BF16), (1, 512, BF16), (CMP_STRIDE, CMP_STRIDE * 256, BF16), (1, 256, BF16),
            (1, 256, BF16), (1, 128, BF16), (1, 256, BF16), (1, 128, F32)]
    return pl.pallas_call(
        functools.partial(_proj_kernel, tm=tm, seq=seq),
        out_shape=tuple(jax.ShapeDtypeStruct((t // r, w), dt) for r, w, dt in outs),
        grid=(t // tm,),
        in_specs=[pl.BlockSpec((tm, D_MODEL), row),
                  pl.BlockSpec(wuv.shape, const2),
                  pl.BlockSpec(wqkv.shape, const2),
                  pl.BlockSpec(lng.shape, const2),
                  pl.BlockSpec(lnb.shape, const2),
                  pl.BlockSpec(ws.shape, const3),
                  pl.BlockSpec(bs.shape, const3)],
        out_specs=tuple(pl.BlockSpec((tm // r, w), row) for r, w, _ in outs),
        scratch_shapes=[pltpu.VMEM((2, tm, LANES), F32)],
        compiler_params=pltpu.CompilerParams(dimension_semantics=("parallel",), vmem_limit_bytes=VMEM_LIMIT),
        name="in_proj_gmlp",
    )(x2d, wuv, wqkv, lng, lnb, ws, bs)


def _compress_kernel(r_ref, wa_ref, wb_ref, pea_ref, peb_ref, w2_ref, kc_ref, vct_ref):
    r = r_ref[0]
    top = _dot(r, wa_ref[...])
    bot = _dot(r, wb_ref[...])
    pe = _dot(pea_ref[...], wa_ref[...]) + _dot(peb_ref[...], wb_ref[...])
    hid = top + pltpu.roll(bot, bot.shape[0] - 1, 0) + pe[0:1]
    out = _dot(jax.nn.gelu(hid).astype(BF16), w2_ref[...])
    kc_ref[0] = out[:, 0:LANES].astype(BF16)
    vct_ref[0] = out[:, LANES:2 * LANES].T.astype(BF16)


def _compress(cmp3, wa, wb, pea, peb, w2):
    b, nc, width = cmp3.shape
    const2 = lambda i: (0, 0)
    return pl.pallas_call(
        _compress_kernel,
        out_shape=(jax.ShapeDtypeStruct((b, nc, LANES), BF16), jax.ShapeDtypeStruct((b, LANES, nc), BF16)),
        grid=(b,),
        in_specs=[pl.BlockSpec((1, nc, width), lambda i: (i, 0, 0)),
                  pl.BlockSpec(wa.shape, const2), pl.BlockSpec(wb.shape, const2),
                  pl.BlockSpec(pea.shape, const2), pl.BlockSpec(peb.shape, const2),
                  pl.BlockSpec(w2.shape, const2)],
        out_specs=(pl.BlockSpec((1, nc, LANES), lambda i: (i, 0, 0)),
                   pl.BlockSpec((1, LANES, nc), lambda i: (i, 0, 0))),
        compiler_params=pltpu.CompilerParams(dimension_semantics=("parallel",), vmem_limit_bytes=VMEM_LIMIT),
        name="kv_compress",
    )(cmp3, wa, wb, pea, peb, w2)


def _softmax_tile(s, m_sc):
    cols = [s[:, j * LANES:(j + 1) * LANES] for j in range(s.shape[1] // LANES)]
    m_old = m_sc[...]
    m_new = jnp.maximum(m_old, jnp.max(functools.reduce(jnp.maximum, cols), axis=-1, keepdims=True))
    m_sc[...] = m_new
    p = jnp.concatenate([jnp.exp(c - m_new) for c in cols], axis=1).astype(BF16)
    return p, jnp.exp(m_old - m_new)


def _scale_both(acc, alpha):
    return jnp.concatenate([acc[:, :LANES] * alpha, acc[:, LANES:] * alpha], axis=1)


def _nsa_kernel(q_ref, gate_ref, kc_ref, vct_ref, kaug_ref, vaug_ref, wink_ref, winv_ref, fc_ref, dn_ref, ovt_ref,
                o_ref, qa_all, s_all, p_all, m_all, acc_all, *, nc, n_tiles):
    ib = pl.program_id(1)
    t0 = ib * Q_BLOCK
    lane = lax.broadcasted_iota(jnp.int32, (Q_BLOCK, LANES), 1)
    qrow = lax.broadcasted_iota(jnp.int32, (Q_BLOCK, LANES), 0)
    q = q_ref[...]
    gates = gate_ref[...]
    kc = kc_ref[0]
    vct = vct_ref[0]
    ovt = ovt_ref[...]

    def stack_heads(fn):
        return jnp.concatenate([fn(g) for g in range(GROUP)], axis=0)

    near0 = 2 * ib - NEAR_BACK // SEL_BLOCK
    n_far = jnp.maximum(ib - 4, 0) // 4
    n_pairs = (n_far + 1) // 2
    kstart = jnp.maximum(t0 - NEAR_BACK, 0)
    j0 = pl.multiple_of(kstart - (t0 - NEAR_BACK), LANES)
    kstart = pl.multiple_of(kstart, LANES)
    wstart = jnp.maximum(t0 - WINDOW, 0)
    jw = pl.multiple_of(wstart - (t0 - NEAR_BACK), LANES)
    wstart = pl.multiple_of(wstart, LANES)
    win_edge = jnp.where((lane > qrow) | (t0 < WINDOW), 0.0, NEG_INF)

    fc_row = pl.multiple_of((nc - 8) - 8 * ib, 8)
    qcol = lax.broadcasted_iota(jnp.int32, (1, ROWS), 1) % Q_BLOCK
    has_cmp = jnp.where(t0 + qcol >= CMP_BLOCK - 1, 1.0, 0.0)
    qhs, o_cts, imps = [], [], []
    for hkv in range(KV_HEADS):
        mine = (lane >= HEAD_DIM) if hkv else (lane < HEAD_DIM)
        qh = stack_heads(lambda g: jnp.where(mine, q[:, g * LANES:(g + 1) * LANES], 0.0).astype(BF16))
        lct = _dot_nt(kc, qh)
        lct = lct + jnp.concatenate([fc_ref[hkv * GROUP + g, pl.ds(fc_row, nc), :] for g in range(GROUP)], axis=1)
        e = jnp.exp(lct - jnp.max(lct, axis=0, keepdims=True))
        pct = e * (has_cmp / jnp.sum(e, axis=0, keepdims=True))
        o_cts.append(_dot(vct, pct.astype(BF16)))
        psum = pct[:, 0:LANES] + pct[:, LANES:2 * LANES] + pct[:, 2 * LANES:3 * LANES] + pct[:, 3 * LANES:]
        p_hi = psum.astype(BF16)
        p_lo = (psum - p_hi.astype(F32)).astype(BF16)
        imps.append(_dot(ovt, p_hi) + _dot(ovt, p_lo))
        qhs.append(qh)

    blk = lax.broadcasted_iota(jnp.int32, (LANES, KV_HEADS * Q_BLOCK), 0)
    col = lax.broadcasted_iota(jnp.int32, (LANES, KV_HEADS * Q_BLOCK), 1)
    jq = 2 * ib + jnp.where(col % Q_BLOCK >= SEL_BLOCK, 1, 0)
    forced = (blk == 0) | (blk == jq) | (blk == jq - 1)
    cand = jnp.where(forced, -3e38, jnp.where(blk > jq, NEG_INF, jnp.concatenate(imps, axis=1)))
    blk_f = blk.astype(F32)

    def pick_one(_, carry):
        cur, chosen = carry
        mx = jnp.max(cur, axis=0, keepdims=True)
        first = jnp.min(jnp.where(cur == mx, blk_f, float(LANES)), axis=0, keepdims=True)
        pick = blk_f == first
        return jnp.where(pick, -3e38, cur), jnp.where(pick, 1.0, chosen)

    _, sel_t = lax.fori_loop(0, N_SELECT - 3, pick_one, (cand, jnp.where(forced, 1.0, 0.0)), unroll=True)

    def window_branch(hkv):
        sw = _dot_nt(qhs[hkv], wink_ref[pl.ds(wstart, WIN_KEYS), :])
        sw = sw + stack_heads(lambda g: dn_ref[hkv * GROUP + g, :, pl.ds(jw, WIN_KEYS)])
        sw = jnp.concatenate([sw[:, :LANES] + jnp.concatenate([win_edge] * GROUP, axis=0), sw[:, LANES:]], axis=1)
        pw = jnp.exp(sw - jnp.max(sw, axis=-1, keepdims=True)).astype(BF16)
        ow = _dot(pw, winv_ref[pl.ds(wstart, WIN_KEYS), :])
        return ow[:, :LANES] / ow[:, LANES:]

    o_w_first = window_branch(0)

    outs = []
    for hkv in range(KV_HEADS):
        qa_sc, s_buf, p_buf = qa_all.at[hkv], s_all.at[hkv], p_all.at[hkv]
        m_sc, acc_sc = m_all.at[hkv], acc_all.at[hkv]
        qh = qhs[hkv]
        sel = sel_t[:, hkv * Q_BLOCK:(hkv + 1) * Q_BLOCK].T

        m_near = jnp.where(sel > 0, 0.0, NEG_INF).astype(BF16)
        m_far = jnp.where((sel > 0) & (lane < near0), 0.0, NEG_INF).astype(BF16)
        qa_sc[0, :, 0:LANES] = qh
        qa_sc[0, :, LANES:2 * LANES] = jnp.concatenate([m_far] * GROUP, axis=0)
        qa_sc[1, :, 0:LANES] = qh
        qa_sc[1, :, LANES:2 * LANES] = jnp.concatenate([m_near] * GROUP, axis=0)
        m_sc[...] = jnp.full(m_sc.shape, -jnp.inf, F32)
        acc_sc[...] = jnp.zeros(acc_sc.shape, F32)
        p_buf[1] = jnp.zeros((ROWS, FAR_TILE), BF16)

        def tile_keys(tile):
            near_idx = tile - 2 * n_pairs
            ks = jnp.where(near_idx >= 0, kstart + near_idx * FAR_TILE, tile * FAR_TILE)
            ks = jnp.clip(ks, 0, (n_tiles - 1) * FAR_TILE)
            return pl.multiple_of(ks, LANES), jnp.where(near_idx >= 0, 1, 0)

        def logits(tile, slot):
            ks, variant = tile_keys(tile)
            s_buf[slot] = _dot_nt(qa_sc[variant], kaug_ref[pl.ds(ks, FAR_TILE), :])

        def values(tile, slot):
            ks, _ = tile_keys(tile)
            return _dot(p_buf[slot], vaug_ref[pl.ds(ks, FAR_TILE), :])

        def absorb(slot, pending, table_col=None):
            s = s_buf[slot]
            if table_col is not None:
                js = pl.multiple_of(table_col, LANES)
                s = s + stack_heads(lambda g: dn_ref[hkv * GROUP + g, :, pl.ds(js, FAR_TILE)])
            p, alpha = _softmax_tile(s, m_sc)
            p_buf[slot] = p
            acc_sc[...] = _scale_both(acc_sc[...] + pending, alpha)

        logits(0, 0)

        o_w = o_w_first if hkv == 0 else window_branch(hkv)

        def far_pair(i, carry):
            t = 2 * i
            logits(t + 1, 1)
            absorb(0, values(t - 1, 1))
            logits(t + 2, 0)
            absorb(1, values(t, 0))
            return carry

        lax.fori_loop(0, n_pairs, far_pair, 0)
        t = 2 * n_pairs
        logits(t + 1, 1)
        absorb(0, values(t - 1, 1), j0)
        absorb(1, values(t, 0), j0 + FAR_TILE)
        acc = acc_sc[...] + values(t + 1, 1)
        o_s = acc[:, :LANES] / acc[:, LANES:]

        per_head = []
        for g in range(GROUP):
            c = 3 * (hkv * GROUP + g)
            rs = slice(g * Q_BLOCK, (g + 1) * Q_BLOCK)
            o_c = o_cts[hkv][:, rs].T
            per_head.append(gates[:, c:c + 1] * o_c + gates[:, c + 1:c + 2] * o_s[rs]
                            + gates[:, c + 2:c + 3] * o_w[rs])
        outs.append(per_head)

    for g in range(GROUP):
        o_ref[:, g * LANES:(g + 1) * LANES] = jnp.where(lane < HEAD_DIM, outs[0][g], outs[1][g]).astype(BF16)


def _nsa(qs, gates, kc, vct, kaug, vaug, wink, winv, fc, dn, ovt, *, batch, seq):
    nc = seq // CMP_STRIDE
    nq = seq // Q_BLOCK
    qrow = lambda b, i: (b * nq + i, 0)
    per_batch = lambda b, i: (b, 0)
    const2 = lambda b, i: (0, 0)
    const3 = lambda b, i: (0, 0, 0)
    once = pl.Buffered(1)
    return pl.pallas_call(
        functools.partial(_nsa_kernel, nc=nc, n_tiles=seq // FAR_TILE),
        out_shape=jax.ShapeDtypeStruct((batch * seq, 4 * LANES), BF16),
        grid=(batch, nq),
        in_specs=[pl.BlockSpec((Q_BLOCK, 4 * LANES), qrow),
                  pl.BlockSpec((Q_BLOCK, LANES), qrow),
                  pl.BlockSpec((1, nc, LANES), lambda b, i: (b, 0, 0)),
                  pl.BlockSpec((1, LANES, nc), lambda b, i: (b, 0, 0)),
                  pl.BlockSpec((seq, 256), per_batch, pipeline_mode=once),
                  pl.BlockSpec((seq, 256), per_batch, pipeline_mode=once),
                  pl.BlockSpec((seq, LANES), per_batch, pipeline_mode=once),
                  pl.BlockSpec((seq, 256), per_batch, pipeline_mode=once),
                  pl.BlockSpec(fc.shape, const3, pipeline_mode=once),
                  pl.BlockSpec(dn.shape, const3, pipeline_mode=once),
                  pl.BlockSpec(ovt.shape, const2)],
        out_specs=pl.BlockSpec((Q_BLOCK, 4 * LANES), qrow),
        scratch_shapes=[pltpu.VMEM((KV_HEADS, 2, ROWS, 2 * LANES), BF16),
                        pltpu.VMEM((KV_HEADS, 2, ROWS, FAR_TILE), F32),
                        pltpu.VMEM((KV_HEADS, 2, ROWS, FAR_TILE), BF16),
                        pltpu.VMEM((KV_HEADS, ROWS, LANES), F32),
                        pltpu.VMEM((KV_HEADS, ROWS, 2 * LANES), F32)],
        compiler_params=pltpu.CompilerParams(dimension_semantics=("parallel", "arbitrary"),
                                             vmem_limit_bytes=VMEM_LIMIT),
        name="sparse_attention",
    )(qs, gates, kc, vct, kaug, vaug, wink, winv, fc, dn, ovt)


def _swiglu_norm(x, wg_ref, wu_ref, wd_ref, g_ref, beta_ref, acc_ref, hc):
    xb = x.astype(BF16)
    hidden = wg_ref.shape[1]
    for c in range(hidden // hc):
        cs = slice(c * hc, (c + 1) * hc)
        gate = _dot(xb, wg_ref[:, cs])
        up = _dot(xb, wu_ref[:, cs])
        part = _dot((jax.nn.silu(gate) * up).astype(BF16), wd_ref[cs, :])
        if c == 0:
            acc_ref[...] = part
        else:
            acc_ref[...] += part
    return _layer_norm(ALPHA * x + acc_ref[...], g_ref[...], beta_ref[...])


def _outproj_ffn_kernel(x_ref, a_ref, b_ref, wa_ref, wb_ref, g1_ref, beta1_ref,
                        wg_ref, wu_ref, wd_ref, g2_ref, beta2_ref, o_ref, acc_ref, *, hc):
    y = _dot(a_ref[...], wa_ref[...]) + _dot(b_ref[...], wb_ref[...])
    x1 = _layer_norm(ALPHA * x_ref[...] + y, g1_ref[...], beta1_ref[...])
    o_ref[...] = _swiglu_norm(x1, wg_ref, wu_ref, wd_ref, g2_ref, beta2_ref, acc_ref, hc)


def _outproj_ffn(x2d, a, b, wa, wb, g1, beta1, wg, wu, wd, g2, beta2, *, tm=512, hc=256):
    t = x2d.shape[0]
    row = lambda i: (i, 0)
    const2 = lambda i: (0, 0)
    consts = (wa, wb, g1, beta1, wg, wu, wd, g2, beta2)
    return pl.pallas_call(
        functools.partial(_outproj_ffn_kernel, hc=hc),
        out_shape=jax.ShapeDtypeStruct((t, D_MODEL), F32),
        grid=(t // tm,),
        in_specs=[pl.BlockSpec((tm, D_MODEL), row), pl.BlockSpec((tm, a.shape[1]), row),
                  pl.BlockSpec((tm, b.shape[1]), row)]
        + [pl.BlockSpec(c.shape, const2, pipeline_mode=pl.Buffered(1)) for c in consts],
        out_specs=pl.BlockSpec((tm, D_MODEL), row),
        scratch_shapes=[pltpu.VMEM((tm, D_MODEL), F32)],
        compiler_params=pltpu.CompilerParams(dimension_semantics=("parallel",), vmem_limit_bytes=VMEM_LIMIT),
        name="out_proj_swiglu_ffn_norm",
    )(x2d, a, b, *consts)


def _conv_kernel(x_ref, win_ref, bin_ref, dww_ref, dwb_ref, lng_ref, lnb_ref, wout_ref, bout_ref,
                 g_ref, beta_ref, wg_ref, wu_ref, wd_ref, g2_ref, beta2_ref, o_ref, buf_ref, cv_ref, *, tm, rc, hc):
    @pl.when(pl.program_id(1) == 0)
    def _():
        buf_ref[0:HALO, :] = jnp.zeros((HALO, D_MODEL), F32)

    x = x_ref[...]
    h = _dot(x.astype(BF16), win_ref[...]) + bin_ref[...]
    buf_ref[HALO:HALO + tm, :] = h[:, :D_MODEL] * jax.nn.sigmoid(h[:, D_MODEL:])

    lead = HALO - (CONV_WIDTH - 1)

    def conv_rows(r, carry):
        r0 = pl.multiple_of(r * rc, rc)
        for lc in range(D_MODEL // LANES):
            ls = slice(lc * LANES, (lc + 1) * LANES)
            acc = jnp.broadcast_to(dwb_ref[:, ls], (rc, LANES))
            za = buf_ref[pl.ds(r0, rc + HALO), ls]
            for b in range(8):
                base, shift = 8 * ((lead + b) // 8), (lead + b) % 8
                span = rc + (8 if shift else 0)
                yb = None
                for a, j in enumerate(range(b, CONV_WIDTH, 8)):
                    term = za[base + 8 * a:base + 8 * a + span] * dww_ref[j:j + 1, ls]
                    yb = term if yb is None else yb + term
                acc = acc + yb[shift:shift + rc]
            cv_ref[pl.ds(r0, rc), ls] = acc
        return carry

    lax.fori_loop(0, tm // rc, conv_rows, 0)
    buf_ref[0:HALO, :] = buf_ref[tm:tm + HALO, :]

    y = jax.nn.silu(_layer_norm(cv_ref[...], lng_ref[...], lnb_ref[...]))
    y = _dot(y.astype(BF16), wout_ref[...]) + bout_ref[...]
    x1 = _layer_norm(ALPHA * x + y, g_ref[...], beta_ref[...])
    o_ref[...] = _swiglu_norm(x1, wg_ref, wu_ref, wd_ref, g2_ref, beta2_ref, cv_ref, hc)


def _conv_ffn(x2d, w_in, b_in, dw_w, dw_b, ln_g, ln_b, w_out, b_out, g, beta, wg, wu, wd, g2, beta2, *,
              batch, seq, tm=512, rc=256, hc=256):
    nt = seq // tm
    row = lambda b, i: (b * nt + i, 0)
    const2 = lambda b, i: (0, 0)
    consts = (w_in, b_in, dw_w, dw_b, ln_g, ln_b, w_out, b_out, g, beta, wg, wu, wd, g2, beta2)
    return pl.pallas_call(
        functools.partial(_conv_kernel, tm=tm, rc=rc, hc=hc),
        out_shape=jax.ShapeDtypeStruct((batch * seq, D_MODEL), F32),
        grid=(batch, nt),
        in_specs=[pl.BlockSpec((tm, D_MODEL), row)]
        + [pl.BlockSpec(c.shape, const2, pipeline_mode=pl.Buffered(1)) for c in consts],
        out_specs=pl.BlockSpec((tm, D_MODEL), row),
        scratch_shapes=[pltpu.VMEM((HALO + tm, D_MODEL), F32), pltpu.VMEM((tm, D_MODEL), F32)],
        compiler_params=pltpu.CompilerParams(dimension_semantics=("parallel", "arbitrary"),
                                             vmem_limit_bytes=VMEM_LIMIT),
        name="conv_module_swiglu_ffn_norm",
    )(x2d, *consts)


def _head_perm():
    p = np.arange(HEADS * HEAD_DIM)
    g, half, d = p // LANES, (p % LANES) // HEAD_DIM, p % HEAD_DIM
    return (g + GROUP * half) * HEAD_DIM + d


def _compress_weights(pe_k, w1_k, w2_k, pe_v, w1_v, w2_v):
    def block_diag(compact):
        row_c = (jnp.arange(compact.shape[0]) % 256) // HEAD_DIM
        col_c = jnp.arange(256) // HEAD_DIM
        return jnp.where(row_c[:, None] == col_c[None, :], jnp.tile(compact, (1, 4)), 0.0).astype(BF16)

    w1k = w1_k.reshape(CMP_BLOCK, 1, HEAD_DIM, HEAD_DIM)
    w1v = w1_v.reshape(CMP_BLOCK, 1, HEAD_DIM, HEAD_DIM)
    w1x = block_diag(jnp.concatenate([w1k, w1k, w1v, w1v], axis=1).reshape(CMP_BLOCK * 256, HEAD_DIM))
    w2x = block_diag(jnp.concatenate([w2_k, w2_k, w2_v, w2_v], axis=0))
    half = CMP_STRIDE * 256
    pe = jnp.stack([pe_k, pe_v])
    pex = jnp.broadcast_to(pe.transpose(1, 0, 2)[:, :, None, :], (CMP_BLOCK, 2, KV_HEADS, HEAD_DIM))
    pex = pex.reshape(2, half)
    pea = jnp.broadcast_to(pex[0:1], (8, half)).astype(BF16)
    peb = jnp.broadcast_to(pex[1:2], (8, half)).astype(BF16)
    return w1x[:half], w1x[half:], pea, peb, w2x


def _overlap_matrix(nc):
    c0 = np.arange(nc)[None, :] * CMP_STRIDE
    s0 = np.arange(LANES)[:, None] * SEL_BLOCK
    ov = (c0 < s0 + SEL_BLOCK) & (c0 + CMP_BLOCK > s0) & (np.arange(nc)[None, :] < nc - 1)
    return jnp.asarray(ov, dtype=BF16)


def _even_layer(x2d, rel_bias, w_in, w_out, ln_g, ln_b, w_s, b_s, pe_k, w1_k, w2_k, pe_v, w1_v, w2_v,
                norm_g, norm_b, *, batch, seq):
    nc = seq // CMP_STRIDE
    assert seq % FAR_TILE == 0 and seq >= NEAR and seq // SEL_BLOCK <= LANES
    perm = _head_perm()
    qw = HEADS * HEAD_DIM
    o = 2 * A_WIDTH
    wuv = w_in[:, :o].astype(BF16)
    wq = w_in[:, o:o + qw][:, perm]
    wkv = w_in[:, o + qw:o + qw + 768]
    wgt = jnp.pad(w_in[:, o + qw + 768:], ((0, 0), (0, LANES - 3 * HEADS)))
    wqkv = jnp.concatenate([wq, wkv, wgt], axis=1).astype(BF16)
    bs = jnp.broadcast_to(b_s[:, :, None], (A_GROUPS, CHUNK, A_WIDTH // A_GROUPS))
    a_out, qs, cmp2, kaug, vaug, wink, winv, gates = _proj(
        x2d, wuv, wqkv, ln_g[None, :], ln_b[None, :], w_s, bs, seq=seq)

    kc, vct = _compress(cmp2.reshape(batch, nc, CMP_STRIDE * 256),
                    *_compress_weights(pe_k, w1_k, w2_k, pe_v, w1_v, w2_v))

    fc, dn = _bias_tables(_bucket_thresholds(seq), rel_bias.T.reshape(-1), nc)
    b_out = _nsa(qs, gates, kc, vct, kaug, vaug, wink, winv, fc, dn, _overlap_matrix(nc), batch=batch, seq=seq)

    wo_a = w_out[:A_WIDTH].astype(BF16)
    wo_b = w_out[A_WIDTH:][perm].astype(BF16)
    return a_out, b_out, wo_a, wo_b, norm_g[None, :], norm_b[None, :]


def kernel(x, rel_bias, hyb_w_in, hyb_w_out, gmlp_ln_g, gmlp_ln_b, gmlp_w_s, gmlp_b_s, cmp_pe_k, cmp_w1_k, cmp_w2_k, cmp_pe_v, cmp_w1_v, cmp_w2_v, conv_w_in, conv_b_in, conv_dw_w, conv_dw_b, conv_ln_g, conv_ln_b, conv_w_out, conv_b_out, ffn_w_gate, ffn_w_up, ffn_w_down, norm_mix_g, norm_mix_b, norm_ffn_g, norm_ffn_b):
    batch, seq, d = x.shape
    h = x.reshape(batch * seq, d)
    for layer in range(DEPTH):
        i = layer // 2
        ffn = (ffn_w_gate[layer].astype(BF16), ffn_w_up[layer].astype(BF16), ffn_w_down[layer].astype(BF16),
               norm_ffn_g[layer][None, :], norm_ffn_b[layer][None, :])
        if layer % 2 == 0:
            mixer = _even_layer(h, rel_bias, hyb_w_in[i], hyb_w_out[i], gmlp_ln_g[i], gmlp_ln_b[i],
                                gmlp_w_s[i], gmlp_b_s[i], cmp_pe_k[i], cmp_w1_k[i], cmp_w2_k[i],
                                cmp_pe_v[i], cmp_w1_v[i], cmp_w2_v[i],
                                norm_mix_g[layer], norm_mix_b[layer], batch=batch, seq=seq)
            h = _outproj_ffn(h, *mixer, *ffn)
        else:
            h = _conv_ffn(h, conv_w_in[i].astype(BF16), conv_b_in[i][None, :], conv_dw_w[i], conv_dw_b[i][None, :],
                          conv_ln_g[i][None, :], conv_ln_b[i][None, :], conv_w_out[i].astype(BF16),
                          conv_b_out[i][None, :], norm_mix_g[layer][None, :], norm_mix_b[layer][None, :],
                          *ffn, batch=batch, seq=seq)
    return h.reshape(batch, seq, d)
```

```python
import functools
import math

import numpy as np
import jax
import jax.numpy as jnp
from jax import lax
from jax.experimental import pallas as pl
from jax.experimental.pallas import tpu as pltpu

F32 = jnp.float32
BF16 = jnp.bfloat16

D_MODEL = 1024
DEPTH = 2
ALPHA = (2 * DEPTH) ** 0.25
CHUNK = 128
A_WIDTH = D_MODEL // 2
A_GROUPS = 4
HEADS = 8
KV_HEADS = 2
GROUP = HEADS // KV_HEADS
HEAD_DIM = (D_MODEL // 2) // HEADS
CMP_STRIDE = 16
CMP_BLOCK = 32
SEL_BLOCK = 64
N_SELECT = 16
WINDOW = 512
Q_BLOCK = 128
N_BUCKETS = 32
MAX_DISTANCE = 1024
CONV_WIDTH = 31
NEG_INF = -1e30
LN_EPS = 1e-5

LANES = 128
VMEM_LIMIT = 56 * 1024 * 1024

ROWS = GROUP * Q_BLOCK
FAR_TILE = 512
NEAR_TILES = 8
NEAR = NEAR_TILES * Q_BLOCK
NEAR_BACK = NEAR - Q_BLOCK
WIN_KEYS = WINDOW + Q_BLOCK
DN_WIDTH = NEAR + NEAR_BACK
HALO = 32


def _dot(a, b):
    return jnp.dot(a, b, preferred_element_type=F32)


def _dot_nt(a, b):
    return lax.dot_general(a, b, (((1,), (1,)), ((), ())), preferred_element_type=F32)


def _layer_norm(z, g, b):
    mu = jnp.mean(z, axis=-1, keepdims=True)
    d = z - mu
    var = jnp.mean(d * d, axis=-1, keepdims=True)
    return d * lax.rsqrt(var + LN_EPS) * g + b


def _t5_bucket(dist):
    n = jnp.maximum(dist, 0)
    max_exact = N_BUCKETS // 2
    nf = jnp.maximum(n, 1).astype(jnp.float32)
    large = max_exact + (jnp.log(nf / max_exact) / math.log(MAX_DISTANCE / max_exact)
                         * (N_BUCKETS - max_exact)).astype(jnp.int32)
    large = jnp.minimum(large, N_BUCKETS - 1)
    return jnp.where(n < max_exact, n, large)


def _bucket_thresholds(seq):
    b = _t5_bucket(jnp.arange(seq, dtype=jnp.int32))
    j = jnp.arange(N_BUCKETS, dtype=jnp.int32)
    return jnp.sum((b[None, :] < j[:, None]).astype(jnp.int32), axis=1).astype(jnp.int32)


def _tables_kernel(thr_ref, rb_ref, fc_ref, dn_ref, *, nc):
    h = pl.program_id(0)
    base = h * N_BUCKETS
    far = rb_ref[base + N_BUCKETS - 1]

    def bias_of(dist):
        val = jnp.full(dist.shape, far, F32)
        for j in range(N_BUCKETS - 1, 0, -1):
            val = jnp.where(dist < thr_ref[j], rb_ref[base + j - 1], val)
        return val

    def tile_values(dist, dmin, dmax, shift):
        if dmax < 0:
            return jnp.full(dist.shape, NEG_INF, F32)
        if dmin >= MAX_DISTANCE:
            return jnp.full(dist.shape, far - shift, F32)
        return jnp.where(dist >= 0, bias_of(dist) - shift, NEG_INF)

    ql = lax.broadcasted_iota(jnp.int32, (Q_BLOCK, LANES), 0)
    ln = lax.broadcasted_iota(jnp.int32, (Q_BLOCK, LANES), 1)
    top = Q_BLOCK - 1
    for c in range(2 * nc // Q_BLOCK):
        off = c * Q_BLOCK - (nc - 8)
        dist = ln - CMP_STRIDE * (ql + off) - (CMP_BLOCK - 1)
        dmin, dmax = -CMP_STRIDE * (top + off) - (CMP_BLOCK - 1), top - CMP_STRIDE * off - (CMP_BLOCK - 1)
        fc_ref[0, c * Q_BLOCK:(c + 1) * Q_BLOCK, :] = tile_values(dist, dmin, dmax, 0.0)
    for c in range(DN_WIDTH // LANES):
        off = NEAR_BACK - c * LANES
        dist = ql + off - ln
        dn_ref[0, :, c * LANES:(c + 1) * LANES] = tile_values(dist, off - top, off + top, far)


def _bias_tables(thr, rb_flat, nc):
    return pl.pallas_call(
        functools.partial(_tables_kernel, nc=nc),
        out_shape=(jax.ShapeDtypeStruct((HEADS, 2 * nc, Q_BLOCK), F32),
                   jax.ShapeDtypeStruct((HEADS, Q_BLOCK, DN_WIDTH), F32)),
        grid=(HEADS,),
        in_specs=[pl.BlockSpec(memory_space=pltpu.SMEM), pl.BlockSpec(memory_space=pltpu.SMEM)],
        out_specs=(pl.BlockSpec((1, 2 * nc, Q_BLOCK), lambda h: (h, 0, 0)),
                   pl.BlockSpec((1, Q_BLOCK, DN_WIDTH), lambda h: (h, 0, 0))),
        name="bias_tables",
    )(thr, rb_flat)


def _proj_kernel(x_ref, wuv_ref, wqkv_ref, lng_ref, lnb_ref, ws_ref, bs_ref,
                 a_ref, q_ref, cmp_ref, kaug_ref, vaug_ref, wink_ref, winv_ref, gate_ref, cmp_sc, *, tm, seq):
    xb = x_ref[...].astype(BF16)
    uv = jax.nn.gelu(_dot(xb, wuv_ref[...]))
    u = uv[:, :A_WIDTH]
    v = _layer_norm(uv[:, A_WIDTH:], lng_ref[...], lnb_ref[...]).astype(BF16)
    row = lax.broadcasted_iota(jnp.int32, (CHUNK, CHUNK), 0)
    col = lax.broadcasted_iota(jnp.int32, (CHUNK, CHUNK), 1)
    gd = A_WIDTH // A_GROUPS
    for g in range(A_GROUPS):
        w = jnp.where(col <= row, ws_ref[g], 0.0).astype(BF16)
        for c in range(tm // CHUNK):
            rs = slice(c * CHUNK, (c + 1) * CHUNK)
            cs = slice(g * gd, (g + 1) * gd)
            s = _dot(w, v[rs, cs]) + bs_ref[g]
            a_ref[rs, cs] = (u[rs, cs] * s).astype(BF16)

    h = _dot(xb, wqkv_ref[...])
    q_ref[...] = (h[:, 0:512] * (HEAD_DIM ** -0.5)).astype(BF16)
    for half in range(2):
        cmp_sc[half] = h[:, 512 + half * LANES:512 + (half + 1) * LANES]
        for l in range(CMP_STRIDE):
            rows = cmp_sc[half, pl.ds(l, tm // CMP_STRIDE, stride=CMP_STRIDE), :]
            cmp_ref[:, l * 256 + half * LANES:l * 256 + (half + 1) * LANES] = rows.astype(BF16)
    kaug_ref[:, 0:LANES] = h[:, 768:896].astype(BF16)
    pos = (pl.program_id(0) * tm) % seq + lax.broadcasted_iota(jnp.int32, (tm, LANES), 0)
    blk = lax.broadcasted_iota(jnp.int32, (tm, LANES), 1)
    kaug_ref[:, LANES:2 * LANES] = jnp.where(pos // SEL_BLOCK == blk, 1.0, 0.0).astype(BF16)
    ones = jnp.ones((tm, LANES), BF16)
    vaug_ref[:, 0:LANES] = h[:, 896:1024].astype(BF16)
    vaug_ref[:, LANES:2 * LANES] = ones
    wink_ref[...] = h[:, 1024:1152].astype(BF16)
    winv_ref[:, 0:LANES] = h[:, 1152:1280].astype(BF16)
    winv_ref[:, LANES:2 * LANES] = ones
    gate_ref[...] = jax.nn.sigmoid(h[:, 1280:1408])


def _proj(x2d, wuv, wqkv, lng, lnb, ws, bs, *, seq, tm=512):
    t = x2d.shape[0]
    row = lambda i: (i, 0)
    const2 = lambda i: (0, 0)
    const3 = lambda i: (0, 0, 0)
    outs = [(1, A_WIDTH, BF16), (1, 512, BF16), (CMP_STRIDE, CMP_STRIDE * 256, BF16), (1, 256, BF16),
            (1, 256, BF16), (1, 128, BF16), (1, 256, BF16), (1, 128, F32)]
    return pl.pallas_call(
        functools.partial(_proj_kernel, tm=tm, seq=seq),
        out_shape=tuple(jax.ShapeDtypeStruct((t // r, w), dt) for r, w, dt in outs),
        grid=(t // tm,),
        in_specs=[pl.BlockSpec((tm, D_MODEL), row),
                  pl.BlockSpec(wuv.shape, const2),
                  pl.BlockSpec(wqkv.shape, const2),
                  pl.BlockSpec(lng.shape, const2),
                  pl.BlockSpec(lnb.shape, const2),
                  pl.BlockSpec(ws.shape, const3),
                  pl.BlockSpec(bs.shape, const3)],
        out_specs=tuple(pl.BlockSpec((tm // r, w), row) for r, w, _ in outs),
        scratch_shapes=[pltpu.VMEM((2, tm, LANES), F32)],
        compiler_params=pltpu.CompilerParams(dimension_semantics=("parallel",), vmem_limit_bytes=VMEM_LIMIT),
        name="in_proj_gmlp",
    )(x2d, wuv, wqkv, lng, lnb, ws, bs)


def _compress_kernel(r_ref, wa_ref, wb_ref, pea_ref, peb_ref, w2_ref, kc_ref, vct_ref):
    r = r_ref[0]
    top = _dot(r, wa_ref[...])
    bot = _dot(r, wb_ref[...])
    pe = _dot(pea_ref[...], wa_ref[...]) + _dot(peb_ref[...], wb_ref[...])
    hid = top + pltpu.roll(bot, bot.shape[0] - 1, 0) + pe[0:1]
    out = _dot(jax.nn.gelu(hid).astype(BF16), w2_ref[...])
    kc_ref[0] = out[:, 0:LANES].astype(BF16)
    vct_ref[0] = out[:, LANES:2 * LANES].T.astype(BF16)


def _compress(cmp3, wa, wb, pea, peb, w2):
    b, nc, width = cmp3.shape
    const2 = lambda i: (0, 0)
    return pl.pallas_call(
        _compress_kernel,
        out_shape=(jax.ShapeDtypeStruct((b, nc, LANES), BF16), jax.ShapeDtypeStruct((b, LANES, nc), BF16)),
        grid=(b,),
        in_specs=[pl.BlockSpec((1, nc, width), lambda i: (i, 0, 0)),
                  pl.BlockSpec(wa.shape, const2), pl.BlockSpec(wb.shape, const2),
                  pl.BlockSpec(pea.shape, const2), pl.BlockSpec(peb.shape, const2),
                  pl.BlockSpec(w2.shape, const2)],
        out_specs=(pl.BlockSpec((1, nc, LANES), lambda i: (i, 0, 0)),
                   pl.BlockSpec((1, LANES, nc), lambda i: (i, 0, 0))),
        compiler_params=pltpu.CompilerParams(dimension_semantics=("parallel",), vmem_limit_bytes=VMEM_LIMIT),
        name="kv_compress",
    )(cmp3, wa, wb, pea, peb, w2)


def _softmax_tile(s, m_sc):
    cols = [s[:, j * LANES:(j + 1) * LANES] for j in range(s.shape[1] // LANES)]
    m_old = m_sc[...]
    m_new = jnp.maximum(m_old, jnp.max(functools.reduce(jnp.maximum, cols), axis=-1, keepdims=True))
    m_sc[...] = m_new
    p = jnp.concatenate([jnp.exp(c - m_new) for c in cols], axis=1).astype(BF16)
    return p, jnp.exp(m_old - m_new)


def _scale_both(acc, alpha):
    return jnp.concatenate([acc[:, :LANES] * alpha, acc[:, LANES:] * alpha], axis=1)


def _nsa_kernel(q_ref, gate_ref, kc_ref, vct_ref, kaug_ref, vaug_ref, wink_ref, winv_ref, fc_ref, dn_ref, ovt_ref,
                o_ref, qa_all, s_all, p_all, m_all, acc_all, *, nc, n_tiles):
    ib = pl.program_id(1)
    t0 = ib * Q_BLOCK
    lane = lax.broadcasted_iota(jnp.int32, (Q_BLOCK, LANES), 1)
    qrow = lax.broadcasted_iota(jnp.int32, (Q_BLOCK, LANES), 0)
    q = q_ref[...]
    gates = gate_ref[...]
    kc = kc_ref[0]
    vct = vct_ref[0]
    ovt = ovt_ref[...]

    def stack_heads(fn):
        return jnp.concatenate([fn(g) for g in range(GROUP)], axis=0)

    near0 = 2 * ib - NEAR_BACK // SEL_BLOCK
    n_far = jnp.maximum(ib - 4, 0) // 4
    n_pairs = (n_far + 1) // 2
    kstart = jnp.maximum(t0 - NEAR_BACK, 0)
    j0 = pl.multiple_of(kstart - (t0 - NEAR_BACK), LANES)
    kstart = pl.multiple_of(kstart, LANES)
    wstart = jnp.maximum(t0 - WINDOW, 0)
    jw = pl.multiple_of(wstart - (t0 - NEAR_BACK), LANES)
    wstart = pl.multiple_of(wstart, LANES)
    win_edge = jnp.where((lane > qrow) | (t0 < WINDOW), 0.0, NEG_INF)

    fc_row = pl.multiple_of((nc - 8) - 8 * ib, 8)
    qcol = lax.broadcasted_iota(jnp.int32, (1, ROWS), 1) % Q_BLOCK
    has_cmp = jnp.where(t0 + qcol >= CMP_BLOCK - 1, 1.0, 0.0)
    qhs, o_cts, imps = [], [], []
    for hkv in range(KV_HEADS):
        mine = (lane >= HEAD_DIM) if hkv else (lane < HEAD_DIM)
        qh = stack_heads(lambda g: jnp.where(mine, q[:, g * LANES:(g + 1) * LANES], 0.0).astype(BF16))
        lct = _dot_nt(kc, qh)
        lct = lct + jnp.concatenate([fc_ref[hkv * GROUP + g, pl.ds(fc_row, nc), :] for g in range(GROUP)], axis=1)
        e = jnp.exp(lct - jnp.max(lct, axis=0, keepdims=True))
        pct = e * (has_cmp / jnp.sum(e, axis=0, keepdims=True))
        o_cts.append(_dot(vct, pct.astype(BF16)))
        psum = pct[:, 0:LANES] + pct[:, LANES:2 * LANES] + pct[:, 2 * LANES:3 * LANES] + pct[:, 3 * LANES:]
        p_hi = psum.astype(BF16)
        p_lo = (psum - p_hi.astype(F32)).astype(BF16)
        imps.append(_dot(ovt, p_hi) + _dot(ovt, p_lo))
        qhs.append(qh)

    blk = lax.broadcasted_iota(jnp.int32, (LANES, KV_HEADS * Q_BLOCK), 0)
    col = lax.broadcasted_iota(jnp.int32, (LANES, KV_HEADS * Q_BLOCK), 1)
    jq = 2 * ib + jnp.where(col % Q_BLOCK >= SEL_BLOCK, 1, 0)
    forced = (blk == 0) | (blk == jq) | (blk == jq - 1)
    cand = jnp.where(forced, -3e38, jnp.where(blk > jq, NEG_INF, jnp.concatenate(imps, axis=1)))
    blk_f = blk.astype(F32)

    def pick_one(_, carry):
        cur, chosen = carry
        mx = jnp.max(cur, axis=0, keepdims=True)
        first = jnp.min(jnp.where(cur == mx, blk_f, float(LANES)), axis=0, keepdims=True)
        pick = blk_f == first
        return jnp.where(pick, -3e38, cur), jnp.where(pick, 1.0, chosen)

    _, sel_t = lax.fori_loop(0, N_SELECT - 3, pick_one, (cand, jnp.where(forced, 1.0, 0.0)), unroll=True)

    def window_branch(hkv):
        sw = _dot_nt(qhs[hkv], wink_ref[pl.ds(wstart, WIN_KEYS), :])
        sw = sw + stack_heads(lambda g: dn_ref[hkv * GROUP + g, :, pl.ds(jw, WIN_KEYS)])
        sw = jnp.concatenate([sw[:, :LANES] + jnp.concatenate([win_edge] * GROUP, axis=0), sw[:, LANES:]], axis=1)
        pw = jnp.exp(sw - jnp.max(sw, axis=-1, keepdims=True)).astype(BF16)
        ow = _dot(pw, winv_ref[pl.ds(wstart, WIN_KEYS), :])
        return ow[:, :LANES] / ow[:, LANES:]

    o_w_first = window_branch(0)

    outs = []
    for hkv in range(KV_HEADS):
        qa_sc, s_buf, p_buf = qa_all.at[hkv], s_all.at[hkv], p_all.at[hkv]
        m_sc, acc_sc = m_all.at[hkv], acc_all.at[hkv]
        qh = qhs[hkv]
        sel = sel_t[:, hkv * Q_BLOCK:(hkv + 1) * Q_BLOCK].T

        m_near = jnp.where(sel > 0, 0.0, NEG_INF).astype(BF16)
        m_far = jnp.where((sel > 0) & (lane < near0), 0.0, NEG_INF).astype(BF16)
        qa_sc[0, :, 0:LANES] = qh
        qa_sc[0, :, LANES:2 * LANES] = jnp.concatenate([m_far] * GROUP, axis=0)
        qa_sc[1, :, 0:LANES] = qh
        qa_sc[1, :, LANES:2 * LANES] = jnp.concatenate([m_near] * GROUP, axis=0)
        m_sc[...] = jnp.full(m_sc.shape, -jnp.inf, F32)
        acc_sc[...] = jnp.zeros(acc_sc.shape, F32)
        p_buf[1] = jnp.zeros((ROWS, FAR_TILE), BF16)

        def tile_keys(tile):
            near_idx = tile - 2 * n_pairs
            ks = jnp.where(near_idx >= 0, kstart + near_idx * FAR_TILE, tile * FAR_TILE)
            ks = jnp.clip(ks, 0, (n_tiles - 1) * FAR_TILE)
            return pl.multiple_of(ks, LANES), jnp.where(near_idx >= 0, 1, 0)

        def logits(tile, slot):
            ks, variant = tile_keys(tile)
            s_buf[slot] = _dot_nt(qa_sc[variant], kaug_ref[pl.ds(ks, FAR_TILE), :])

        def values(tile, slot):
            ks, _ = tile_keys(tile)
            return _dot(p_buf[slot], vaug_ref[pl.ds(ks, FAR_TILE), :])

        def absorb(slot, pending, table_col=None):
            s = s_buf[slot]
            if table_col is not None:
                js = pl.multiple_of(table_col, LANES)
                s = s + stack_heads(lambda g: dn_ref[hkv * GROUP + g, :, pl.ds(js, FAR_TILE)])
            p, alpha = _softmax_tile(s, m_sc)
            p_buf[slot] = p
            acc_sc[...] = _scale_both(acc_sc[...] + pending, alpha)

        logits(0, 0)

        o_w = o_w_first if hkv == 0 else window_branch(hkv)

        def far_pair(i, carry):
            t = 2 * i
            logits(t + 1, 1)
            absorb(0, values(t - 1, 1))
            logits(t + 2, 0)
            absorb(1, values(t, 0))
            return carry

        lax.fori_loop(0, n_pairs, far_pair, 0)
        t = 2 * n_pairs
        logits(t + 1, 1)
        absorb(0, values(t - 1, 1), j0)
        absorb(1, values(t, 0), j0 + FAR_TILE)
        acc = acc_sc[...] + values(t + 1, 1)
        o_s = acc[:, :LANES] / acc[:, LANES:]

        per_head = []
        for g in range(GROUP):
            c = 3 * (hkv * GROUP + g)
            rs = slice(g * Q_BLOCK, (g + 1) * Q_BLOCK)
            o_c = o_cts[hkv][:, rs].T
            per_head.append(gates[:, c:c + 1] * o_c + gates[:, c + 1:c + 2] * o_s[rs]
                            + gates[:, c + 2:c + 3] * o_w[rs])
        outs.append(per_head)

    for g in range(GROUP):
        o_ref[:, g * LANES:(g + 1) * LANES] = jnp.where(lane < HEAD_DIM, outs[0][g], outs[1][g]).astype(BF16)


def _nsa(qs, gates, kc, vct, kaug, vaug, wink, winv, fc, dn, ovt, *, batch, seq):
    nc = seq // CMP_STRIDE
    nq = seq // Q_BLOCK
    qrow = lambda b, i: (b * nq + i, 0)
    per_batch = lambda b, i: (b, 0)
    const2 = lambda b, i: (0, 0)
    const3 = lambda b, i: (0, 0, 0)
    once = pl.Buffered(1)
    return pl.pallas_call(
        functools.partial(_nsa_kernel, nc=nc, n_tiles=seq // FAR_TILE),
        out_shape=jax.ShapeDtypeStruct((batch * seq, 4 * LANES), BF16),
        grid=(batch, nq),
        in_specs=[pl.BlockSpec((Q_BLOCK, 4 * LANES), qrow),
                  pl.BlockSpec((Q_BLOCK, LANES), qrow),
                  pl.BlockSpec((1, nc, LANES), lambda b, i: (b, 0, 0)),
                  pl.BlockSpec((1, LANES, nc), lambda b, i: (b, 0, 0)),
                  pl.BlockSpec((seq, 256), per_batch, pipeline_mode=once),
                  pl.BlockSpec((seq, 256), per_batch, pipeline_mode=once),
                  pl.BlockSpec((seq, LANES), per_batch, pipeline_mode=once),
                  pl.BlockSpec((seq, 256), per_batch, pipeline_mode=once),
                  pl.BlockSpec(fc.shape, const3, pipeline_mode=once),
                  pl.BlockSpec(dn.shape, const3, pipeline_mode=once),
                  pl.BlockSpec(ovt.shape, const2)],
        out_specs=pl.BlockSpec((Q_BLOCK, 4 * LANES), qrow),
        scratch_shapes=[pltpu.VMEM((KV_HEADS, 2, ROWS, 2 * LANES), BF16),
                        pltpu.VMEM((KV_HEADS, 2, ROWS, FAR_TILE), F32),
                        pltpu.VMEM((KV_HEADS, 2, ROWS, FAR_TILE), BF16),
                        pltpu.VMEM((KV_HEADS, ROWS, LANES), F32),
                        pltpu.VMEM((KV_HEADS, ROWS, 2 * LANES), F32)],
        compiler_params=pltpu.CompilerParams(dimension_semantics=("parallel", "arbitrary"),
                                             vmem_limit_bytes=VMEM_LIMIT),
        name="sparse_attention",
    )(qs, gates, kc, vct, kaug, vaug, wink, winv, fc, dn, ovt)


def _swiglu_norm(x, wg_ref, wu_ref, wd_ref, g_ref, beta_ref, acc_ref, hc):
    xb = x.astype(BF16)
    hidden = wg_ref.shape[1]
    for c in range(hidden // hc):
        cs = slice(c * hc, (c + 1) * hc)
        gate = _dot(xb, wg_ref[:, cs])
        up = _dot(xb, wu_ref[:, cs])
        part = _dot((jax.nn.silu(gate) * up).astype(BF16), wd_ref[cs, :])
        if c == 0:
            acc_ref[...] = part
        else:
            acc_ref[...] += part
    return _layer_norm(ALPHA * x + acc_ref[...], g_ref[...], beta_ref[...])


def _outproj_ffn_kernel(x_ref, a_ref, b_ref, wa_ref, wb_ref, g1_ref, beta1_ref,
                        wg_hbm, wu_hbm, wd_hbm, g2_ref, beta2_ref, o_ref,
                        acc_ref, wg_ref, wu_ref, wd_ref, stage_col, stage_row, sem, *, hc):
    @pl.when(pl.program_id(0) == 0)
    def _():
        n = wg_ref.shape[1] // hc
        jobs = [(wg_hbm, wg_ref, True, c) for c in range(n)] + [(wu_hbm, wu_ref, True, c) for c in range(n)] \
            + [(wd_hbm, wd_ref, False, c) for c in range(n)]

        def chunk_copy(j):
            src, _, by_col, c = jobs[j]
            slot = j % 2
            if by_col:
                return pltpu.make_async_copy(src.at[:, pl.ds(c * hc, hc)], stage_col.at[slot], sem.at[slot])
            return pltpu.make_async_copy(src.at[pl.ds(c * hc, hc), :], stage_row.at[slot], sem.at[slot])

        chunk_copy(0).start()
        for j, (_, dst, by_col, c) in enumerate(jobs):
            if j + 1 < len(jobs):
                chunk_copy(j + 1).start()
            chunk_copy(j).wait()
            if by_col:
                dst[:, c * hc:(c + 1) * hc] = stage_col[j % 2].astype(BF16)
            else:
                dst[c * hc:(c + 1) * hc, :] = stage_row[j % 2].astype(BF16)

    y = _dot(a_ref[...], wa_ref[...]) + _dot(b_ref[...], wb_ref[...])
    x1 = _layer_norm(ALPHA * x_ref[...] + y, g1_ref[...], beta1_ref[...])
    o_ref[...] = _swiglu_norm(x1, wg_ref, wu_ref, wd_ref, g2_ref, beta2_ref, acc_ref, hc)


def _outproj_ffn(x2d, a, b, wa, wb, g1, beta1, wg, wu, wd, g2, beta2, *, tm=512, hc=256):
    t = x2d.shape[0]
    row = lambda i: (i, 0)
    const2 = lambda i: (0, 0)
    resident = lambda c: pl.BlockSpec(c.shape, const2, pipeline_mode=pl.Buffered(1))
    in_hbm = pl.BlockSpec(memory_space=pl.ANY)
    hidden = wg.shape[1]
    return pl.pallas_call(
        functools.partial(_outproj_ffn_kernel, hc=hc),
        out_shape=jax.ShapeDtypeStruct((t, D_MODEL), F32),
        grid=(t // tm,),
        in_specs=[pl.BlockSpec((tm, D_MODEL), row), pl.BlockSpec((tm, a.shape[1]), row),
                  pl.BlockSpec((tm, b.shape[1]), row), resident(wa), resident(wb), resident(g1), resident(beta1),
                  in_hbm, in_hbm, in_hbm, resident(g2), resident(beta2)],
        out_specs=pl.BlockSpec((tm, D_MODEL), row),
        scratch_shapes=[pltpu.VMEM((tm, D_MODEL), F32),
                        pltpu.VMEM((D_MODEL, hidden), BF16), pltpu.VMEM((D_MODEL, hidden), BF16),
                        pltpu.VMEM((hidden, D_MODEL), BF16),
                        pltpu.VMEM((2, D_MODEL, hc), F32), pltpu.VMEM((2, hc, D_MODEL), F32),
                        pltpu.SemaphoreType.DMA((2,))],
        compiler_params=pltpu.CompilerParams(dimension_semantics=("arbitrary",), vmem_limit_bytes=VMEM_LIMIT),
        name="out_proj_swiglu_ffn_norm",
    )(x2d, a, b, wa, wb, g1, beta1, wg, wu, wd, g2, beta2)


def _conv_kernel(x_ref, win_ref, bin_ref, dww_ref, dwb_ref, lng_ref, lnb_ref, wout_ref, bout_ref,
                 g_ref, beta_ref, wg_ref, wu_ref, wd_ref, g2_ref, beta2_ref, o_ref, buf_ref, cv_ref, *, tm, rc, hc):
    @pl.when(pl.program_id(1) == 0)
    def _():
        buf_ref[0:HALO, :] = jnp.zeros((HALO, D_MODEL), F32)

    x = x_ref[...]
    h = _dot(x.astype(BF16), win_ref[...]) + bin_ref[...]
    buf_ref[HALO:HALO + tm, :] = h[:, :D_MODEL] * jax.nn.sigmoid(h[:, D_MODEL:])

    lead = HALO - (CONV_WIDTH - 1)

    def conv_rows(r, carry):
        r0 = pl.multiple_of(r * rc, rc)
        for lc in range(D_MODEL // LANES):
            ls = slice(lc * LANES, (lc + 1) * LANES)
            acc = jnp.broadcast_to(dwb_ref[:, ls], (rc, LANES))
            za = buf_ref[pl.ds(r0, rc + HALO), ls]
            for b in range(8):
                base, shift = 8 * ((lead + b) // 8), (lead + b) % 8
                span = rc + (8 if shift else 0)
                yb = None
                for a, j in enumerate(range(b, CONV_WIDTH, 8)):
                    term = za[base + 8 * a:base + 8 * a + span] * dww_ref[j:j + 1, ls]
                    yb = term if yb is None else yb + term
                acc = acc + yb[shift:shift + rc]
            cv_ref[pl.ds(r0, rc), ls] = acc
        return carry

    lax.fori_loop(0, tm // rc, conv_rows, 0)
    buf_ref[0:HALO, :] = buf_ref[tm:tm + HALO, :]

    y = jax.nn.silu(_layer_norm(cv_ref[...], lng_ref[...], lnb_ref[...]))
    y = _dot(y.astype(BF16), wout_ref[...]) + bout_ref[...]
    x1 = _layer_norm(ALPHA * x + y, g_ref[...], beta_ref[...])
    o_ref[...] = _swiglu_norm(x1, wg_ref, wu_ref, wd_ref, g2_ref, beta2_ref, cv_ref, hc)


def _conv_ffn(x2d, w_in, b_in, dw_w, dw_b, ln_g, ln_b, w_out, b_out, g, beta, wg, wu, wd, g2, beta2, *,
              batch, seq, tm=512, rc=256, hc=256):
    nt = seq // tm
    row = lambda b, i: (b * nt + i, 0)
    const2 = lambda b, i: (0, 0)
    consts = (w_in, b_in, dw_w, dw_b, ln_g, ln_b, w_out, b_out, g, beta, wg, wu, wd, g2, beta2)
    return pl.pallas_call(
        functools.partial(_conv_kernel, tm=tm, rc=rc, hc=hc),
        out_shape=jax.ShapeDtypeStruct((batch * seq, D_MODEL), F32),
        grid=(batch, nt),
        in_specs=[pl.BlockSpec((tm, D_MODEL), row)]
        + [pl.BlockSpec(c.shape, const2, pipeline_mode=pl.Buffered(1)) for c in consts],
        out_specs=pl.BlockSpec((tm, D_MODEL), row),
        scratch_shapes=[pltpu.VMEM((HALO + tm, D_MODEL), F32), pltpu.VMEM((tm, D_MODEL), F32)],
        compiler_params=pltpu.CompilerParams(dimension_semantics=("parallel", "arbitrary"),
                                             vmem_limit_bytes=VMEM_LIMIT),
        name="conv_module_swiglu_ffn_norm",
    )(x2d, *consts)


def _head_perm():
    p = np.arange(HEADS * HEAD_DIM)
    g, half, d = p // LANES, (p % LANES) // HEAD_DIM, p % HEAD_DIM
    return (g + GROUP * half) * HEAD_DIM + d


def _compress_weights(pe_k, w1_k, w2_k, pe_v, w1_v, w2_v):
    def block_diag(compact):
        row_c = (jnp.arange(compact.shape[0]) % 256) // HEAD_DIM
        col_c = jnp.arange(256) // HEAD_DIM
        return jnp.where(row_c[:, None] == col_c[None, :], jnp.tile(compact, (1, 4)), 0.0).astype(BF16)

    w1k = w1_k.reshape(CMP_BLOCK, 1, HEAD_DIM, HEAD_DIM)
    w1v = w1_v.reshape(CMP_BLOCK, 1, HEAD_DIM, HEAD_DIM)
    w1x = block_diag(jnp.concatenate([w1k, w1k, w1v, w1v], axis=1).reshape(CMP_BLOCK * 256, HEAD_DIM))
    w2x = block_diag(jnp.concatenate([w2_k, w2_k, w2_v, w2_v], axis=0))
    half = CMP_STRIDE * 256
    pe = jnp.stack([pe_k, pe_v])
    pex = jnp.broadcast_to(pe.transpose(1, 0, 2)[:, :, None, :], (CMP_BLOCK, 2, KV_HEADS, HEAD_DIM))
    pex = pex.reshape(2, half)
    pea = jnp.broadcast_to(pex[0:1], (8, half)).astype(BF16)
    peb = jnp.broadcast_to(pex[1:2], (8, half)).astype(BF16)
    return w1x[:half], w1x[half:], pea, peb, w2x


def _overlap_matrix(nc):
    c0 = np.arange(nc)[None, :] * CMP_STRIDE
    s0 = np.arange(LANES)[:, None] * SEL_BLOCK
    ov = (c0 < s0 + SEL_BLOCK) & (c0 + CMP_BLOCK > s0) & (np.arange(nc)[None, :] < nc - 1)
    return jnp.asarray(ov, dtype=BF16)


def _even_layer(x2d, rel_bias, w_in, w_out, ln_g, ln_b, w_s, b_s, pe_k, w1_k, w2_k, pe_v, w1_v, w2_v,
                norm_g, norm_b, *, batch, seq):
    nc = seq // CMP_STRIDE
    assert seq % FAR_TILE == 0 and seq >= NEAR and seq // SEL_BLOCK <= LANES
    perm = _head_perm()
    qw = HEADS * HEAD_DIM
    o = 2 * A_WIDTH
    wuv = w_in[:, :o].astype(BF16)
    wq = w_in[:, o:o + qw][:, perm]
    wkv = w_in[:, o + qw:o + qw + 768]
    wgt = jnp.pad(w_in[:, o + qw + 768:], ((0, 0), (0, LANES - 3 * HEADS)))
    wqkv = jnp.concatenate([wq, wkv, wgt], axis=1).astype(BF16)
    bs = jnp.broadcast_to(b_s[:, :, None], (A_GROUPS, CHUNK, A_WIDTH // A_GROUPS))
    a_out, qs, cmp2, kaug, vaug, wink, winv, gates = _proj(
        x2d, wuv, wqkv, ln_g[None, :], ln_b[None, :], w_s, bs, seq=seq)

    kc, vct = _compress(cmp2.reshape(batch, nc, CMP_STRIDE * 256),
                    *_compress_weights(pe_k, w1_k, w2_k, pe_v, w1_v, w2_v))

    fc, dn = _bias_tables(_bucket_thresholds(seq), rel_bias.T.reshape(-1), nc)
    b_out = _nsa(qs, gates, kc, vct, kaug, vaug, wink, winv, fc, dn, _overlap_matrix(nc), batch=batch, seq=seq)

    wo_a = w_out[:A_WIDTH].astype(BF16)
    wo_b = w_out[A_WIDTH:][perm].astype(BF16)
    return a_out, b_out, wo_a, wo_b, norm_g[None, :], norm_b[None, :]


def kernel(x, rel_bias, hyb_w_in, hyb_w_out, gmlp_ln_g, gmlp_ln_b, gmlp_w_s, gmlp_b_s, cmp_pe_k, cmp_w1_k, cmp_w2_k, cmp_pe_v, cmp_w1_v, cmp_w2_v, conv_w_in, conv_b_in, conv_dw_w, conv_dw_b, conv_ln_g, conv_ln_b, conv_w_out, conv_b_out, ffn_w_gate, ffn_w_up, ffn_w_down, norm_mix_g, norm_mix_b, norm_ffn_g, norm_ffn_b):
    batch, seq, d = x.shape
    h = x.reshape(batch * seq, d)
    for layer in range(DEPTH):
        i = layer // 2
        norms = (norm_ffn_g[layer][None, :], norm_ffn_b[layer][None, :])
        ffn = (ffn_w_gate[layer].astype(BF16), ffn_w_up[layer].astype(BF16), ffn_w_down[layer].astype(BF16)) + norms
        if layer % 2 == 0:
            mixer = _even_layer(h, rel_bias, hyb_w_in[i], hyb_w_out[i], gmlp_ln_g[i], gmlp_ln_b[i],
                                gmlp_w_s[i], gmlp_b_s[i], cmp_pe_k[i], cmp_w1_k[i], cmp_w2_k[i],
                                cmp_pe_v[i], cmp_w1_v[i], cmp_w2_v[i],
                                norm_mix_g[layer], norm_mix_b[layer], batch=batch, seq=seq)
            h = _outproj_ffn(h, *mixer, ffn_w_gate[layer], ffn_w_up[layer], ffn_w_down[layer], *norms)
        else:
            h = _conv_ffn(h, conv_w_in[i].astype(BF16), conv_b_in[i][None, :], conv_dw_w[i], conv_dw_b[i][None, :],
                          conv_ln_g[i][None, :], conv_ln_b[i][None, :], conv_w_out[i].astype(BF16),
                          conv_b_out[i][None, :], norm_mix_g[layer][None, :], norm_mix_b[layer][None, :],
                          *ffn, batch=batch, seq=seq)
    return h.reshape(batch, seq, d)
```

```python
import functools
import math

import numpy as np
import jax
import jax.numpy as jnp
from jax import lax
from jax.experimental import pallas as pl
from jax.experimental.pallas import tpu as pltpu

F32 = jnp.float32
BF16 = jnp.bfloat16

D_MODEL = 1024
DEPTH = 2
ALPHA = (2 * DEPTH) ** 0.25
CHUNK = 128
A_WIDTH = D_MODEL // 2
A_GROUPS = 4
HEADS = 8
KV_HEADS = 2
GROUP = HEADS // KV_HEADS
HEAD_DIM = (D_MODEL // 2) // HEADS
CMP_STRIDE = 16
CMP_BLOCK = 32
SEL_BLOCK = 64
N_SELECT = 16
WINDOW = 512
Q_BLOCK = 128
N_BUCKETS = 32
MAX_DISTANCE = 1024
CONV_WIDTH = 31
NEG_INF = -1e30
LN_EPS = 1e-5

LANES = 128
VMEM_LIMIT = 56 * 1024 * 1024

ROWS = GROUP * Q_BLOCK
FAR_TILE = 512
NEAR_TILES = 8
NEAR = NEAR_TILES * Q_BLOCK
NEAR_BACK = NEAR - Q_BLOCK
WIN_KEYS = WINDOW + Q_BLOCK
DN_WIDTH = NEAR + NEAR_BACK
HALO = 32
STAGE_SLOTS = 4


def _dot(a, b):
    return jnp.dot(a, b, preferred_element_type=F32)


def _dot_nt(a, b):
    return lax.dot_general(a, b, (((1,), (1,)), ((), ())), preferred_element_type=F32)


def _layer_norm(z, g, b):
    mu = jnp.mean(z, axis=-1, keepdims=True)
    d = z - mu
    var = jnp.mean(d * d, axis=-1, keepdims=True)
    return d * lax.rsqrt(var + LN_EPS) * g + b


def _t5_bucket(dist):
    n = jnp.maximum(dist, 0)
    max_exact = N_BUCKETS // 2
    nf = jnp.maximum(n, 1).astype(jnp.float32)
    large = max_exact + (jnp.log(nf / max_exact) / math.log(MAX_DISTANCE / max_exact)
                         * (N_BUCKETS - max_exact)).astype(jnp.int32)
    large = jnp.minimum(large, N_BUCKETS - 1)
    return jnp.where(n < max_exact, n, large)


def _bucket_thresholds(seq):
    b = _t5_bucket(jnp.arange(seq, dtype=jnp.int32))
    j = jnp.arange(N_BUCKETS, dtype=jnp.int32)
    return jnp.sum((b[None, :] < j[:, None]).astype(jnp.int32), axis=1).astype(jnp.int32)


def _tables_kernel(thr_ref, rb_ref, fc_ref, dn_ref, *, nc):
    h = pl.program_id(0)
    base = h * N_BUCKETS
    far = rb_ref[base + N_BUCKETS - 1]

    def bias_of(dist):
        val = jnp.full(dist.shape, far, F32)
        for j in range(N_BUCKETS - 1, 0, -1):
            val = jnp.where(dist < thr_ref[j], rb_ref[base + j - 1], val)
        return val

    def tile_values(dist, dmin, dmax, shift):
        if dmax < 0:
            return jnp.full(dist.shape, NEG_INF, F32)
        if dmin >= MAX_DISTANCE:
            return jnp.full(dist.shape, far - shift, F32)
        return jnp.where(dist >= 0, bias_of(dist) - shift, NEG_INF)

    ql = lax.broadcasted_iota(jnp.int32, (Q_BLOCK, LANES), 0)
    ln = lax.broadcasted_iota(jnp.int32, (Q_BLOCK, LANES), 1)
    top = Q_BLOCK - 1
    for c in range(2 * nc // Q_BLOCK):
        off = c * Q_BLOCK - (nc - 8)
        dist = ln - CMP_STRIDE * (ql + off) - (CMP_BLOCK - 1)
        dmin, dmax = -CMP_STRIDE * (top + off) - (CMP_BLOCK - 1), top - CMP_STRIDE * off - (CMP_BLOCK - 1)
        fc_ref[0, c * Q_BLOCK:(c + 1) * Q_BLOCK, :] = tile_values(dist, dmin, dmax, 0.0)
    for c in range(DN_WIDTH // LANES):
        off = NEAR_BACK - c * LANES
        dist = ql + off - ln
        dn_ref[0, :, c * LANES:(c + 1) * LANES] = tile_values(dist, off - top, off + top, far)


def _bias_tables(thr, rb_flat, nc):
    return pl.pallas_call(
        functools.partial(_tables_kernel, nc=nc),
        out_shape=(jax.ShapeDtypeStruct((HEADS, 2 * nc, Q_BLOCK), F32),
                   jax.ShapeDtypeStruct((HEADS, Q_BLOCK, DN_WIDTH), F32)),
        grid=(HEADS,),
        in_specs=[pl.BlockSpec(memory_space=pltpu.SMEM), pl.BlockSpec(memory_space=pltpu.SMEM)],
        out_specs=(pl.BlockSpec((1, 2 * nc, Q_BLOCK), lambda h: (h, 0, 0)),
                   pl.BlockSpec((1, Q_BLOCK, DN_WIDTH), lambda h: (h, 0, 0))),
        name="bias_tables",
    )(thr, rb_flat)


def _proj_kernel(x_ref, wuv_ref, wqkv_ref, lng_ref, lnb_ref, ws_ref, bs_ref,
                 a_ref, q_ref, cmp_ref, kaug_ref, vaug_ref, wink_ref, winv_ref, gate_ref, cmp_sc, *, tm, seq):
    xb = x_ref[...].astype(BF16)
    uv = jax.nn.gelu(_dot(xb, wuv_ref[...]))
    u = uv[:, :A_WIDTH]
    v = _layer_norm(uv[:, A_WIDTH:], lng_ref[...], lnb_ref[...]).astype(BF16)
    row = lax.broadcasted_iota(jnp.int32, (CHUNK, CHUNK), 0)
    col = lax.broadcasted_iota(jnp.int32, (CHUNK, CHUNK), 1)
    gd = A_WIDTH // A_GROUPS
    for g in range(A_GROUPS):
        w = jnp.where(col <= row, ws_ref[g], 0.0).astype(BF16)
        for c in range(tm // CHUNK):
            rs = slice(c * CHUNK, (c + 1) * CHUNK)
            cs = slice(g * gd, (g + 1) * gd)
            s = _dot(w, v[rs, cs]) + bs_ref[g]
            a_ref[rs, cs] = (u[rs, cs] * s).astype(BF16)

    h = _dot(xb, wqkv_ref[...])
    q_ref[...] = (h[:, 0:512] * (HEAD_DIM ** -0.5)).astype(BF16)
    for half in range(2):
        cmp_sc[half] = h[:, 512 + half * LANES:512 + (half + 1) * LANES]
        for l in range(CMP_STRIDE):
            rows = cmp_sc[half, pl.ds(l, tm // CMP_STRIDE, stride=CMP_STRIDE), :]
            cmp_ref[:, l * 256 + half * LANES:l * 256 + (half + 1) * LANES] = rows.astype(BF16)
    kaug_ref[:, 0:LANES] = h[:, 768:896].astype(BF16)
    pos = (pl.program_id(0) * tm) % seq + lax.broadcasted_iota(jnp.int32, (tm, LANES), 0)
    blk = lax.broadcasted_iota(jnp.int32, (tm, LANES), 1)
    kaug_ref[:, LANES:2 * LANES] = jnp.where(pos // SEL_BLOCK == blk, 1.0, 0.0).astype(BF16)
    ones = jnp.ones((tm, LANES), BF16)
    vaug_ref[:, 0:LANES] = h[:, 896:1024].astype(BF16)
    vaug_ref[:, LANES:2 * LANES] = ones
    wink_ref[...] = h[:, 1024:1152].astype(BF16)
    winv_ref[:, 0:LANES] = h[:, 1152:1280].astype(BF16)
    winv_ref[:, LANES:2 * LANES] = ones
    gate_ref[...] = jax.nn.sigmoid(h[:, 1280:1408])


def _proj(x2d, wuv, wqkv, lng, lnb, ws, bs, *, seq, tm=512):
    t = x2d.shape[0]
    row = lambda i: (i, 0)
    const2 = lambda i: (0, 0)
    const3 = lambda i: (0, 0, 0)
    outs = [(1, A_WIDTH, BF16), (1, 512, BF16), (CMP_STRIDE, CMP_STRIDE * 256, BF16), (1, 256, BF16),
            (1, 256, BF16), (1, 128, BF16), (1, 256, BF16), (1, 128, F32)]
    return pl.pallas_call(
        functools.partial(_proj_kernel, tm=tm, seq=seq),
        out_shape=tuple(jax.ShapeDtypeStruct((t // r, w), dt) for r, w, dt in outs),
        grid=(t // tm,),
        in_specs=[pl.BlockSpec((tm, D_MODEL), row),
                  pl.BlockSpec(wuv.shape, const2),
                  pl.BlockSpec(wqkv.shape, const2),
                  pl.BlockSpec(lng.shape, const2),
                  pl.BlockSpec(lnb.shape, const2),
                  pl.BlockSpec(ws.shape, const3),
                  pl.BlockSpec(bs.shape, const3)],
        out_specs=tuple(pl.BlockSpec((tm // r, w), row) for r, w, _ in outs),
        scratch_shapes=[pltpu.VMEM((2, tm, LANES), F32)],
        compiler_params=pltpu.CompilerParams(dimension_semantics=("parallel",), vmem_limit_bytes=VMEM_LIMIT),
        name="in_proj_gmlp",
    )(x2d, wuv, wqkv, lng, lnb, ws, bs)


def _compress_kernel(r_ref, wa_ref, wb_ref, pea_ref, peb_ref, w2_ref, kc_ref, vct_ref):
    r = r_ref[0]
    top = _dot(r, wa_ref[...])
    bot = _dot(r, wb_ref[...])
    pe = _dot(pea_ref[...], wa_ref[...]) + _dot(peb_ref[...], wb_ref[...])
    hid = top + pltpu.roll(bot, bot.shape[0] - 1, 0) + pe[0:1]
    out = _dot(jax.nn.gelu(hid).astype(BF16), w2_ref[...])
    kc_ref[0] = out[:, 0:LANES].astype(BF16)
    vct_ref[0] = out[:, LANES:2 * LANES].T.astype(BF16)


def _compress(cmp3, wa, wb, pea, peb, w2):
    b, nc, width = cmp3.shape
    const2 = lambda i: (0, 0)
    return pl.pallas_call(
        _compress_kernel,
        out_shape=(jax.ShapeDtypeStruct((b, nc, LANES), BF16), jax.ShapeDtypeStruct((b, LANES, nc), BF16)),
        grid=(b,),
        in_specs=[pl.BlockSpec((1, nc, width), lambda i: (i, 0, 0)),
                  pl.BlockSpec(wa.shape, const2), pl.BlockSpec(wb.shape, const2),
                  pl.BlockSpec(pea.shape, const2), pl.BlockSpec(peb.shape, const2),
                  pl.BlockSpec(w2.shape, const2)],
        out_specs=(pl.BlockSpec((1, nc, LANES), lambda i: (i, 0, 0)),
                   pl.BlockSpec((1, LANES, nc), lambda i: (i, 0, 0))),
        compiler_params=pltpu.CompilerParams(dimension_semantics=("parallel",), vmem_limit_bytes=VMEM_LIMIT),
        name="kv_compress",
    )(cmp3, wa, wb, pea, peb, w2)


def _softmax_tile(s, m_sc):
    cols = [s[:, j * LANES:(j + 1) * LANES] for j in range(s.shape[1] // LANES)]
    m_old = m_sc[...]
    m_new = jnp.maximum(m_old, jnp.max(functools.reduce(jnp.maximum, cols), axis=-1, keepdims=True))
    m_sc[...] = m_new
    p = jnp.concatenate([jnp.exp(c - m_new) for c in cols], axis=1).astype(BF16)
    return p, jnp.exp(m_old - m_new)


def _scale_both(acc, alpha):
    return jnp.concatenate([acc[:, :LANES] * alpha, acc[:, LANES:] * alpha], axis=1)


def _nsa_kernel(q_ref, gate_ref, kc_ref, vct_ref, kaug_ref, vaug_ref, wink_ref, winv_ref, fc_ref, dn_ref, ovt_ref,
                o_ref, qa_all, s_all, p_all, m_all, acc_all, *, nc, n_tiles):
    ib = pl.program_id(1)
    t0 = ib * Q_BLOCK
    lane = lax.broadcasted_iota(jnp.int32, (Q_BLOCK, LANES), 1)
    qrow = lax.broadcasted_iota(jnp.int32, (Q_BLOCK, LANES), 0)
    q = q_ref[...]
    gates = gate_ref[...]
    kc = kc_ref[0]
    vct = vct_ref[0]
    ovt = ovt_ref[...]

    def stack_heads(fn):
        return jnp.concatenate([fn(g) for g in range(GROUP)], axis=0)

    near0 = 2 * ib - NEAR_BACK // SEL_BLOCK
    n_far = jnp.maximum(ib - 4, 0) // 4
    n_pairs = (n_far + 1) // 2
    kstart = jnp.maximum(t0 - NEAR_BACK, 0)
    j0 = pl.multiple_of(kstart - (t0 - NEAR_BACK), LANES)
    kstart = pl.multiple_of(kstart, LANES)
    wstart = jnp.maximum(t0 - WINDOW, 0)
    jw = pl.multiple_of(wstart - (t0 - NEAR_BACK), LANES)
    wstart = pl.multiple_of(wstart, LANES)
    win_edge = jnp.where((lane > qrow) | (t0 < WINDOW), 0.0, NEG_INF)

    fc_row = pl.multiple_of((nc - 8) - 8 * ib, 8)
    qcol = lax.broadcasted_iota(jnp.int32, (1, ROWS), 1) % Q_BLOCK
    has_cmp = jnp.where(t0 + qcol >= CMP_BLOCK - 1, 1.0, 0.0)
    qhs, o_cts, imps = [], [], []
    for hkv in range(KV_HEADS):
        mine = (lane >= HEAD_DIM) if hkv else (lane < HEAD_DIM)
        qh = stack_heads(lambda g: jnp.where(mine, q[:, g * LANES:(g + 1) * LANES], 0.0).astype(BF16))
        lct = _dot_nt(kc, qh)
        lct = lct + jnp.concatenate([fc_ref[hkv * GROUP + g, pl.ds(fc_row, nc), :] for g in range(GROUP)], axis=1)
        e = jnp.exp(lct - jnp.max(lct, axis=0, keepdims=True))
        pct = e * (has_cmp / jnp.sum(e, axis=0, keepdims=True))
        o_cts.append(_dot(vct, pct.astype(BF16)))
        psum = pct[:, 0:LANES] + pct[:, LANES:2 * LANES] + pct[:, 2 * LANES:3 * LANES] + pct[:, 3 * LANES:]
        p_hi = psum.astype(BF16)
        p_lo = (psum - p_hi.astype(F32)).astype(BF16)
        imps.append(_dot(ovt, p_hi) + _dot(ovt, p_lo))
        qhs.append(qh)

    blk = lax.broadcasted_iota(jnp.int32, (LANES, KV_HEADS * Q_BLOCK), 0)
    col = lax.broadcasted_iota(jnp.int32, (LANES, KV_HEADS * Q_BLOCK), 1)
    jq = 2 * ib + jnp.where(col % Q_BLOCK >= SEL_BLOCK, 1, 0)
    forced = (blk == 0) | (blk == jq) | (blk == jq - 1)
    cand = jnp.where(forced, -3e38, jnp.where(blk > jq, NEG_INF, jnp.concatenate(imps, axis=1)))
    blk_f = blk.astype(F32)

    def pick_one(_, carry):
        cur, chosen = carry
        mx = jnp.max(cur, axis=0, keepdims=True)
        first = jnp.min(jnp.where(cur == mx, blk_f, float(LANES)), axis=0, keepdims=True)
        pick = blk_f == first
        return jnp.where(pick, -3e38, cur), jnp.where(pick, 1.0, chosen)

    _, sel_t = lax.fori_loop(0, N_SELECT - 3, pick_one, (cand, jnp.where(forced, 1.0, 0.0)), unroll=True)

    def window_branch(hkv):
        sw = _dot_nt(qhs[hkv], wink_ref[pl.ds(wstart, WIN_KEYS), :])
        sw = sw + stack_heads(lambda g: dn_ref[hkv * GROUP + g, :, pl.ds(jw, WIN_KEYS)])
        sw = jnp.concatenate([sw[:, :LANES] + jnp.concatenate([win_edge] * GROUP, axis=0), sw[:, LANES:]], axis=1)
        pw = jnp.exp(sw - jnp.max(sw, axis=-1, keepdims=True)).astype(BF16)
        ow = _dot(pw, winv_ref[pl.ds(wstart, WIN_KEYS), :])
        return ow[:, :LANES] / ow[:, LANES:]

    o_w_first = window_branch(0)

    outs = []
    for hkv in range(KV_HEADS):
        qa_sc, s_buf, p_buf = qa_all.at[hkv], s_all.at[hkv], p_all.at[hkv]
        m_sc, acc_sc = m_all.at[hkv], acc_all.at[hkv]
        qh = qhs[hkv]
        sel = sel_t[:, hkv * Q_BLOCK:(hkv + 1) * Q_BLOCK].T

        m_near = jnp.where(sel > 0, 0.0, NEG_INF).astype(BF16)
        m_far = jnp.where((sel > 0) & (lane < near0), 0.0, NEG_INF).astype(BF16)
        qa_sc[0, :, 0:LANES] = qh
        qa_sc[0, :, LANES:2 * LANES] = jnp.concatenate([m_far] * GROUP, axis=0)
        qa_sc[1, :, 0:LANES] = qh
        qa_sc[1, :, LANES:2 * LANES] = jnp.concatenate([m_near] * GROUP, axis=0)
        m_sc[...] = jnp.full(m_sc.shape, -jnp.inf, F32)
        acc_sc[...] = jnp.zeros(acc_sc.shape, F32)
        p_buf[1] = jnp.zeros((ROWS, FAR_TILE), BF16)

        def tile_keys(tile):
            near_idx = tile - 2 * n_pairs
            ks = jnp.where(near_idx >= 0, kstart + near_idx * FAR_TILE, tile * FAR_TILE)
            ks = jnp.clip(ks, 0, (n_tiles - 1) * FAR_TILE)
            return pl.multiple_of(ks, LANES), jnp.where(near_idx >= 0, 1, 0)

        def logits(tile, slot):
            ks, variant = tile_keys(tile)
            s_buf[slot] = _dot_nt(qa_sc[variant], kaug_ref[pl.ds(ks, FAR_TILE), :])

        def values(tile, slot):
            ks, _ = tile_keys(tile)
            return _dot(p_buf[slot], vaug_ref[pl.ds(ks, FAR_TILE), :])

        def absorb(slot, pending, table_col=None):
            s = s_buf[slot]
            if table_col is not None:
                js = pl.multiple_of(table_col, LANES)
                s = s + stack_heads(lambda g: dn_ref[hkv * GROUP + g, :, pl.ds(js, FAR_TILE)])
            p, alpha = _softmax_tile(s, m_sc)
            p_buf[slot] = p
            acc_sc[...] = _scale_both(acc_sc[...] + pending, alpha)

        logits(0, 0)

        o_w = o_w_first if hkv == 0 else window_branch(hkv)

        def far_pair(i, carry):
            t = 2 * i
            logits(t + 1, 1)
            absorb(0, values(t - 1, 1))
            logits(t + 2, 0)
            absorb(1, values(t, 0))
            return carry

        lax.fori_loop(0, n_pairs, far_pair, 0)
        t = 2 * n_pairs
        logits(t + 1, 1)
        absorb(0, values(t - 1, 1), j0)
        absorb(1, values(t, 0), j0 + FAR_TILE)
        acc = acc_sc[...] + values(t + 1, 1)
        o_s = acc[:, :LANES] / acc[:, LANES:]

        per_head = []
        for g in range(GROUP):
            c = 3 * (hkv * GROUP + g)
            rs = slice(g * Q_BLOCK, (g + 1) * Q_BLOCK)
            o_c = o_cts[hkv][:, rs].T
            per_head.append(gates[:, c:c + 1] * o_c + gates[:, c + 1:c + 2] * o_s[rs]
                            + gates[:, c + 2:c + 3] * o_w[rs])
        outs.append(per_head)

    for g in range(GROUP):
        o_ref[:, g * LANES:(g + 1) * LANES] = jnp.where(lane < HEAD_DIM, outs[0][g], outs[1][g]).astype(BF16)


def _nsa(qs, gates, kc, vct, kaug, vaug, wink, winv, fc, dn, ovt, *, batch, seq):
    nc = seq // CMP_STRIDE
    nq = seq // Q_BLOCK
    qrow = lambda b, i: (b * nq + i, 0)
    per_batch = lambda b, i: (b, 0)
    const2 = lambda b, i: (0, 0)
    const3 = lambda b, i: (0, 0, 0)
    once = pl.Buffered(1)
    return pl.pallas_call(
        functools.partial(_nsa_kernel, nc=nc, n_tiles=seq // FAR_TILE),
        out_shape=jax.ShapeDtypeStruct((batch * seq, 4 * LANES), BF16),
        grid=(batch, nq),
        in_specs=[pl.BlockSpec((Q_BLOCK, 4 * LANES), qrow),
                  pl.BlockSpec((Q_BLOCK, LANES), qrow),
                  pl.BlockSpec((1, nc, LANES), lambda b, i: (b, 0, 0)),
                  pl.BlockSpec((1, LANES, nc), lambda b, i: (b, 0, 0)),
                  pl.BlockSpec((seq, 256), per_batch, pipeline_mode=once),
                  pl.BlockSpec((seq, 256), per_batch, pipeline_mode=once),
                  pl.BlockSpec((seq, LANES), per_batch, pipeline_mode=once),
                  pl.BlockSpec((seq, 256), per_batch, pipeline_mode=once),
                  pl.BlockSpec(fc.shape, const3, pipeline_mode=once),
                  pl.BlockSpec(dn.shape, const3, pipeline_mode=once),
                  pl.BlockSpec(ovt.shape, const2)],
        out_specs=pl.BlockSpec((Q_BLOCK, 4 * LANES), qrow),
        scratch_shapes=[pltpu.VMEM((KV_HEADS, 2, ROWS, 2 * LANES), BF16),
                        pltpu.VMEM((KV_HEADS, 2, ROWS, FAR_TILE), F32),
                        pltpu.VMEM((KV_HEADS, 2, ROWS, FAR_TILE), BF16),
                        pltpu.VMEM((KV_HEADS, ROWS, LANES), F32),
                        pltpu.VMEM((KV_HEADS, ROWS, 2 * LANES), F32)],
        compiler_params=pltpu.CompilerParams(dimension_semantics=("parallel", "arbitrary"),
                                             vmem_limit_bytes=VMEM_LIMIT),
        name="sparse_attention",
    )(qs, gates, kc, vct, kaug, vaug, wink, winv, fc, dn, ovt)


def _swiglu_norm(x, wg_ref, wu_ref, wd_ref, g_ref, beta_ref, acc_ref, hc):
    xb = x.astype(BF16)
    hidden = wg_ref.shape[1]
    for c in range(hidden // hc):
        cs = slice(c * hc, (c + 1) * hc)
        gate = _dot(xb, wg_ref[:, cs])
        up = _dot(xb, wu_ref[:, cs])
        part = _dot((jax.nn.silu(gate) * up).astype(BF16), wd_ref[cs, :])
        if c == 0:
            acc_ref[...] = part
        else:
            acc_ref[...] += part
    return _layer_norm(ALPHA * x + acc_ref[...], g_ref[...], beta_ref[...])


def _outproj_ffn_kernel(x_ref, a_ref, b_ref, wa_ref, wb_ref, g1_ref, beta1_ref,
                        wg_hbm, wu_hbm, wd_hbm, g2_ref, beta2_ref, o_ref,
                        acc_ref, wg_ref, wu_ref, wd_ref, stage_col, stage_row, sem, *, hc):
    @pl.when(pl.program_id(0) == 0)
    def _():
        n = wg_ref.shape[1] // hc
        jobs = [(wg_hbm, wg_ref, True, c) for c in range(n)] + [(wu_hbm, wu_ref, True, c) for c in range(n)] \
            + [(wd_hbm, wd_ref, False, c) for c in range(n)]

        def chunk_copy(j):
            src, _, by_col, c = jobs[j]
            slot = j % STAGE_SLOTS
            if by_col:
                return pltpu.make_async_copy(src.at[:, pl.ds(c * hc, hc)], stage_col.at[slot], sem.at[slot])
            return pltpu.make_async_copy(src.at[pl.ds(c * hc, hc), :], stage_row.at[slot], sem.at[slot])

        ahead = STAGE_SLOTS - 1
        for j in range(min(ahead, len(jobs))):
            chunk_copy(j).start()
        for j, (_, dst, by_col, c) in enumerate(jobs):
            if j + ahead < len(jobs):
                chunk_copy(j + ahead).start()
            chunk_copy(j).wait()
            if by_col:
                dst[:, c * hc:(c + 1) * hc] = stage_col[j % STAGE_SLOTS].astype(BF16)
            else:
                dst[c * hc:(c + 1) * hc, :] = stage_row[j % STAGE_SLOTS].astype(BF16)

    y = _dot(a_ref[...], wa_ref[...]) + _dot(b_ref[...], wb_ref[...])
    x1 = _layer_norm(ALPHA * x_ref[...] + y, g1_ref[...], beta1_ref[...])
    o_ref[...] = _swiglu_norm(x1, wg_ref, wu_ref, wd_ref, g2_ref, beta2_ref, acc_ref, hc)


def _outproj_ffn(x2d, a, b, wa, wb, g1, beta1, wg, wu, wd, g2, beta2, *, tm=512, hc=256):
    t = x2d.shape[0]
    row = lambda i: (i, 0)
    const2 = lambda i: (0, 0)
    resident = lambda c: pl.BlockSpec(c.shape, const2, pipeline_mode=pl.Buffered(1))
    in_hbm = pl.BlockSpec(memory_space=pl.ANY)
    hidden = wg.shape[1]
    return pl.pallas_call(
        functools.partial(_outproj_ffn_kernel, hc=hc),
        out_shape=jax.ShapeDtypeStruct((t, D_MODEL), F32),
        grid=(t // tm,),
        in_specs=[pl.BlockSpec((tm, D_MODEL), row), pl.BlockSpec((tm, a.shape[1]), row),
                  pl.BlockSpec((tm, b.shape[1]), row), resident(wa), resident(wb), resident(g1), resident(beta1),
                  in_hbm, in_hbm, in_hbm, resident(g2), resident(beta2)],
        out_specs=pl.BlockSpec((tm, D_MODEL), row),
        scratch_shapes=[pltpu.VMEM((tm, D_MODEL), F32),
                        pltpu.VMEM((D_MODEL, hidden), BF16), pltpu.VMEM((D_MODEL, hidden), BF16),
                        pltpu.VMEM((hidden, D_MODEL), BF16),
                        pltpu.VMEM((STAGE_SLOTS, D_MODEL, hc), F32), pltpu.VMEM((STAGE_SLOTS, hc, D_MODEL), F32),
                        pltpu.SemaphoreType.DMA((STAGE_SLOTS,))],
        compiler_params=pltpu.CompilerParams(dimension_semantics=("arbitrary",), vmem_limit_bytes=VMEM_LIMIT),
        name="out_proj_swiglu_ffn_norm",
    )(x2d, a, b, wa, wb, g1, beta1, wg, wu, wd, g2, beta2)


def _conv_kernel(x_ref, win_ref, bin_ref, dww_ref, dwb_ref, lng_ref, lnb_ref, wout_ref, bout_ref,
                 g_ref, beta_ref, wg_ref, wu_ref, wd_ref, g2_ref, beta2_ref, o_ref, buf_ref, cv_ref, *, tm, rc, hc):
    @pl.when(pl.program_id(1) == 0)
    def _():
        buf_ref[0:HALO, :] = jnp.zeros((HALO, D_MODEL), F32)

    x = x_ref[...]
    h = _dot(x.astype(BF16), win_ref[...]) + bin_ref[...]
    buf_ref[HALO:HALO + tm, :] = h[:, :D_MODEL] * jax.nn.sigmoid(h[:, D_MODEL:])

    lead = HALO - (CONV_WIDTH - 1)

    def conv_rows(r, carry):
        r0 = pl.multiple_of(r * rc, rc)
        for lc in range(D_MODEL // LANES):
            ls = slice(lc * LANES, (lc + 1) * LANES)
            acc = jnp.broadcast_to(dwb_ref[:, ls], (rc, LANES))
            za = buf_ref[pl.ds(r0, rc + HALO), ls]
            for b in range(8):
                base, shift = 8 * ((lead + b) // 8), (lead + b) % 8
                span = rc + (8 if shift else 0)
                yb = None
                for a, j in enumerate(range(b, CONV_WIDTH, 8)):
                    term = za[base + 8 * a:base + 8 * a + span] * dww_ref[j:j + 1, ls]
                    yb = term if yb is None else yb + term
                acc = acc + yb[shift:shift + rc]
            cv_ref[pl.ds(r0, rc), ls] = acc
        return carry

    lax.fori_loop(0, tm // rc, conv_rows, 0)
    buf_ref[0:HALO, :] = buf_ref[tm:tm + HALO, :]

    y = jax.nn.silu(_layer_norm(cv_ref[...], lng_ref[...], lnb_ref[...]))
    y = _dot(y.astype(BF16), wout_ref[...]) + bout_ref[...]
    x1 = _layer_norm(ALPHA * x + y, g_ref[...], beta_ref[...])
    o_ref[...] = _swiglu_norm(x1, wg_ref, wu_ref, wd_ref, g2_ref, beta2_ref, cv_ref, hc)


def _conv_ffn(x2d, w_in, b_in, dw_w, dw_b, ln_g, ln_b, w_out, b_out, g, beta, wg, wu, wd, g2, beta2, *,
              batch, seq, tm=512, rc=256, hc=256):
    nt = seq // tm
    row = lambda b, i: (b * nt + i, 0)
    const2 = lambda b, i: (0, 0)
    consts = (w_in, b_in, dw_w, dw_b, ln_g, ln_b, w_out, b_out, g, beta, wg, wu, wd, g2, beta2)
    return pl.pallas_call(
        functools.partial(_conv_kernel, tm=tm, rc=rc, hc=hc),
        out_shape=jax.ShapeDtypeStruct((batch * seq, D_MODEL), F32),
        grid=(batch, nt),
        in_specs=[pl.BlockSpec((tm, D_MODEL), row)]
        + [pl.BlockSpec(c.shape, const2, pipeline_mode=pl.Buffered(1)) for c in consts],
        out_specs=pl.BlockSpec((tm, D_MODEL), row),
        scratch_shapes=[pltpu.VMEM((HALO + tm, D_MODEL), F32), pltpu.VMEM((tm, D_MODEL), F32)],
        compiler_params=pltpu.CompilerParams(dimension_semantics=("parallel", "arbitrary"),
                                             vmem_limit_bytes=VMEM_LIMIT),
        name="conv_module_swiglu_ffn_norm",
    )(x2d, *consts)


def _head_perm():
    p = np.arange(HEADS * HEAD_DIM)
    g, half, d = p // LANES, (p % LANES) // HEAD_DIM, p % HEAD_DIM
    return (g + GROUP * half) * HEAD_DIM + d


def _compress_weights(pe_k, w1_k, w2_k, pe_v, w1_v, w2_v):
    def block_diag(compact):
        row_c = (jnp.arange(compact.shape[0]) % 256) // HEAD_DIM
        col_c = jnp.arange(256) // HEAD_DIM
        return jnp.where(row_c[:, None] == col_c[None, :], jnp.tile(compact, (1, 4)), 0.0).astype(BF16)

    w1k = w1_k.reshape(CMP_BLOCK, 1, HEAD_DIM, HEAD_DIM)
    w1v = w1_v.reshape(CMP_BLOCK, 1, HEAD_DIM, HEAD_DIM)
    w1x = block_diag(jnp.concatenate([w1k, w1k, w1v, w1v], axis=1).reshape(CMP_BLOCK * 256, HEAD_DIM))
    w2x = block_diag(jnp.concatenate([w2_k, w2_k, w2_v, w2_v], axis=0))
    half = CMP_STRIDE * 256
    pe = jnp.stack([pe_k, pe_v])
    pex = jnp.broadcast_to(pe.transpose(1, 0, 2)[:, :, None, :], (CMP_BLOCK, 2, KV_HEADS, HEAD_DIM))
    pex = pex.reshape(2, half)
    pea = jnp.broadcast_to(pex[0:1], (8, half)).astype(BF16)
    peb = jnp.broadcast_to(pex[1:2], (8, half)).astype(BF16)
    return w1x[:half], w1x[half:], pea, peb, w2x


def _overlap_matrix(nc):
    c0 = np.arange(nc)[None, :] * CMP_STRIDE
    s0 = np.arange(LANES)[:, None] * SEL_BLOCK
    ov = (c0 < s0 + SEL_BLOCK) & (c0 + CMP_BLOCK > s0) & (np.arange(nc)[None, :] < nc - 1)
    return jnp.asarray(ov, dtype=BF16)


def _even_layer(x2d, rel_bias, w_in, w_out, ln_g, ln_b, w_s, b_s, pe_k, w1_k, w2_k, pe_v, w1_v, w2_v,
                norm_g, norm_b, *, batch, seq):
    nc = seq // CMP_STRIDE
    assert seq % FAR_TILE == 0 and seq >= NEAR and seq // SEL_BLOCK <= LANES
    perm = _head_perm()
    qw = HEADS * HEAD_DIM
    o = 2 * A_WIDTH
    wuv = w_in[:, :o].astype(BF16)
    wq = w_in[:, o:o + qw][:, perm]
    wkv = w_in[:, o + qw:o + qw + 768]
    wgt = jnp.pad(w_in[:, o + qw + 768:], ((0, 0), (0, LANES - 3 * HEADS)))
    wqkv = jnp.concatenate([wq, wkv, wgt], axis=1).astype(BF16)
    bs = jnp.broadcast_to(b_s[:, :, None], (A_GROUPS, CHUNK, A_WIDTH // A_GROUPS))
    a_out, qs, cmp2, kaug, vaug, wink, winv, gates = _proj(
        x2d, wuv, wqkv, ln_g[None, :], ln_b[None, :], w_s, bs, seq=seq)

    kc, vct = _compress(cmp2.reshape(batch, nc, CMP_STRIDE * 256),
                    *_compress_weights(pe_k, w1_k, w2_k, pe_v, w1_v, w2_v))

    fc, dn = _bias_tables(_bucket_thresholds(seq), rel_bias.T.reshape(-1), nc)
    b_out = _nsa(qs, gates, kc, vct, kaug, vaug, wink, winv, fc, dn, _overlap_matrix(nc), batch=batch, seq=seq)

    wo_a = w_out[:A_WIDTH].astype(BF16)
    wo_b = w_out[A_WIDTH:][perm].astype(BF16)
    return a_out, b_out, wo_a, wo_b, norm_g[None, :], norm_b[None, :]


def kernel(x, rel_bias, hyb_w_in, hyb_w_out, gmlp_ln_g, gmlp_ln_b, gmlp_w_s, gmlp_b_s, cmp_pe_k, cmp_w1_k, cmp_w2_k, cmp_pe_v, cmp_w1_v, cmp_w2_v, conv_w_in, conv_b_in, conv_dw_w, conv_dw_b, conv_ln_g, conv_ln_b, conv_w_out, conv_b_out, ffn_w_gate, ffn_w_up, ffn_w_down, norm_mix_g, norm_mix_b, norm_ffn_g, norm_ffn_b):
    batch, seq, d = x.shape
    h = x.reshape(batch * seq, d)
    for layer in range(DEPTH):
        i = layer // 2
        norms = (norm_ffn_g[layer][None, :], norm_ffn_b[layer][None, :])
        ffn = (ffn_w_gate[layer].astype(BF16), ffn_w_up[layer].astype(BF16), ffn_w_down[layer].astype(BF16)) + norms
        if layer % 2 == 0:
            mixer = _even_layer(h, rel_bias, hyb_w_in[i], hyb_w_out[i], gmlp_ln_g[i], gmlp_ln_b[i],
                                gmlp_w_s[i], gmlp_b_s[i], cmp_pe_k[i], cmp_w1_k[i], cmp_w2_k[i],
                                cmp_pe_v[i], cmp_w1_v[i], cmp_w2_v[i],
                                norm_mix_g[layer], norm_mix_b[layer], batch=batch, seq=seq)
            h = _outproj_ffn(h, *mixer, ffn_w_gate[layer], ffn_w_up[layer], ffn_w_down[layer], *norms)
        else:
            h = _conv_ffn(h, conv_w_in[i].astype(BF16), conv_b_in[i][None, :], conv_dw_w[i], conv_dw_b[i][None, :],
                          conv_ln_g[i][None, :], conv_ln_b[i][None, :], conv_w_out[i].astype(BF16),
                          conv_b_out[i][None, :], norm_mix_g[layer][None, :], norm_mix_b[layer][None, :],
                          *ffn, batch=batch, seq=seq)
    return h.reshape(batch, seq, d)
```

```python
import functools
import math

import numpy as np
import jax
import jax.numpy as jnp
from jax import lax
from jax.experimental import pallas as pl
from jax.experimental.pallas import tpu as pltpu

F32 = jnp.float32
BF16 = jnp.bfloat16

D_MODEL = 1024
DEPTH = 2
ALPHA = (2 * DEPTH) ** 0.25
CHUNK = 128
A_WIDTH = D_MODEL // 2
A_GROUPS = 4
HEADS = 8
KV_HEADS = 2
GROUP = HEADS // KV_HEADS
HEAD_DIM = (D_MODEL // 2) // HEADS
CMP_STRIDE = 16
CMP_BLOCK = 32
SEL_BLOCK = 64
N_SELECT = 16
WINDOW = 512
Q_BLOCK = 128
N_BUCKETS = 32
MAX_DISTANCE = 1024
CONV_WIDTH = 31
NEG_INF = -1e30
LN_EPS = 1e-5

LANES = 128
VMEM_LIMIT = 56 * 1024 * 1024

ROWS = GROUP * Q_BLOCK
FAR_TILE = 512
NEAR_TILES = 8
NEAR = NEAR_TILES * Q_BLOCK
NEAR_BACK = NEAR - Q_BLOCK
WIN_KEYS = WINDOW + Q_BLOCK
DN_WIDTH = NEAR + NEAR_BACK
HALO = 32
STAGE_SLOTS = 4


def _dot(a, b):
    return jnp.dot(a, b, preferred_element_type=F32)


def _dot_nt(a, b):
    return lax.dot_general(a, b, (((1,), (1,)), ((), ())), preferred_element_type=F32)


def _layer_norm(z, g, b):
    mu = jnp.mean(z, axis=-1, keepdims=True)
    d = z - mu
    var = jnp.mean(d * d, axis=-1, keepdims=True)
    return d * lax.rsqrt(var + LN_EPS) * g + b


def _t5_bucket(dist):
    n = jnp.maximum(dist, 0)
    max_exact = N_BUCKETS // 2
    nf = jnp.maximum(n, 1).astype(jnp.float32)
    large = max_exact + (jnp.log(nf / max_exact) / math.log(MAX_DISTANCE / max_exact)
                         * (N_BUCKETS - max_exact)).astype(jnp.int32)
    large = jnp.minimum(large, N_BUCKETS - 1)
    return jnp.where(n < max_exact, n, large)


def _bucket_thresholds(seq):
    b = _t5_bucket(jnp.arange(seq, dtype=jnp.int32))
    j = jnp.arange(N_BUCKETS, dtype=jnp.int32)
    return jnp.sum((b[None, :] < j[:, None]).astype(jnp.int32), axis=1).astype(jnp.int32)


def _tables_kernel(thr_ref, rb_ref, fc_ref, dn_ref, *, nc):
    h = pl.program_id(0)
    base = h * N_BUCKETS
    far = rb_ref[base + N_BUCKETS - 1]

    def bias_of(dist):
        val = jnp.full(dist.shape, far, F32)
        for j in range(N_BUCKETS - 1, 0, -1):
            val = jnp.where(dist < thr_ref[j], rb_ref[base + j - 1], val)
        return val

    def tile_values(dist, dmin, dmax, shift):
        if dmax < 0:
            return jnp.full(dist.shape, NEG_INF, F32)
        if dmin >= MAX_DISTANCE:
            return jnp.full(dist.shape, far - shift, F32)
        return jnp.where(dist >= 0, bias_of(dist) - shift, NEG_INF)

    ql = lax.broadcasted_iota(jnp.int32, (Q_BLOCK, LANES), 0)
    ln = lax.broadcasted_iota(jnp.int32, (Q_BLOCK, LANES), 1)
    top = Q_BLOCK - 1
    for c in range(2 * nc // Q_BLOCK):
        off = c * Q_BLOCK - (nc - 8)
        dist = ln - CMP_STRIDE * (ql + off) - (CMP_BLOCK - 1)
        dmin, dmax = -CMP_STRIDE * (top + off) - (CMP_BLOCK - 1), top - CMP_STRIDE * off - (CMP_BLOCK - 1)
        fc_ref[0, c * Q_BLOCK:(c + 1) * Q_BLOCK, :] = tile_values(dist, dmin, dmax, 0.0)
    for c in range(DN_WIDTH // LANES):
        off = NEAR_BACK - c * LANES
        dist = ql + off - ln
        dn_ref[0, :, c * LANES:(c + 1) * LANES] = tile_values(dist, off - top, off + top, far)


def _bias_tables(thr, rb_flat, nc):
    return pl.pallas_call(
        functools.partial(_tables_kernel, nc=nc),
        out_shape=(jax.ShapeDtypeStruct((HEADS, 2 * nc, Q_BLOCK), F32),
                   jax.ShapeDtypeStruct((HEADS, Q_BLOCK, DN_WIDTH), F32)),
        grid=(HEADS,),
        in_specs=[pl.BlockSpec(memory_space=pltpu.SMEM), pl.BlockSpec(memory_space=pltpu.SMEM)],
        out_specs=(pl.BlockSpec((1, 2 * nc, Q_BLOCK), lambda h: (h, 0, 0)),
                   pl.BlockSpec((1, Q_BLOCK, DN_WIDTH), lambda h: (h, 0, 0))),
        name="bias_tables",
    )(thr, rb_flat)


def _proj_kernel(x_ref, wuv_ref, wqkv_ref, lng_ref, lnb_ref, ws_ref, bs_ref,
                 a_ref, q_ref, cmp_ref, kaug_ref, vaug_ref, wink_ref, winv_ref, gate_ref, cmp_sc, *, tm, seq):
    xb = x_ref[...].astype(BF16)
    uv = jax.nn.gelu(_dot(xb, wuv_ref[...]))
    u = uv[:, :A_WIDTH]
    v = _layer_norm(uv[:, A_WIDTH:], lng_ref[...], lnb_ref[...]).astype(BF16)
    row = lax.broadcasted_iota(jnp.int32, (CHUNK, CHUNK), 0)
    col = lax.broadcasted_iota(jnp.int32, (CHUNK, CHUNK), 1)
    gd = A_WIDTH // A_GROUPS
    for g in range(A_GROUPS):
        w = jnp.where(col <= row, ws_ref[g], 0.0).astype(BF16)
        for c in range(tm // CHUNK):
            rs = slice(c * CHUNK, (c + 1) * CHUNK)
            cs = slice(g * gd, (g + 1) * gd)
            s = _dot(w, v[rs, cs]) + bs_ref[g]
            a_ref[rs, cs] = (u[rs, cs] * s).astype(BF16)

    h = _dot(xb, wqkv_ref[...])
    q_ref[...] = (h[:, 0:512] * (HEAD_DIM ** -0.5)).astype(BF16)
    for half in range(2):
        cmp_sc[half] = h[:, 512 + half * LANES:512 + (half + 1) * LANES]
        for l in range(CMP_STRIDE):
            rows = cmp_sc[half, pl.ds(l, tm // CMP_STRIDE, stride=CMP_STRIDE), :]
            cmp_ref[:, l * 256 + half * LANES:l * 256 + (half + 1) * LANES] = rows.astype(BF16)
    kaug_ref[:, 0:LANES] = h[:, 768:896].astype(BF16)
    pos = (pl.program_id(0) * tm) % seq + lax.broadcasted_iota(jnp.int32, (tm, LANES), 0)
    blk = lax.broadcasted_iota(jnp.int32, (tm, LANES), 1)
    kaug_ref[:, LANES:2 * LANES] = jnp.where(pos // SEL_BLOCK == blk, 1.0, 0.0).astype(BF16)
    ones = jnp.ones((tm, LANES), BF16)
    vaug_ref[:, 0:LANES] = h[:, 896:1024].astype(BF16)
    vaug_ref[:, LANES:2 * LANES] = ones
    wink_ref[...] = h[:, 1024:1152].astype(BF16)
    winv_ref[:, 0:LANES] = h[:, 1152:1280].astype(BF16)
    winv_ref[:, LANES:2 * LANES] = ones
    gate_ref[...] = jax.nn.sigmoid(h[:, 1280:1408])


def _proj(x2d, wuv, wqkv, lng, lnb, ws, bs, *, seq, tm=512):
    t = x2d.shape[0]
    row = lambda i: (i, 0)
    const2 = lambda i: (0, 0)
    const3 = lambda i: (0, 0, 0)
    outs = [(1, A_WIDTH, BF16), (1, 512, BF16), (CMP_STRIDE, CMP_STRIDE * 256, BF16), (1, 256, BF16),
            (1, 256, BF16), (1, 128, BF16), (1, 256, BF16), (1, 128, F32)]
    return pl.pallas_call(
        functools.partial(_proj_kernel, tm=tm, seq=seq),
        out_shape=tuple(jax.ShapeDtypeStruct((t // r, w), dt) for r, w, dt in outs),
        grid=(t // tm,),
        in_specs=[pl.BlockSpec((tm, D_MODEL), row),
                  pl.BlockSpec(wuv.shape, const2),
                  pl.BlockSpec(wqkv.shape, const2),
                  pl.BlockSpec(lng.shape, const2),
                  pl.BlockSpec(lnb.shape, const2),
                  pl.BlockSpec(ws.shape, const3),
                  pl.BlockSpec(bs.shape, const3)],
        out_specs=tuple(pl.BlockSpec((tm // r, w), row) for r, w, _ in outs),
        scratch_shapes=[pltpu.VMEM((2, tm, LANES), F32)],
        compiler_params=pltpu.CompilerParams(dimension_semantics=("parallel",), vmem_limit_bytes=VMEM_LIMIT),
        name="in_proj_gmlp",
    )(x2d, wuv, wqkv, lng, lnb, ws, bs)


def _compress_kernel(r_ref, wa_ref, wb_ref, pea_ref, peb_ref, w2_ref, kc_ref, vct_ref):
    r = r_ref[0]
    top = _dot(r, wa_ref[...])
    bot = _dot(r, wb_ref[...])
    pe = _dot(pea_ref[...], wa_ref[...]) + _dot(peb_ref[...], wb_ref[...])
    hid = top + pltpu.roll(bot, bot.shape[0] - 1, 0) + pe[0:1]
    out = _dot(jax.nn.gelu(hid).astype(BF16), w2_ref[...])
    kc_ref[0] = out[:, 0:LANES].astype(BF16)
    vct_ref[0] = out[:, LANES:2 * LANES].T.astype(BF16)


def _compress(cmp3, wa, wb, pea, peb, w2):
    b, nc, width = cmp3.shape
    const2 = lambda i: (0, 0)
    return pl.pallas_call(
        _compress_kernel,
        out_shape=(jax.ShapeDtypeStruct((b, nc, LANES), BF16), jax.ShapeDtypeStruct((b, LANES, nc), BF16)),
        grid=(b,),
        in_specs=[pl.BlockSpec((1, nc, width), lambda i: (i, 0, 0)),
                  pl.BlockSpec(wa.shape, const2), pl.BlockSpec(wb.shape, const2),
                  pl.BlockSpec(pea.shape, const2), pl.BlockSpec(peb.shape, const2),
                  pl.BlockSpec(w2.shape, const2)],
        out_specs=(pl.BlockSpec((1, nc, LANES), lambda i: (i, 0, 0)),
                   pl.BlockSpec((1, LANES, nc), lambda i: (i, 0, 0))),
        compiler_params=pltpu.CompilerParams(dimension_semantics=("parallel",), vmem_limit_bytes=VMEM_LIMIT),
        name="kv_compress",
    )(cmp3, wa, wb, pea, peb, w2)


def _softmax_tile(s, m_sc):
    cols = [s[:, j * LANES:(j + 1) * LANES] for j in range(s.shape[1] // LANES)]
    m_old = m_sc[...]
    m_new = jnp.maximum(m_old, jnp.max(functools.reduce(jnp.maximum, cols), axis=-1, keepdims=True))
    m_sc[...] = m_new
    p = jnp.concatenate([jnp.exp(c - m_new) for c in cols], axis=1).astype(BF16)
    return p, jnp.exp(m_old - m_new)


def _scale_both(acc, alpha):
    return jnp.concatenate([acc[:, :LANES] * alpha, acc[:, LANES:] * alpha], axis=1)


def _nsa_kernel(q_ref, gate_ref, kc_ref, vct_ref, kaug_ref, vaug_ref, wink_ref, winv_ref, fc_ref, dn_ref, ovt_ref,
                o_ref, qa_all, s_all, p_all, m_all, acc_all, *, nc, n_tiles):
    ib = pl.program_id(1)
    t0 = ib * Q_BLOCK
    lane = lax.broadcasted_iota(jnp.int32, (Q_BLOCK, LANES), 1)
    qrow = lax.broadcasted_iota(jnp.int32, (Q_BLOCK, LANES), 0)
    q = q_ref[...]
    gates = gate_ref[...]
    kc = kc_ref[0]
    vct = vct_ref[0]
    ovt = ovt_ref[...]

    def stack_heads(fn):
        return jnp.concatenate([fn(g) for g in range(GROUP)], axis=0)

    near0 = 2 * ib - NEAR_BACK // SEL_BLOCK
    n_far = jnp.maximum(ib - 4, 0) // 4
    n_pairs = (n_far + 1) // 2
    kstart = jnp.maximum(t0 - NEAR_BACK, 0)
    j0 = pl.multiple_of(kstart - (t0 - NEAR_BACK), LANES)
    kstart = pl.multiple_of(kstart, LANES)
    wstart = jnp.maximum(t0 - WINDOW, 0)
    jw = pl.multiple_of(wstart - (t0 - NEAR_BACK), LANES)
    wstart = pl.multiple_of(wstart, LANES)
    win_edge = jnp.where((lane > qrow) | (t0 < WINDOW), 0.0, NEG_INF)

    fc_row = pl.multiple_of((nc - 8) - 8 * ib, 8)
    qcol = lax.broadcasted_iota(jnp.int32, (1, ROWS), 1) % Q_BLOCK
    has_cmp = jnp.where(t0 + qcol >= CMP_BLOCK - 1, 1.0, 0.0)
    qhs, o_cts, imps = [], [], []
    for hkv in range(KV_HEADS):
        mine = (lane >= HEAD_DIM) if hkv else (lane < HEAD_DIM)
        qh = stack_heads(lambda g: jnp.where(mine, q[:, g * LANES:(g + 1) * LANES], 0.0).astype(BF16))
        lct = _dot_nt(kc, qh)
        lct = lct + jnp.concatenate([fc_ref[hkv * GROUP + g, pl.ds(fc_row, nc), :] for g in range(GROUP)], axis=1)
        e = jnp.exp(lct - jnp.max(lct, axis=0, keepdims=True))
        pct = e * (has_cmp / jnp.sum(e, axis=0, keepdims=True))
        o_cts.append(_dot(vct, pct.astype(BF16)))
        psum = pct[:, 0:LANES] + pct[:, LANES:2 * LANES] + pct[:, 2 * LANES:3 * LANES] + pct[:, 3 * LANES:]
        p_hi = psum.astype(BF16)
        p_lo = (psum - p_hi.astype(F32)).astype(BF16)
        imps.append(_dot(ovt, p_hi) + _dot(ovt, p_lo))
        qhs.append(qh)

    blk = lax.broadcasted_iota(jnp.int32, (LANES, KV_HEADS * Q_BLOCK), 0)
    col = lax.broadcasted_iota(jnp.int32, (LANES, KV_HEADS * Q_BLOCK), 1)
    jq = 2 * ib + jnp.where(col % Q_BLOCK >= SEL_BLOCK, 1, 0)
    forced = (blk == 0) | (blk == jq) | (blk == jq - 1)
    cand = jnp.where(forced, -3e38, jnp.where(blk > jq, NEG_INF, jnp.concatenate(imps, axis=1)))
    blk_f = blk.astype(F32)

    def pick_one(_, carry):
        cur, chosen = carry
        mx = jnp.max(cur, axis=0, keepdims=True)
        first = jnp.min(jnp.where(cur == mx, blk_f, float(LANES)), axis=0, keepdims=True)
        pick = blk_f == first
        return jnp.where(pick, -3e38, cur), jnp.where(pick, 1.0, chosen)

    _, sel_t = lax.fori_loop(0, N_SELECT - 3, pick_one, (cand, jnp.where(forced, 1.0, 0.0)), unroll=True)

    def window_branch(hkv):
        sw = _dot_nt(qhs[hkv], wink_ref[pl.ds(wstart, WIN_KEYS), :])
        sw = sw + stack_heads(lambda g: dn_ref[hkv * GROUP + g, :, pl.ds(jw, WIN_KEYS)])
        sw = jnp.concatenate([sw[:, :LANES] + jnp.concatenate([win_edge] * GROUP, axis=0), sw[:, LANES:]], axis=1)
        pw = jnp.exp(sw - jnp.max(sw, axis=-1, keepdims=True)).astype(BF16)
        ow = _dot(pw, winv_ref[pl.ds(wstart, WIN_KEYS), :])
        return ow[:, :LANES] / ow[:, LANES:]

    o_w_first = window_branch(0)

    outs = []
    for hkv in range(KV_HEADS):
        qa_sc, s_buf, p_buf = qa_all.at[hkv], s_all.at[hkv], p_all.at[hkv]
        m_sc, acc_sc = m_all.at[hkv], acc_all.at[hkv]
        qh = qhs[hkv]
        sel = sel_t[:, hkv * Q_BLOCK:(hkv + 1) * Q_BLOCK].T

        m_near = jnp.where(sel > 0, 0.0, NEG_INF).astype(BF16)
        m_far = jnp.where((sel > 0) & (lane < near0), 0.0, NEG_INF).astype(BF16)
        qa_sc[0, :, 0:LANES] = qh
        qa_sc[0, :, LANES:2 * LANES] = jnp.concatenate([m_far] * GROUP, axis=0)
        qa_sc[1, :, 0:LANES] = qh
        qa_sc[1, :, LANES:2 * LANES] = jnp.concatenate([m_near] * GROUP, axis=0)
        m_sc[...] = jnp.full(m_sc.shape, -jnp.inf, F32)
        acc_sc[...] = jnp.zeros(acc_sc.shape, F32)
        p_buf[1] = jnp.zeros((ROWS, FAR_TILE), BF16)

        def tile_keys(tile):
            near_idx = tile - 2 * n_pairs
            ks = jnp.where(near_idx >= 0, kstart + near_idx * FAR_TILE, tile * FAR_TILE)
            ks = jnp.clip(ks, 0, (n_tiles - 1) * FAR_TILE)
            return pl.multiple_of(ks, LANES), jnp.where(near_idx >= 0, 1, 0)

        def logits(tile, slot):
            ks, variant = tile_keys(tile)
            s_buf[slot] = _dot_nt(qa_sc[variant], kaug_ref[pl.ds(ks, FAR_TILE), :])

        def values(tile, slot):
            ks, _ = tile_keys(tile)
            return _dot(p_buf[slot], vaug_ref[pl.ds(ks, FAR_TILE), :])

        def absorb(slot, pending, table_col=None):
            s = s_buf[slot]
            if table_col is not None:
                js = pl.multiple_of(table_col, LANES)
                s = s + stack_heads(lambda g: dn_ref[hkv * GROUP + g, :, pl.ds(js, FAR_TILE)])
            p, alpha = _softmax_tile(s, m_sc)
            p_buf[slot] = p
            acc_sc[...] = _scale_both(acc_sc[...] + pending, alpha)

        logits(0, 0)

        o_w = o_w_first if hkv == 0 else window_branch(hkv)

        def far_pair(i, carry):
            t = 2 * i
            logits(t + 1, 1)
            absorb(0, values(t - 1, 1))
            logits(t + 2, 0)
            absorb(1, values(t, 0))
            return carry

        lax.fori_loop(0, n_pairs, far_pair, 0)
        t = 2 * n_pairs
        logits(t + 1, 1)
        absorb(0, values(t - 1, 1), j0)
        absorb(1, values(t, 0), j0 + FAR_TILE)
        acc = acc_sc[...] + values(t + 1, 1)
        o_s = acc[:, :LANES] / acc[:, LANES:]

        per_head = []
        for g in range(GROUP):
            c = 3 * (hkv * GROUP + g)
            rs = slice(g * Q_BLOCK, (g + 1) * Q_BLOCK)
            o_c = o_cts[hkv][:, rs].T
            per_head.append(gates[:, c:c + 1] * o_c + gates[:, c + 1:c + 2] * o_s[rs]
                            + gates[:, c + 2:c + 3] * o_w[rs])
        outs.append(per_head)

    for g in range(GROUP):
        o_ref[:, g * LANES:(g + 1) * LANES] = jnp.where(lane < HEAD_DIM, outs[0][g], outs[1][g]).astype(BF16)


def _nsa(qs, gates, kc, vct, kaug, vaug, wink, winv, fc, dn, ovt, *, batch, seq):
    nc = seq // CMP_STRIDE
    nq = seq // Q_BLOCK
    qrow = lambda b, i: (b * nq + i, 0)
    per_batch = lambda b, i: (b, 0)
    const2 = lambda b, i: (0, 0)
    const3 = lambda b, i: (0, 0, 0)
    once = pl.Buffered(1)
    return pl.pallas_call(
        functools.partial(_nsa_kernel, nc=nc, n_tiles=seq // FAR_TILE),
        out_shape=jax.ShapeDtypeStruct((batch * seq, 4 * LANES), BF16),
        grid=(batch, nq),
        in_specs=[pl.BlockSpec((Q_BLOCK, 4 * LANES), qrow),
                  pl.BlockSpec((Q_BLOCK, LANES), qrow),
                  pl.BlockSpec((1, nc, LANES), lambda b, i: (b, 0, 0)),
                  pl.BlockSpec((1, LANES, nc), lambda b, i: (b, 0, 0)),
                  pl.BlockSpec((seq, 256), per_batch, pipeline_mode=once),
                  pl.BlockSpec((seq, 256), per_batch, pipeline_mode=once),
                  pl.BlockSpec((seq, LANES), per_batch, pipeline_mode=once),
                  pl.BlockSpec((seq, 256), per_batch, pipeline_mode=once),
                  pl.BlockSpec(fc.shape, const3, pipeline_mode=once),
                  pl.BlockSpec(dn.shape, const3, pipeline_mode=once),
                  pl.BlockSpec(ovt.shape, const2)],
        out_specs=pl.BlockSpec((Q_BLOCK, 4 * LANES), qrow),
        scratch_shapes=[pltpu.VMEM((KV_HEADS, 2, ROWS, 2 * LANES), BF16),
                        pltpu.VMEM((KV_HEADS, 2, ROWS, FAR_TILE), F32),
                        pltpu.VMEM((KV_HEADS, 2, ROWS, FAR_TILE), BF16),
                        pltpu.VMEM((KV_HEADS, ROWS, LANES), F32),
                        pltpu.VMEM((KV_HEADS, ROWS, 2 * LANES), F32)],
        compiler_params=pltpu.CompilerParams(dimension_semantics=("parallel", "arbitrary"),
                                             vmem_limit_bytes=VMEM_LIMIT),
        name="sparse_attention",
    )(qs, gates, kc, vct, kaug, vaug, wink, winv, fc, dn, ovt)


def _swiglu_norm(x, wg_ref, wu_ref, wd_ref, g_ref, beta_ref, acc_ref, hc):
    xb = x.astype(BF16)
    hidden = wg_ref.shape[1]
    for c in range(hidden // hc):
        cs = slice(c * hc, (c + 1) * hc)
        gate = _dot(xb, wg_ref[:, cs])
        up = _dot(xb, wu_ref[:, cs])
        part = _dot((jax.nn.silu(gate) * up).astype(BF16), wd_ref[cs, :])
        if c == 0:
            acc_ref[...] = part
        else:
            acc_ref[...] += part
    return _layer_norm(ALPHA * x + acc_ref[...], g_ref[...], beta_ref[...])


def _load_ffn_weights(wg_hbm, wu_hbm, wd_hbm, wg_ref, wu_ref, wd_ref, stage_col, stage_row, sem, hc):
    n = wg_ref.shape[1] // hc
    jobs = [(wg_hbm, wg_ref, True, c) for c in range(n)] + [(wu_hbm, wu_ref, True, c) for c in range(n)] \
        + [(wd_hbm, wd_ref, False, c) for c in range(n)]

    def chunk_copy(j):
        src, _, by_col, c = jobs[j]
        slot = j % STAGE_SLOTS
        if by_col:
            return pltpu.make_async_copy(src.at[:, pl.ds(c * hc, hc)], stage_col.at[slot], sem.at[slot])
        return pltpu.make_async_copy(src.at[pl.ds(c * hc, hc), :], stage_row.at[slot], sem.at[slot])

    ahead = STAGE_SLOTS - 1
    for j in range(min(ahead, len(jobs))):
        chunk_copy(j).start()
    for j, (_, dst, by_col, c) in enumerate(jobs):
        if j + ahead < len(jobs):
            chunk_copy(j + ahead).start()
        chunk_copy(j).wait()
        if by_col:
            dst[:, c * hc:(c + 1) * hc] = stage_col[j % STAGE_SLOTS].astype(BF16)
        else:
            dst[c * hc:(c + 1) * hc, :] = stage_row[j % STAGE_SLOTS].astype(BF16)


def _outproj_ffn_kernel(x_ref, a_ref, b_ref, wa_ref, wb_ref, g1_ref, beta1_ref,
                        wg_hbm, wu_hbm, wd_hbm, g2_ref, beta2_ref, o_ref,
                        acc_ref, wg_ref, wu_ref, wd_ref, stage_col, stage_row, sem, *, hc):
    @pl.when(pl.program_id(0) == 0)
    def _():
        _load_ffn_weights(wg_hbm, wu_hbm, wd_hbm, wg_ref, wu_ref, wd_ref, stage_col, stage_row, sem, hc)

    y = _dot(a_ref[...], wa_ref[...]) + _dot(b_ref[...], wb_ref[...])
    x1 = _layer_norm(ALPHA * x_ref[...] + y, g1_ref[...], beta1_ref[...])
    o_ref[...] = _swiglu_norm(x1, wg_ref, wu_ref, wd_ref, g2_ref, beta2_ref, acc_ref, hc)


def _outproj_ffn(x2d, a, b, wa, wb, g1, beta1, wg, wu, wd, g2, beta2, *, tm=512, hc=256):
    t = x2d.shape[0]
    row = lambda i: (i, 0)
    const2 = lambda i: (0, 0)
    resident = lambda c: pl.BlockSpec(c.shape, const2, pipeline_mode=pl.Buffered(1))
    in_hbm = pl.BlockSpec(memory_space=pl.ANY)
    hidden = wg.shape[1]
    return pl.pallas_call(
        functools.partial(_outproj_ffn_kernel, hc=hc),
        out_shape=jax.ShapeDtypeStruct((t, D_MODEL), F32),
        grid=(t // tm,),
        in_specs=[pl.BlockSpec((tm, D_MODEL), row), pl.BlockSpec((tm, a.shape[1]), row),
                  pl.BlockSpec((tm, b.shape[1]), row), resident(wa), resident(wb), resident(g1), resident(beta1),
                  in_hbm, in_hbm, in_hbm, resident(g2), resident(beta2)],
        out_specs=pl.BlockSpec((tm, D_MODEL), row),
        scratch_shapes=[pltpu.VMEM((tm, D_MODEL), F32),
                        pltpu.VMEM((D_MODEL, hidden), BF16), pltpu.VMEM((D_MODEL, hidden), BF16),
                        pltpu.VMEM((hidden, D_MODEL), BF16),
                        pltpu.VMEM((STAGE_SLOTS, D_MODEL, hc), F32), pltpu.VMEM((STAGE_SLOTS, hc, D_MODEL), F32),
                        pltpu.SemaphoreType.DMA((STAGE_SLOTS,))],
        compiler_params=pltpu.CompilerParams(dimension_semantics=("arbitrary",), vmem_limit_bytes=VMEM_LIMIT),
        name="out_proj_swiglu_ffn_norm",
    )(x2d, a, b, wa, wb, g1, beta1, wg, wu, wd, g2, beta2)


def _conv_kernel(x_ref, win_ref, bin_ref, dww_ref, dwb_ref, lng_ref, lnb_ref, wout_ref, bout_ref,
                 g_ref, beta_ref, wg_hbm, wu_hbm, wd_hbm, g2_ref, beta2_ref, o_ref, buf_ref, cv_ref,
                 wg_ref, wu_ref, wd_ref, stage_col, stage_row, sem, *, tm, rc, hc):
    @pl.when((pl.program_id(0) == 0) & (pl.program_id(1) == 0))
    def _():
        _load_ffn_weights(wg_hbm, wu_hbm, wd_hbm, wg_ref, wu_ref, wd_ref, stage_col, stage_row, sem, hc)

    @pl.when(pl.program_id(1) == 0)
    def _():
        buf_ref[0:HALO, :] = jnp.zeros((HALO, D_MODEL), F32)

    x = x_ref[...]
    h = _dot(x.astype(BF16), win_ref[...]) + bin_ref[...]
    buf_ref[HALO:HALO + tm, :] = h[:, :D_MODEL] * jax.nn.sigmoid(h[:, D_MODEL:])

    lead = HALO - (CONV_WIDTH - 1)

    def conv_rows(r, carry):
        r0 = pl.multiple_of(r * rc, rc)
        for lc in range(D_MODEL // LANES):
            ls = slice(lc * LANES, (lc + 1) * LANES)
            acc = jnp.broadcast_to(dwb_ref[:, ls], (rc, LANES))
            za = buf_ref[pl.ds(r0, rc + HALO), ls]
            for b in range(8):
                base, shift = 8 * ((lead + b) // 8), (lead + b) % 8
                span = rc + (8 if shift else 0)
                yb = None
                for a, j in enumerate(range(b, CONV_WIDTH, 8)):
                    term = za[base + 8 * a:base + 8 * a + span] * dww_ref[j:j + 1, ls]
                    yb = term if yb is None else yb + term
                acc = acc + yb[shift:shift + rc]
            cv_ref[pl.ds(r0, rc), ls] = acc
        return carry

    lax.fori_loop(0, tm // rc, conv_rows, 0)
    buf_ref[0:HALO, :] = buf_ref[tm:tm + HALO, :]

    y = jax.nn.silu(_layer_norm(cv_ref[...], lng_ref[...], lnb_ref[...]))
    y = _dot(y.astype(BF16), wout_ref[...]) + bout_ref[...]
    x1 = _layer_norm(ALPHA * x + y, g_ref[...], beta_ref[...])
    o_ref[...] = _swiglu_norm(x1, wg_ref, wu_ref, wd_ref, g2_ref, beta2_ref, cv_ref, hc)


def _conv_ffn(x2d, w_in, b_in, dw_w, dw_b, ln_g, ln_b, w_out, b_out, g, beta, wg, wu, wd, g2, beta2, *,
              batch, seq, tm=512, rc=256, hc=256):
    nt = seq // tm
    row = lambda b, i: (b * nt + i, 0)
    const2 = lambda b, i: (0, 0)
    resident = lambda c: pl.BlockSpec(c.shape, const2, pipeline_mode=pl.Buffered(1))
    in_hbm = pl.BlockSpec(memory_space=pl.ANY)
    before = (w_in, b_in, dw_w, dw_b, ln_g, ln_b, w_out, b_out, g, beta)
    hidden = wg.shape[1]
    return pl.pallas_call(
        functools.partial(_conv_kernel, tm=tm, rc=rc, hc=hc),
        out_shape=jax.ShapeDtypeStruct((batch * seq, D_MODEL), F32),
        grid=(batch, nt),
        in_specs=[pl.BlockSpec((tm, D_MODEL), row)] + [resident(c) for c in before]
        + [in_hbm, in_hbm, in_hbm, resident(g2), resident(beta2)],
        out_specs=pl.BlockSpec((tm, D_MODEL), row),
        scratch_shapes=[pltpu.VMEM((HALO + tm, D_MODEL), F32), pltpu.VMEM((tm, D_MODEL), F32),
                        pltpu.VMEM((D_MODEL, hidden), BF16), pltpu.VMEM((D_MODEL, hidden), BF16),
                        pltpu.VMEM((hidden, D_MODEL), BF16),
                        pltpu.VMEM((STAGE_SLOTS, D_MODEL, hc), F32), pltpu.VMEM((STAGE_SLOTS, hc, D_MODEL), F32),
                        pltpu.SemaphoreType.DMA((STAGE_SLOTS,))],
        compiler_params=pltpu.CompilerParams(dimension_semantics=("arbitrary", "arbitrary"),
                                             vmem_limit_bytes=VMEM_LIMIT),
        name="conv_module_swiglu_ffn_norm",
    )(x2d, *before, wg, wu, wd, g2, beta2)


def _head_perm():
    p = np.arange(HEADS * HEAD_DIM)
    g, half, d = p // LANES, (p % LANES) // HEAD_DIM, p % HEAD_DIM
    return (g + GROUP * half) * HEAD_DIM + d


def _compress_weights(pe_k, w1_k, w2_k, pe_v, w1_v, w2_v):
    def block_diag(compact):
        row_c = (jnp.arange(compact.shape[0]) % 256) // HEAD_DIM
        col_c = jnp.arange(256) // HEAD_DIM
        return jnp.where(row_c[:, None] == col_c[None, :], jnp.tile(compact, (1, 4)), 0.0).astype(BF16)

    w1k = w1_k.reshape(CMP_BLOCK, 1, HEAD_DIM, HEAD_DIM)
    w1v = w1_v.reshape(CMP_BLOCK, 1, HEAD_DIM, HEAD_DIM)
    w1x = block_diag(jnp.concatenate([w1k, w1k, w1v, w1v], axis=1).reshape(CMP_BLOCK * 256, HEAD_DIM))
    w2x = block_diag(jnp.concatenate([w2_k, w2_k, w2_v, w2_v], axis=0))
    half = CMP_STRIDE * 256
    pe = jnp.stack([pe_k, pe_v])
    pex = jnp.broadcast_to(pe.transpose(1, 0, 2)[:, :, None, :], (CMP_BLOCK, 2, KV_HEADS, HEAD_DIM))
    pex = pex.reshape(2, half)
    pea = jnp.broadcast_to(pex[0:1], (8, half)).astype(BF16)
    peb = jnp.broadcast_to(pex[1:2], (8, half)).astype(BF16)
    return w1x[:half], w1x[half:], pea, peb, w2x


def _overlap_matrix(nc):
    c0 = np.arange(nc)[None, :] * CMP_STRIDE
    s0 = np.arange(LANES)[:, None] * SEL_BLOCK
    ov = (c0 < s0 + SEL_BLOCK) & (c0 + CMP_BLOCK > s0) & (np.arange(nc)[None, :] < nc - 1)
    return jnp.asarray(ov, dtype=BF16)


def _even_layer(x2d, rel_bias, w_in, w_out, ln_g, ln_b, w_s, b_s, pe_k, w1_k, w2_k, pe_v, w1_v, w2_v,
                norm_g, norm_b, *, batch, seq):
    nc = seq // CMP_STRIDE
    assert seq % FAR_TILE == 0 and seq >= NEAR and seq // SEL_BLOCK <= LANES
    perm = _head_perm()
    qw = HEADS * HEAD_DIM
    o = 2 * A_WIDTH
    wuv = w_in[:, :o].astype(BF16)
    wq = w_in[:, o:o + qw][:, perm]
    wkv = w_in[:, o + qw:o + qw + 768]
    wgt = jnp.pad(w_in[:, o + qw + 768:], ((0, 0), (0, LANES - 3 * HEADS)))
    wqkv = jnp.concatenate([wq, wkv, wgt], axis=1).astype(BF16)
    bs = jnp.broadcast_to(b_s[:, :, None], (A_GROUPS, CHUNK, A_WIDTH // A_GROUPS))
    a_out, qs, cmp2, kaug, vaug, wink, winv, gates = _proj(
        x2d, wuv, wqkv, ln_g[None, :], ln_b[None, :], w_s, bs, seq=seq)

    kc, vct = _compress(cmp2.reshape(batch, nc, CMP_STRIDE * 256),
                    *_compress_weights(pe_k, w1_k, w2_k, pe_v, w1_v, w2_v))

    fc, dn = _bias_tables(_bucket_thresholds(seq), rel_bias.T.reshape(-1), nc)
    b_out = _nsa(qs, gates, kc, vct, kaug, vaug, wink, winv, fc, dn, _overlap_matrix(nc), batch=batch, seq=seq)

    wo_a = w_out[:A_WIDTH].astype(BF16)
    wo_b = w_out[A_WIDTH:][perm].astype(BF16)
    return a_out, b_out, wo_a, wo_b, norm_g[None, :], norm_b[None, :]


def kernel(x, rel_bias, hyb_w_in, hyb_w_out, gmlp_ln_g, gmlp_ln_b, gmlp_w_s, gmlp_b_s, cmp_pe_k, cmp_w1_k, cmp_w2_k, cmp_pe_v, cmp_w1_v, cmp_w2_v, conv_w_in, conv_b_in, conv_dw_w, conv_dw_b, conv_ln_g, conv_ln_b, conv_w_out, conv_b_out, ffn_w_gate, ffn_w_up, ffn_w_down, norm_mix_g, norm_mix_b, norm_ffn_g, norm_ffn_b):
    batch, seq, d = x.shape
    h = x.reshape(batch * seq, d)
    for layer in range(DEPTH):
        i = layer // 2
        norms = (norm_ffn_g[layer][None, :], norm_ffn_b[layer][None, :])
        ffn = (ffn_w_gate[layer].astype(BF16), ffn_w_up[layer].astype(BF16), ffn_w_down[layer].astype(BF16)) + norms
        if layer % 2 == 0:
            mixer = _even_layer(h, rel_bias, hyb_w_in[i], hyb_w_out[i], gmlp_ln_g[i], gmlp_ln_b[i],
                                gmlp_w_s[i], gmlp_b_s[i], cmp_pe_k[i], cmp_w1_k[i], cmp_w2_k[i],
                                cmp_pe_v[i], cmp_w1_v[i], cmp_w2_v[i],
                                norm_mix_g[layer], norm_mix_b[layer], batch=batch, seq=seq)
            h = _outproj_ffn(h, *mixer, ffn_w_gate[layer], ffn_w_up[layer], ffn_w_down[layer], *norms)
        else:
            h = _conv_ffn(h, conv_w_in[i].astype(BF16), conv_b_in[i][None, :], conv_dw_w[i], conv_dw_b[i][None, :],
                          conv_ln_g[i][None, :], conv_ln_b[i][None, :], conv_w_out[i].astype(BF16),
                          conv_b_out[i][None, :], norm_mix_g[layer][None, :], norm_mix_b[layer][None, :],
                          ffn_w_gate[layer], ffn_w_up[layer], ffn_w_down[layer], *norms, batch=batch, seq=seq)
    return h.reshape(batch, seq, d)
```
